```python
import math, functools
import jax, jax.numpy as jnp
from jax import lax
import numpy as np

D_MODEL = 2048
BATCH = 4
SEQ = 2048
DEPTH = 2
DEC_BATCH = 128
DEC_SEQ = 4
PAST_LEN = 16384
PAGE_SIZE = 128

N_META = 16
NORM_EPS = 1e-6
A_WIDTH = D_MODEL
A_HEAD = 64
A_HEADS = A_WIDTH // A_HEAD
DECAY_LORA = max(32, int(round(1.8 * D_MODEL ** 0.5 / 32)) * 32)
AAA_LORA = max(32, int(round(1.8 * D_MODEL ** 0.5 / 32)) * 32)
LNX_EPS = 64e-5
B_HEADS = 4
B_KDIM = D_MODEL // 2
B_VDIM = D_MODEL
B_HK = B_KDIM // B_HEADS
B_HV = B_VDIM // B_HEADS
GATE_LORA = 16
GATE_TAU = 16.0
CHUNK = 64
SHIFT_COLS = 3 * A_WIDTH + DECAY_LORA + AAA_LORA
SHIFT_SPLITS = (A_WIDTH, 2 * A_WIDTH, 3 * A_WIDTH, 3 * A_WIDTH + DECAY_LORA)
REST_SIZES = (A_WIDTH, B_KDIM, B_KDIM, B_VDIM, GATE_LORA, B_VDIM, D_MODEL, D_MODEL)
REST_SPLITS = tuple(int(i) for i in np.cumsum(REST_SIZES)[:-1])
N_COLS = SHIFT_COLS + sum(REST_SIZES)

kernel_name = "hybrid_rwkv7_gla_gated_decoder_step"


def _rms_norm(x, g):
    xf = x.astype(jnp.float32)
    y = xf * lax.rsqrt(jnp.mean(xf * xf, axis=-1, keepdims=True) + NORM_EPS)
    return (y * g.astype(jnp.float32)).astype(x.dtype)


def _rwkv7_branch(zs, za, state, buf, mu, w0, w2, a0, a2, k_k, k_a, r_k, lnx_w, lnx_b):
    n, t, _ = zs.shape
    f32 = jnp.float32
    zf = zs.astype(f32)
    prev = jnp.concatenate([buf[:, None, :].astype(f32), zf[:, :-1]], axis=1)
    xs = zf + (prev - zf) * mu.astype(f32)
    r, k, v, w_lo, a_lo = jnp.split(xs, SHIFT_SPLITS, axis=-1)
    w = -jax.nn.softplus(-(w0 + jnp.tanh(w_lo) @ w2)) - 0.5
    decay = jnp.exp(-jnp.exp(w))
    a = jax.nn.sigmoid(a0 + a_lo @ a2)
    hd = lambda u: u.reshape(n, t, A_HEADS, A_HEAD)
    kk = hd(k * k_k)
    kk = kk / jnp.maximum(jnp.sqrt(jnp.sum(kk * kk, axis=-1, keepdims=True)), 1e-12)
    k = k * (1.0 + (a - 1.0) * k_a)
    r, k, v, decay, a = hd(r), hd(k), hd(v), hd(decay), hd(a)
    a_vec = -kk
    b_vec = kk * a

    def step(S, inp):
        r_t, w_t, k_t, v_t, a_t, b_t = inp
        sa = jnp.einsum('nhvk,nhk->nhv', S, a_t)
        S = S * w_t[:, :, None, :] + sa[..., None] * b_t[:, :, None, :] + v_t[..., None] * k_t[:, :, None, :]
        return S, jnp.einsum('nhvk,nhk->nhv', S, r_t)

    seq = tuple(jnp.swapaxes(u, 0, 1) for u in (r, decay, k, v, a_vec, b_vec))
    s_fin, y = lax.scan(step, state.astype(f32), seq)
    y = jnp.swapaxes(y, 0, 1)
    mean = jnp.mean(y, axis=-1, keepdims=True)
    var = jnp.mean(jnp.square(y - mean), axis=-1, keepdims=True)
    y = ((y - mean) * lax.rsqrt(var + LNX_EPS)).reshape(n, t, A_WIDTH) * lnx_w + lnx_b
    bonus = jnp.sum(r * k * r_k.astype(f32), axis=-1, keepdims=True) * v
    y = (y + bonus.reshape(n, t, A_WIDTH)) * jax.nn.silu(za.astype(f32))
    return y.astype(zs.dtype), s_fin, zs[:, -1]


def _gla_branch(q, k, v, g_lo, zb, state, gk_w2, gk_b, norm_w, front):
    n, t, _ = q.shape
    f32 = jnp.float32
    g = jax.nn.log_sigmoid(g_lo.astype(f32) @ gk_w2 + gk_b) / GATE_TAU
    q = q.astype(f32) * (B_HK ** -0.5)
    total = front + t
    c = min(CHUNK, total)
    n_chunks = -(-total // c)
    back = n_chunks * c - total

    def prep(u, d):
        u = jnp.pad(u.astype(f32), ((0, 0), (front, back), (0, 0)))
        return u.reshape(n, n_chunks, c, B_HEADS, d).transpose(1, 0, 3, 2, 4)

    mask = jnp.tril(jnp.ones((c, c), dtype=bool))[:, :, None]

    def step(S, inp):
        qc, kc, vc, gc = inp
        b = jnp.cumsum(gc, axis=2)
        o = jnp.einsum('nhtk,nhkv->nhtv', qc * jnp.exp(b), S)
        diff = b[:, :, :, None, :] - b[:, :, None, :, :]
        dec = jnp.exp(jnp.where(mask, diff, -jnp.inf))
        att = jnp.einsum('nhtk,nhsk,nhtsk->nhts', qc, kc, dec)
        o = o + jnp.einsum('nhts,nhsv->nhtv', att, vc)
        b_last = b[:, :, -1:, :]
        S = S * jnp.exp(b_last)[:, :, 0, :, None] + jnp.einsum('nhsk,nhsv->nhkv', kc * jnp.exp(b_last - b), vc)
        return S, o

    s_fin, o = lax.scan(step, state.astype(f32), (prep(q, B_HK), prep(k, B_HK), prep(v, B_HV), prep(g, B_HK)))
    o = o.transpose(1, 0, 3, 2, 4).reshape(n, n_chunks * c, B_HEADS, B_HV)[:, front:front + t]
    o = o * lax.rsqrt(jnp.mean(o * o, axis=-1, keepdims=True) + NORM_EPS) * norm_w.astype(f32)
    o = o.reshape(n, t, B_VDIM) * jax.nn.silu(zb.astype(f32))
    return o.astype(zb.dtype), s_fin


def _trunk(x, st_rwkv, st_shift, st_gla, gla_front, params):
    (norm_pre, w_in, rwkv_mu, rwkv_w0, rwkv_w2, rwkv_a0, rwkv_a2, rwkv_k_k, rwkv_k_a, rwkv_r_k,
     rwkv_lnx_w, rwkv_lnx_b, gla_gk_w2, gla_gk_b, gla_norm_w, w_out, norm_post) = params
    new_r, new_s, new_g = [], [], []
    for l in range(DEPTH):
        h = _rms_norm(x, norm_pre[l])
        z = h @ w_in[l]
        zs = z[..., :SHIFT_COLS]
        za, qb, kb, vb, glo, zb, ga, gb = jnp.split(z[..., SHIFT_COLS:], REST_SPLITS, axis=-1)
        oa, sr, ss = _rwkv7_branch(zs, za, st_rwkv[l], st_shift[l], rwkv_mu[l], rwkv_w0[l], rwkv_w2[l],
                                   rwkv_a0[l], rwkv_a2[l], rwkv_k_k[l], rwkv_k_a[l], rwkv_r_k[l],
                                   rwkv_lnx_w[l], rwkv_lnx_b[l])
        ob, sg = _gla_branch(qb, kb, vb, glo, zb, st_gla[l], gla_gk_w2[l], gla_gk_b[l], gla_norm_w[l], gla_front)
        m = jax.nn.sigmoid(ga) * oa + jax.nn.sigmoid(gb) * ob
        x = x + _rms_norm(m @ w_out[l], norm_post[l])
        new_r.append(sr.astype(x.dtype))
        new_s.append(ss.astype(x.dtype))
        new_g.append(sg.astype(x.dtype))
    return x, jnp.stack(new_r), jnp.stack(new_s), jnp.stack(new_g)


def setup_inputs(seed: int = 0) -> dict:
    key = jax.random.key(seed)
    ks = jax.random.split(key, 24)
    nrm = lambda k, s, sc: jax.random.normal(k, s, jnp.float32) * sc
    D = D_MODEL
    return {
        "x_prompt": nrm(ks[0], (BATCH, SEQ, D), 1.0),
        "x_sample": nrm(ks[1], (DEC_BATCH, DEC_SEQ, D), 1.0),
        "state_rwkv": nrm(ks[2], (DEPTH, DEC_BATCH, A_HEADS, A_HEAD, A_HEAD), 0.5),
        "state_shift": nrm(ks[3], (DEPTH, DEC_BATCH, SHIFT_COLS), 1.0),
        "state_gla": nrm(ks[4], (DEPTH, DEC_BATCH, B_HEADS, B_HK, B_HV), 1.0),
        "meta_tokens": nrm(ks[5], (N_META, D), 1.0),
        "norm_pre": 1.0 + nrm(ks[6], (DEPTH, D), 0.05),
        "w_in": nrm(ks[7], (DEPTH, D, N_COLS), D ** -0.5),
        "rwkv_mu": jax.random.uniform(ks[8], (DEPTH, SHIFT_COLS), jnp.float32, 0.0, 1.0),
        "rwkv_w0": jax.random.uniform(ks[9], (DEPTH, A_WIDTH), jnp.float32, -6.0, 0.0),
        "rwkv_w2": nrm(ks[10], (DEPTH, DECAY_LORA, A_WIDTH), DECAY_LORA ** -0.5),
        "rwkv_a0": nrm(ks[11], (DEPTH, A_WIDTH), 0.1),
        "rwkv_a2": nrm(ks[12], (DEPTH, AAA_LORA, A_WIDTH), AAA_LORA ** -0.5),
        "rwkv_k_k": 0.85 + nrm(ks[13], (DEPTH, A_WIDTH), 0.05),
        "rwkv_k_a": 1.0 + nrm(ks[14], (DEPTH, A_WIDTH), 0.05),
        "rwkv_r_k": nrm(ks[15], (DEPTH, A_HEADS, A_HEAD), 0.1),
        "rwkv_lnx_w": 1.0 + nrm(ks[16], (DEPTH, A_WIDTH), 0.05),
        "rwkv_lnx_b": nrm(ks[17], (DEPTH, A_WIDTH), 0.01),
        "gla_gk_w2": nrm(ks[18], (DEPTH, GATE_LORA, B_KDIM), GATE_LORA ** -0.5),
        "gla_gk_b": nrm(ks[19], (DEPTH, B_KDIM), 0.1),
        "gla_norm_w": 1.0 + nrm(ks[20], (DEPTH, B_HV), 0.05),
        "w_out": nrm(ks[21], (DEPTH, D, D), D ** -0.5),
        "norm_post": 1.0 + nrm(ks[22], (DEPTH, D), 0.05),
    }


def reference(x_prompt, x_sample, state_rwkv, state_shift, state_gla, meta_tokens, norm_pre, w_in, rwkv_mu,
              rwkv_w0, rwkv_w2, rwkv_a0, rwkv_a2, rwkv_k_k, rwkv_k_a, rwkv_r_k, rwkv_lnx_w, rwkv_lnx_b,
              gla_gk_w2, gla_gk_b, gla_norm_w, w_out, norm_post):
    params = (norm_pre, w_in, rwkv_mu, rwkv_w0, rwkv_w2, rwkv_a0, rwkv_a2, rwkv_k_k, rwkv_k_a, rwkv_r_k,
              rwkv_lnx_w, rwkv_lnx_b, gla_gk_w2, gla_gk_b, gla_norm_w, w_out, norm_post)
    dt = x_prompt.dtype
    meta = jnp.broadcast_to(meta_tokens.astype(dt)[None], (x_prompt.shape[0], N_META, D_MODEL))
    xp = jnp.concatenate([meta, x_prompt], axis=1)
    bp = x_prompt.shape[0]
    z_rwkv = jnp.zeros((DEPTH, bp, A_HEADS, A_HEAD, A_HEAD), dt)
    z_shift = jnp.zeros((DEPTH, bp, SHIFT_COLS), dt)
    z_gla = jnp.zeros((DEPTH, bp, B_HEADS, B_HK, B_HV), dt)
    hp, p_rwkv, p_shift, p_gla = _trunk(xp, z_rwkv, z_shift, z_gla, CHUNK - N_META, params)
    y_prompt = hp[:, N_META:]
    y_sample, s_rwkv, s_shift, s_gla = _trunk(x_sample, state_rwkv, state_shift, state_gla, 0, params)
    return (y_prompt, y_sample, p_rwkv, p_shift, p_gla, s_rwkv, s_shift, s_gla)
```

```python
import functools

import jax
import jax.numpy as jnp
from jax import lax
from jax.experimental import pallas as pl
from jax.experimental.pallas import tpu as pltpu

F32 = jnp.float32
BF16 = jnp.bfloat16

D_MODEL = 2048
N_META = 16
NORM_EPS = 1e-6
LNX_EPS = 64e-5
A_HEAD = 64
A_HEADS = D_MODEL // A_HEAD
LORA = 96
B_HEADS = 4
B_HK = 256
B_HV = 512
GATE_LORA = 16
GATE_TAU = 16.0
CHUNK = 64
SAMPLE_ROWS = 8
GLA_SUB = 16

LANE = 128
NPAIR = D_MODEL // LANE
SLABS_PER_TILE = 8
TILE_N = SLABS_PER_TILE * LANE
SL_R, SL_K, SL_V, SL_ZA = 0, 16, 32, 48
SL_QB, SL_KB, SL_VB, SL_ZB = 64, 72, 80, 96
SL_GA, SL_GB = 112, 128
SL_WLO, SL_ALO, SL_GK = 144, 145, 146
NSLAB = 152
NCOLP = NSLAB * LANE
N_BUF_ROWS = 56
N_PAR_ROWS = 16
VMEM_LIMIT = 56 * 1024 * 1024

NN = ((1,), (0,))
NT = ((1,), (1,))
TN = ((0,), (0,))
HIGHEST = lax.Precision.HIGHEST


def _dot(a, b, dims=NN, precision=None):
    return lax.dot_general(a, b, (dims, ((), ())), precision=precision, preferred_element_type=F32)


def _mm(a, b, dims=NN):
    return _dot(a.astype(BF16), b.astype(BF16), dims)


def _mm_hi(a, b, dims=NN):
    return _dot(a, b, dims, precision=HIGHEST)


def _split2(x):
    hi = x.astype(BF16)
    lo = (x - hi.astype(F32)).astype(BF16)
    return hi, lo


def _split3(x):
    hi = x.astype(BF16)
    r1 = x - hi.astype(F32)
    mid = r1.astype(BF16)
    lo = (r1 - mid.astype(F32)).astype(BF16)
    return hi, mid, lo


def _sel_right(x, m01):
    hi, lo = _split2(x)
    return _dot(hi, m01) + _dot(lo, m01)


def _cumsum_rows(tri, x):
    hi, mid, lo = _split3(x)
    return _dot(tri, hi) + _dot(tri, mid) + _dot(tri, lo)


def _softplus(x):
    return jnp.maximum(x, 0.0) + jnp.log(1.0 + jnp.exp(-jnp.abs(x)))


def _sigmoid(x):
    return 1.0 / (1.0 + jnp.exp(-x))


def _silu(x):
    return x * _sigmoid(x)


def _log_sigmoid(x):
    return -_softplus(-x)


def _iota(shape, dim):
    return lax.broadcasted_iota(jnp.int32, shape, dim)


def _inproj_kernel(x_ref, g_ref, w_ref, o_ref, h_ref):
    @pl.when(pl.program_id(1) == 0)
    def _():
        x = x_ref[...]
        ms = jnp.mean(x * x, axis=-1, keepdims=True)
        h_ref[...] = (x * lax.rsqrt(ms + NORM_EPS) * g_ref[...]).astype(BF16)

    acc = jnp.dot(h_ref[...], w_ref[...], preferred_element_type=F32)
    for q in range(SLABS_PER_TILE):
        o_ref[q] = acc[:, q * LANE:(q + 1) * LANE]


def _inproj(x, g, w, tile_m):
    m = x.shape[0]
    return pl.pallas_call(
        _inproj_kernel,
        grid=(m // tile_m, NCOLP // TILE_N),
        in_specs=[
            pl.BlockSpec((tile_m, D_MODEL), lambda i, j: (i, 0)),
            pl.BlockSpec((1, D_MODEL), lambda i, j: (0, 0)),
            pl.BlockSpec((D_MODEL, TILE_N), lambda i, j: (0, j)),
        ],
        out_specs=pl.BlockSpec((SLABS_PER_TILE, tile_m, LANE), lambda i, j: (j, i, 0)),
        out_shape=jax.ShapeDtypeStruct((NSLAB, m, LANE), F32),
        scratch_shapes=[pltpu.VMEM((tile_m, D_MODEL), BF16)],
        compiler_params=pltpu.CompilerParams(
            dimension_semantics=("arbitrary", "arbitrary"), vmem_limit_bytes=VMEM_LIMIT),
        name="inproj",
    )(x, g, w)


def _rwkv_kernel(zr_ref, za_ref, lo_ref, par_ref, mulo_ref, w2_ref, a2_ref, buf_ref, sin_ref,
                 oa_ref, sout_ref, bd_ref, prev_ref, tw_ref, xa_ref, *, C, valid_lo, valid_hi, unroll):
    c = pl.program_id(1)
    H = A_HEAD

    @pl.when(c == 0)
    def _init():
        prev_ref[...] = buf_ref[0]
        z = jnp.zeros((H, H), F32)
        for p in range(NPAIR):
            se = sin_ref[0, 2 * p]
            so = sin_ref[0, 2 * p + 1]
            bd_ref[p] = jnp.concatenate(
                [jnp.concatenate([se, z], axis=1), jnp.concatenate([z, so], axis=1)], axis=0)

    row = _iota((C, LANE), 0)
    lane = _iota((C, LANE), 1)
    pos = c * C + row
    valid = (pos >= valid_lo) & (pos < valid_hi)
    first = pos == valid_lo
    left = lane < H

    def prev_rows(x, s):
        pr = prev_ref[pl.ds(s, 1), :]
        rolled = pltpu.roll(x, 1, axis=0)
        prv = jnp.where(row == 0, pr, rolled)
        return jnp.where(first, buf_ref[0, pl.ds(s, 1), :], prv)

    wl_raw = lo_ref[0]
    al_raw = lo_ref[1]
    pw = prev_rows(wl_raw, 3 * NPAIR)
    pa = prev_rows(al_raw, 3 * NPAIR + 1)
    tw_ref[...] = jnp.tanh(wl_raw + (pw - wl_raw) * mulo_ref[0:1, :])
    xa_ref[...] = al_raw + (pa - al_raw) * mulo_ref[1:2, :]
    prev_ref[3 * NPAIR:3 * NPAIR + 1, :] = wl_raw[C - 1:C]
    prev_ref[3 * NPAIR + 1:3 * NPAIR + 2, :] = al_raw[C - 1:C]

    gi = _iota((LANE, LANE), 0)
    gj = _iota((LANE, LANE), 1)
    seg01 = jnp.where((gi < H) == (gj < H), 1.0, 0.0).astype(BF16)
    ti = _iota((C, C), 0)
    tj = _iota((C, C), 1)
    tri = jnp.where(tj <= ti, 1.0, 0.0).astype(BF16)
    ri = _iota((2 * C, 2 * C), 0)
    ci = _iota((2 * C, 2 * C), 1)
    rt_ = jnp.where(ri >= C, ri - C, ri)
    ct_ = jnp.where(ci >= C, ci - C, ci)
    same = (ri >= C) == (ci >= C)
    strict = same & (ct_ < rt_)
    incl = same & (ct_ <= rt_)
    eye = jnp.where(ri == ci, 1.0, 0.0)
    n_dbl = C.bit_length() - 2

    def stack(x):
        return jnp.concatenate([jnp.where(left, x, 0.0), jnp.where(left, 0.0, x)], axis=0)

    def body(p, carry):
        par = par_ref[p]
        mu_r, mu_k, mu_v = par[0:1], par[1:2], par[2:3]
        w0, a0, k_k, k_a, r_k = par[3:4], par[4:5], par[5:6], par[6:7], par[7:8]
        lnw, lnb = par[8:9], par[9:10]

        rz = zr_ref[p]
        kz = zr_ref[NPAIR + p]
        vz = zr_ref[2 * NPAIR + p]
        r = rz + (prev_rows(rz, p) - rz) * mu_r
        k = kz + (prev_rows(kz, NPAIR + p) - kz) * mu_k
        v = vz + (prev_rows(vz, 2 * NPAIR + p) - vz) * mu_v
        prev_ref[pl.ds(p, 1), :] = rz[C - 1:C]
        prev_ref[pl.ds(NPAIR + p, 1), :] = kz[C - 1:C]
        prev_ref[pl.ds(2 * NPAIR + p, 1), :] = vz[C - 1:C]

        wl = w0 + _mm_hi(tw_ref[...], w2_ref[p])
        ld = -jnp.exp(-_softplus(-wl) - 0.5)
        al = _sigmoid(a0 + _mm_hi(xa_ref[...], a2_ref[p]))
        kkv = k * k_k
        ssq = _sel_right(kkv * kkv, seg01)
        kkn = kkv / jnp.maximum(jnp.sqrt(ssq), 1e-12)
        k2 = k * (1.0 + (al - 1.0) * k_a)
        bonus = _sel_right(r * k2 * r_k, seg01) * v

        ld = jnp.where(valid, ld, 0.0)
        k2m = jnp.where(valid, k2, 0.0)
        bvec = jnp.where(valid, kkn * al, 0.0)
        vm = jnp.where(valid, v, 0.0)

        L = _cumsum_rows(tri, ld)
        lref = L[C // 2 - 1:C // 2]
        e_fwd = jnp.exp(L - lref)
        e_bwd = jnp.exp(lref - L)
        at = -kkn * jnp.exp(L - ld - lref)
        rt = r * e_fwd
        bt = bvec * e_bwd
        kt = k2m * e_bwd
        e_mid = jnp.exp(lref)
        e_last = e_fwd[C - 1:C]
        p_last = jnp.exp(L[C - 1:C])

        r4 = jnp.concatenate([stack(bt), stack(kt)], axis=0)
        g = _mm(jnp.concatenate([at, rt], axis=0), r4, NT)

        def blockdiag(x, mask):
            return jnp.where(mask, jnp.concatenate([x, x], axis=0), 0.0)

        a_ab = blockdiag(g[:C, :2 * C], strict)
        a_ak = blockdiag(g[:C, 2 * C:], strict)
        a_rb = blockdiag(g[C:, :2 * C], incl)
        a_rk = blockdiag(g[C:, 2 * C:], incl)

        inv = eye + a_ab
        apow = a_ab
        for _ in range(n_dbl):
            apow = _mm(apow, apow)
            inv = inv + _mm(inv, apow)

        bd = bd_ref[p]
        bdm = bd * e_mid
        vst = stack(vm)
        w1a = _mm(stack(at), bdm, NT) + _mm(a_ak, vst)
        ust = _mm(inv, w1a)
        yst = _mm(stack(rt), bdm, NT) + _mm(a_rk, vst) + _mm(a_rb, ust)
        y = yst[:C] + yst[C:]
        upd = _mm(jnp.concatenate([ust, vst], axis=0), r4, TN)
        bd_ref[p] = bd * p_last + upd * e_last

        mean = _sel_right(y, seg01) * (1.0 / H)
        yc = y - mean
        var = _sel_right(yc * yc, seg01) * (1.0 / H)
        yn = yc * lax.rsqrt(var + LNX_EPS) * lnw + lnb
        oa_ref[p] = (yn + bonus) * _silu(za_ref[p])
        return carry

    lax.fori_loop(0, NPAIR, body, 0, unroll=unroll)

    @pl.when(c == pl.num_programs(1) - 1)
    def _fin():
        for p in range(NPAIR):
            bd = bd_ref[p]
            sout_ref[0, 2 * p] = bd[:H, :H]
            sout_ref[0, 2 * p + 1] = bd[H:, H:]


def _rwkv(z3, par, mulo, w2p, a2p, buf, state, *, nseq, nchunk, C, row0, valid_lo, valid_hi, unroll):
    m = nseq * nchunk * C
    rb = lambda n, c: row0 + n * nchunk + c
    kern = functools.partial(_rwkv_kernel, C=C, valid_lo=valid_lo, valid_hi=valid_hi, unroll=unroll)
    return pl.pallas_call(
        kern,
        grid=(nseq, nchunk),
        in_specs=[
            pl.BlockSpec((3 * NPAIR, C, LANE), lambda n, c: (0, rb(n, c), 0)),
            pl.BlockSpec((NPAIR, C, LANE), lambda n, c: (SL_ZA // NPAIR, rb(n, c), 0)),
            pl.BlockSpec((2, C, LANE), lambda n, c: (SL_WLO // 2, rb(n, c), 0)),
            pl.BlockSpec((NPAIR, N_PAR_ROWS, LANE), lambda n, c: (0, 0, 0)),
            pl.BlockSpec((8, LANE), lambda n, c: (0, 0)),
            pl.BlockSpec((NPAIR, LANE, LANE), lambda n, c: (0, 0, 0)),
            pl.BlockSpec((NPAIR, LANE, LANE), lambda n, c: (0, 0, 0)),
            pl.BlockSpec((1, N_BUF_ROWS, LANE), lambda n, c: (n, 0, 0)),
            pl.BlockSpec((1, A_HEADS, A_HEAD, A_HEAD), lambda n, c: (n, 0, 0, 0)),
        ],
        out_specs=[
            pl.BlockSpec((NPAIR, C, LANE), lambda n, c: (0, n * nchunk + c, 0)),
            pl.BlockSpec((1, A_HEADS, A_HEAD, A_HEAD), lambda n, c: (n, 0, 0, 0)),
        ],
        out_shape=[
            jax.ShapeDtypeStruct((NPAIR, m, LANE), F32),
            jax.ShapeDtypeStruct((nseq, A_HEADS, A_HEAD, A_HEAD), F32),
        ],
        scratch_shapes=[
            pltpu.VMEM((NPAIR, LANE, LANE), F32),
            pltpu.VMEM((N_BUF_ROWS, LANE), F32),
            pltpu.VMEM((C, LANE), F32),
            pltpu.VMEM((C, LANE), F32),
        ],
        compiler_params=pltpu.CompilerParams(
            dimension_semantics=("arbitrary", "arbitrary"), vmem_limit_bytes=VMEM_LIMIT),
        name="rwkv_c%d" % C,
    )(z3, z3, z3, par, mulo, w2p, a2p, buf, state)


def _gla_kernel(q_ref, k_ref, v_ref, zb_ref, gk_ref, gw_ref, gwt_ref, nw_ref, sin_ref,
                ob_ref, sout_ref, s_ref, o_ref, *, C, SB, valid_lo, valid_hi):
    c = pl.program_id(2)

    @pl.when(c == 0)
    def _init():
        s_ref[...] = sin_ref[0, 0]

    pos = c * C + _iota((C, 1), 0)
    valid = (pos >= valid_lo) & (pos < valid_hi)

    glo = gk_ref[0]
    glo1 = jnp.where(_iota((C, LANE), 1) == GATE_LORA, 1.0, glo)
    g = jnp.where(valid, _log_sigmoid(_mm_hi(glo1, gw_ref[...])) / GATE_TAU, 0.0)
    ti = _iota((C, C), 0)
    tj = _iota((C, C), 1)
    tri = jnp.where(tj <= ti, 1.0, 0.0).astype(BF16)
    b = _cumsum_rows(tri, g)

    posc = c * C + _iota((1, C), 1)
    validc = (posc >= valid_lo) & (posc < valid_hi)
    gt = jnp.where(validc, _log_sigmoid(_mm_hi(gwt_ref[...], glo1, NT)) / GATE_TAU, 0.0)
    b_last_col = jnp.sum(gt, axis=1, keepdims=True)

    q = jnp.concatenate([q_ref[0], q_ref[1]], axis=1) * (B_HK ** -0.5)
    k = jnp.where(valid, jnp.concatenate([k_ref[0], k_ref[1]], axis=1), 0.0)
    v = jnp.where(valid, jnp.concatenate([v_ref[j] for j in range(4)], axis=1), 0.0)
    s = s_ref[...]

    o_ref[...] = _mm(q * jnp.exp(b), s)

    nb = C // SB
    for i in range(1, nb):
        lo, hi = SB * i, SB * (i + 1)
        beta = b[lo - 1:lo]
        qi = q[lo:hi] * jnp.exp(b[lo:hi] - beta)
        kj = k[:lo] * jnp.exp(beta - b[:lo])
        att = _mm(qi, kj, NT)
        o_ref[lo:hi, :] += _mm(att, v[:lo])

    srow = _iota((SB, 1), 0)
    for i in range(nb):
        lo = SB * i
        bb = b[lo:lo + SB]
        kb = k[lo:lo + SB]
        vb = v[lo:lo + SB]
        for t in range(SB):
            e = jnp.exp(jnp.where(srow <= t, bb[t:t + 1] - bb, -jnp.inf))
            pcol = jnp.sum(q[lo + t:lo + t + 1] * kb * e, axis=-1, keepdims=True)
            o_ref[lo + t:lo + t + 1, :] += jnp.sum(pcol * vb, axis=0, keepdims=True)

    o = o_ref[...]
    ms = jnp.mean(o * o, axis=-1, keepdims=True)
    on = o * lax.rsqrt(ms + NORM_EPS) * nw_ref[...]
    for j in range(4):
        ob_ref[j] = on[:, j * LANE:(j + 1) * LANE] * _silu(zb_ref[j])

    kd = k * jnp.exp(b[C - 1:C] - b)
    s_ref[...] = s * jnp.exp(b_last_col) + _mm(kd, v, TN)

    @pl.when(c == pl.num_programs(2) - 1)
    def _fin():
        sout_ref[0, 0] = s_ref[...]


def _gla(z3, gw, gwt, nw, state, *, nseq, nchunk, C, SB, row0, valid_lo, valid_hi):
    m = nseq * nchunk * C
    rb = lambda n, c: row0 + n * nchunk + c
    kern = functools.partial(_gla_kernel, C=C, SB=SB, valid_lo=valid_lo, valid_hi=valid_hi)
    return pl.pallas_call(
        kern,
        grid=(nseq, B_HEADS, nchunk),
        in_specs=[
            pl.BlockSpec((2, C, LANE), lambda n, h, c: (SL_QB // 2 + h, rb(n, c), 0)),
            pl.BlockSpec((2, C, LANE), lambda n, h, c: (SL_KB // 2 + h, rb(n, c), 0)),
            pl.BlockSpec((4, C, LANE), lambda n, h, c: (SL_VB // 4 + h, rb(n, c), 0)),
            pl.BlockSpec((4, C, LANE), lambda n, h, c: (SL_ZB // 4 + h, rb(n, c), 0)),
            pl.BlockSpec((1, C, LANE), lambda n, h, c: (SL_GK, rb(n, c), 0)),
            pl.BlockSpec((LANE, B_HK), lambda n, h, c: (0, h)),
            pl.BlockSpec((B_HK, LANE), lambda n, h, c: (h, 0)),
            pl.BlockSpec((1, B_HV), lambda n, h, c: (0, 0)),
            pl.BlockSpec((1, 1, B_HK, B_HV), lambda n, h, c: (n, h, 0, 0)),
        ],
        out_specs=[
            pl.BlockSpec((4, C, LANE), lambda n, h, c: (h, n * nchunk + c, 0)),
            pl.BlockSpec((1, 1, B_HK, B_HV), lambda n, h, c: (n, h, 0, 0)),
        ],
        out_shape=[
            jax.ShapeDtypeStruct((NPAIR, m, LANE), F32),
            jax.ShapeDtypeStruct((nseq, B_HEADS, B_HK, B_HV), F32),
        ],
        scratch_shapes=[
            pltpu.VMEM((B_HK, B_HV), F32),
            pltpu.VMEM((C, B_HV), F32),
        ],
        compiler_params=pltpu.CompilerParams(
            dimension_semantics=("arbitrary", "arbitrary", "arbitrary"), vmem_limit_bytes=VMEM_LIMIT),
        name="gla_c%d" % C,
    )(z3, z3, z3, z3, z3, gw, gwt, nw, state)


def _outproj_kernel(oa_ref, ob_ref, ga_ref, gb_ref, x_ref, w_ref, g_ref, o_ref, m_ref):
    for q in range(NPAIR):
        m = _sigmoid(ga_ref[q]) * oa_ref[q] + _sigmoid(gb_ref[q]) * ob_ref[q]
        m_ref[:, q * LANE:(q + 1) * LANE] = m.astype(BF16)
    y = jnp.dot(m_ref[...], w_ref[...], preferred_element_type=F32)
    ms = jnp.mean(y * y, axis=-1, keepdims=True)
    o_ref[...] = x_ref[...] + y * lax.rsqrt(ms + NORM_EPS) * g_ref[...]


def _outproj(oa3, ob3, z3, x, w, g, tile_m):
    m = x.shape[0]
    return pl.pallas_call(
        _outproj_kernel,
        grid=(m // tile_m,),
        in_specs=[
            pl.BlockSpec((NPAIR, tile_m, LANE), lambda i: (0, i, 0)),
            pl.BlockSpec((NPAIR, tile_m, LANE), lambda i: (0, i, 0)),
            pl.BlockSpec((NPAIR, tile_m, LANE), lambda i: (SL_GA // NPAIR, i, 0)),
            pl.BlockSpec((NPAIR, tile_m, LANE), lambda i: (SL_GB // NPAIR, i, 0)),
            pl.BlockSpec((tile_m, D_MODEL), lambda i: (i, 0)),
            pl.BlockSpec((D_MODEL, D_MODEL), lambda i: (0, 0)),
            pl.BlockSpec((1, D_MODEL), lambda i: (0, 0)),
        ],
        out_specs=pl.BlockSpec((tile_m, D_MODEL), lambda i: (i, 0)),
        out_shape=jax.ShapeDtypeStruct((m, D_MODEL), F32),
        scratch_shapes=[pltpu.VMEM((tile_m, D_MODEL), BF16)],
        compiler_params=pltpu.CompilerParams(
            dimension_semantics=("arbitrary",), vmem_limit_bytes=VMEM_LIMIT),
        name="outproj",
    )(oa3, ob3, z3, z3, x, w, g)


def _pick_tile(m, cap, mult):
    best = mult
    for t in range(mult, cap + 1, mult):
        if m % t == 0:
            best = t
    assert m % best == 0
    return best


def _pad_lanes(x):
    return jnp.pad(x, [(0, 0)] * (x.ndim - 1) + [(0, LANE - x.shape[-1])])


def _pack_w_in(w):
    aw = D_MODEL
    s = 3 * aw + 2 * LORA
    za = w[:, s:s + aw]
    qb = w[:, s + aw:s + aw + 1024]
    kb = w[:, s + aw + 1024:s + aw + 2048]
    vb = w[:, s + aw + 2048:s + aw + 4096]
    gk = w[:, s + aw + 4096:s + aw + 4096 + GATE_LORA]
    o = s + aw + 4096 + GATE_LORA
    zb = w[:, o:o + 2048]
    ga = w[:, o + 2048:o + 4096]
    gb = w[:, o + 4096:o + 6144]
    wlo = w[:, 3 * aw:3 * aw + LORA]
    alo = w[:, 3 * aw + LORA:s]
    parts = [w[:, :3 * aw], za, qb, kb, vb, zb, ga, gb, _pad_lanes(wlo), _pad_lanes(alo), _pad_lanes(gk)]
    packed = jnp.concatenate(parts, axis=1)
    return jnp.pad(packed, ((0, 0), (0, NCOLP - packed.shape[1]))).astype(BF16)


def _pair_major(x):
    return x.reshape(x.shape[0], NPAIR, LANE).transpose(1, 0, 2)


def _shift_rows(zrows):
    n = zrows.shape[1]
    rkv = zrows[:3 * NPAIR].transpose(1, 0, 2).reshape(n, 3 * D_MODEL)
    return jnp.concatenate([rkv, zrows[SL_WLO, :, :LORA], zrows[SL_ALO, :, :LORA]], axis=1)


def kernel(x_prompt, x_sample, state_rwkv, state_shift, state_gla, meta_tokens, norm_pre, w_in, rwkv_mu, rwkv_w0, rwkv_w2, rwkv_a0, rwkv_a2, rwkv_k_k, rwkv_k_a, rwkv_r_k, rwkv_lnx_w, rwkv_lnx_b, gla_gk_w2, gla_gk_b, gla_norm_w, w_out, norm_post):
    depth = w_in.shape[0]
    bp, seq, _ = x_prompt.shape
    bs, dseq, _ = x_sample.shape
    front = CHUNK - N_META
    tp = front + N_META + seq
    assert tp % CHUNK == 0 and dseq <= SAMPLE_ROWS
    nchunk = tp // CHUNK
    mp = bp * tp
    ms = bs * SAMPLE_ROWS
    mtot = mp + ms
    tile_in = _pick_tile(mtot, 1184, 16)
    tile_out = _pick_tile(mtot, 296, 8)

    meta = jnp.broadcast_to(meta_tokens[None], (bp, N_META, D_MODEL))
    xp = jnp.concatenate([jnp.zeros((bp, front, D_MODEL), F32), meta, x_prompt], axis=1).reshape(mp, D_MODEL)
    xs = jnp.pad(x_sample, ((0, 0), (0, SAMPLE_ROWS - dseq), (0, 0))).reshape(ms, D_MODEL)
    x = jnp.concatenate([xp, xs], axis=0)

    zero_buf = jnp.zeros((bp, N_BUF_ROWS, LANE), F32)
    zero_rwkv = jnp.zeros((bp, A_HEADS, A_HEAD, A_HEAD), F32)
    zero_gla = jnp.zeros((bp, B_HEADS, B_HK, B_HV), F32)

    outs = {k: [] for k in ("pr", "ps", "pg", "sr", "ss", "sg")}
    for l in range(depth):
        w_packed = _pack_w_in(w_in[l])
        mu = rwkv_mu[l]
        par = jnp.stack([mu[:D_MODEL], mu[D_MODEL:2 * D_MODEL], mu[2 * D_MODEL:3 * D_MODEL],
                         rwkv_w0[l], rwkv_a0[l], rwkv_k_k[l], rwkv_k_a[l], rwkv_r_k[l].reshape(-1),
                         rwkv_lnx_w[l], rwkv_lnx_b[l]], axis=0)
        par = jnp.pad(_pair_major(par), ((0, 0), (0, N_PAR_ROWS - par.shape[0]), (0, 0)))
        mulo = jnp.pad(jnp.stack([_pad_lanes(mu[3 * D_MODEL:3 * D_MODEL + LORA]),
                                  _pad_lanes(mu[3 * D_MODEL + LORA:])], axis=0), ((0, 6), (0, 0)))
        w2p = _pair_major(jnp.pad(rwkv_w2[l], ((0, LANE - LORA), (0, 0))))
        a2p = _pair_major(jnp.pad(rwkv_a2[l], ((0, LANE - LORA), (0, 0))))
        gw = jnp.concatenate([gla_gk_w2[l], gla_gk_b[l][None]], axis=0)
        gw = jnp.pad(gw, ((0, LANE - gw.shape[0]), (0, 0)))
        gwt = gw.T
        nw = gla_norm_w[l][None]
        sh = state_shift[l]
        buf_s = jnp.concatenate([sh[:, :3 * D_MODEL].reshape(bs, 3 * NPAIR, LANE),
                                 _pad_lanes(sh[:, 3 * D_MODEL:3 * D_MODEL + LORA])[:, None],
                                 _pad_lanes(sh[:, 3 * D_MODEL + LORA:])[:, None]], axis=1)
        buf_s = jnp.pad(buf_s, ((0, 0), (0, N_BUF_ROWS - buf_s.shape[1]), (0, 0)))

        z3 = _inproj(x, norm_pre[l][None], w_packed, tile_in)

        rw = dict(par=par, mulo=mulo, w2p=w2p, a2p=a2p)
        oa_p, sr_p = _rwkv(z3, buf=zero_buf, state=zero_rwkv, nseq=bp, nchunk=nchunk, C=CHUNK, row0=0,
                           valid_lo=front, valid_hi=tp, unroll=2, **rw)
        oa_s, sr_s = _rwkv(z3, buf=buf_s, state=state_rwkv[l], nseq=bs, nchunk=1, C=SAMPLE_ROWS,
                           row0=mp // SAMPLE_ROWS, valid_lo=0, valid_hi=dseq, unroll=4, **rw)
        ob_p, sg_p = _gla(z3, gw, gwt, nw, zero_gla, nseq=bp, nchunk=nchunk, C=CHUNK, SB=GLA_SUB, row0=0,
                          valid_lo=front, valid_hi=tp)
        ob_s, sg_s = _gla(z3, gw, gwt, nw, state_gla[l], nseq=bs, nchunk=1, C=SAMPLE_ROWS, SB=SAMPLE_ROWS,
                          row0=mp // SAMPLE_ROWS, valid_lo=0, valid_hi=dseq)
        oa3 = jnp.concatenate([oa_p, oa_s], axis=1)
        ob3 = jnp.concatenate([ob_p, ob_s], axis=1)

        x = _outproj(oa3, ob3, z3, x, w_out[l].astype(BF16), norm_post[l][None], tile_out)

        last_p = jnp.arange(bp) * tp + tp - 1
        last_s = mp + jnp.arange(bs) * SAMPLE_ROWS + dseq - 1
        outs["pr"].append(sr_p)
        outs["ps"].append(_shift_rows(z3[:, last_p, :]))
        outs["pg"].append(sg_p)
        outs["sr"].append(sr_s)
        outs["ss"].append(_shift_rows(z3[:, last_s, :]))
        outs["sg"].append(sg_s)

    y_prompt = x[:mp].reshape(bp, tp, D_MODEL)[:, front + N_META:]
    y_sample = x[mp:].reshape(bs, SAMPLE_ROWS, D_MODEL)[:, :dseq]
    st = lambda k: jnp.stack(outs[k])
    return (y_prompt, y_sample, st("pr"), st("ps"), st("pg"), st("sr"), st("ss"), st("sg"))
```

```python
import functools

import jax
import jax.numpy as jnp
from jax import lax
from jax.experimental import pallas as pl
from jax.experimental.pallas import tpu as pltpu

F32 = jnp.float32
BF16 = jnp.bfloat16

D_MODEL = 2048
N_META = 16
NORM_EPS = 1e-6
LNX_EPS = 64e-5
A_HEAD = 64
A_HEADS = D_MODEL // A_HEAD
LORA = 96
B_HEADS = 4
B_HK = 256
B_HV = 512
GATE_LORA = 16
GATE_TAU = 16.0
CHUNK = 64
SAMPLE_ROWS = 8
GLA_SUB = 16
RWKV_SAMPLE_SEQS = 2

LANE = 128
NPAIR = D_MODEL // LANE
SLABS_PER_TILE = 8
TILE_N = SLABS_PER_TILE * LANE
SL_R, SL_K, SL_V, SL_ZA = 0, 16, 32, 48
SL_QB, SL_KB, SL_VB, SL_ZB = 64, 72, 80, 96
SL_GA, SL_GB = 112, 128
SL_WLO, SL_ALO, SL_GK = 144, 145, 146
NSLAB = 152
NCOLP = NSLAB * LANE
N_BUF_ROWS = 56
N_PAR_ROWS = 16
VMEM_LIMIT = 56 * 1024 * 1024

NN = ((1,), (0,))
NT = ((1,), (1,))
TN = ((0,), (0,))
HIGHEST = lax.Precision.HIGHEST


def _dot(a, b, dims=NN, precision=None):
    return lax.dot_general(a, b, (dims, ((), ())), precision=precision, preferred_element_type=F32)


def _mm(a, b, dims=NN):
    return _dot(a.astype(BF16), b.astype(BF16), dims)


def _mm_hi(a, b, dims=NN):
    return _dot(a, b, dims, precision=HIGHEST)


def _split2(x):
    hi = x.astype(BF16)
    lo = (x - hi.astype(F32)).astype(BF16)
    return hi, lo


def _split3(x):
    hi = x.astype(BF16)
    r1 = x - hi.astype(F32)
    mid = r1.astype(BF16)
    lo = (r1 - mid.astype(F32)).astype(BF16)
    return hi, mid, lo


def _mm3(a, b, dims=NN):
    ah, al = _split2(a)
    bh, bl = _split2(b)
    return _dot(ah, bh, dims) + _dot(al, bh, dims) + _dot(ah, bl, dims)


def _sel_right(x, m01):
    hi, lo = _split2(x)
    return _dot(hi, m01) + _dot(lo, m01)


def _cumsum_rows(tri, x):
    hi, mid, lo = _split3(x)
    return _dot(tri, hi) + _dot(tri, mid) + _dot(tri, lo)


def _softplus(x):
    return jnp.maximum(x, 0.0) + jnp.log(1.0 + jnp.exp(-jnp.abs(x)))


def _sigmoid(x):
    return 1.0 / (1.0 + jnp.exp(-x))


def _silu(x):
    return x * _sigmoid(x)


def _log_sigmoid(x):
    return -_softplus(-x)


def _iota(shape, dim):
    return lax.broadcasted_iota(jnp.int32, shape, dim)


def _inproj_kernel(x_ref, g_ref, w_ref, o_ref, h_ref):
    @pl.when(pl.program_id(1) == 0)
    def _():
        x = x_ref[...]
        ms = jnp.mean(x * x, axis=-1, keepdims=True)
        h_ref[...] = (x * lax.rsqrt(ms + NORM_EPS) * g_ref[...]).astype(BF16)

    acc = jnp.dot(h_ref[...], w_ref[...], preferred_element_type=F32)
    for q in range(SLABS_PER_TILE):
        o_ref[q] = acc[:, q * LANE:(q + 1) * LANE]


def _inproj(x, g, w, tile_m):
    m = x.shape[0]
    return pl.pallas_call(
        _inproj_kernel,
        grid=(m // tile_m, NCOLP // TILE_N),
        in_specs=[
            pl.BlockSpec((tile_m, D_MODEL), lambda i, j: (i, 0)),
            pl.BlockSpec((1, D_MODEL), lambda i, j: (0, 0)),
            pl.BlockSpec((D_MODEL, TILE_N), lambda i, j: (0, j)),
        ],
        out_specs=pl.BlockSpec((SLABS_PER_TILE, tile_m, LANE), lambda i, j: (j, i, 0)),
        out_shape=jax.ShapeDtypeStruct((NSLAB, m, LANE), F32),
        scratch_shapes=[pltpu.VMEM((tile_m, D_MODEL), BF16)],
        compiler_params=pltpu.CompilerParams(
            dimension_semantics=("arbitrary", "arbitrary"), vmem_limit_bytes=VMEM_LIMIT),
        name="inproj",
    )(x, g, w)


def _rwkv_kernel(zr_ref, za_ref, lo_ref, par_ref, mulo_ref, w2_ref, a2_ref, buf_ref, sin_ref,
                 oa_ref, sout_ref, bd_ref, prev_ref, at_ref, rt_ref, bt_ref, kt_ref, vm_ref, y_ref,
                 *, C, nsb, nchunk, valid_lo, valid_hi):
    c = pl.program_id(1)
    H = A_HEAD
    R = nsb * C
    W = D_MODEL
    units = [(s, p) for s in range(nsb) for p in range(NPAIR)]

    @pl.when(c == 0)
    def _init():
        prev_ref[...] = buf_ref[...]
        z = jnp.zeros((H, H), F32)
        for s, p in units:
            se = sin_ref[s, 2 * p]
            so = sin_ref[s, 2 * p + 1]
            bd_ref[s * NPAIR + p] = jnp.concatenate(
                [jnp.concatenate([se, z], axis=1), jnp.concatenate([z, so], axis=1)], axis=0)

    row = _iota((R, 1), 0)
    rin = row & (C - 1)
    pos = c * C + rin
    valid = (pos >= valid_lo) & (pos < valid_hi)
    first = pos == valid_lo

    def wide(ref, n0, n):
        return jnp.concatenate([ref[n0 + j] for j in range(n)], axis=1)

    def rowcat(ref, s, n0, n):
        return jnp.concatenate([ref[s, n0 + j:n0 + j + 1, :] for j in range(n)], axis=1)

    def prev_rows(x, n0, n):
        prv = pltpu.roll(x, 1, axis=0)
        for s in range(nsb):
            prv = jnp.where(row == s * C, rowcat(prev_ref, s, n0, n), prv)
            if valid_lo > 0:
                in_s = (row >= s * C) & (row < (s + 1) * C)
                prv = jnp.where(first & in_s, rowcat(buf_ref, s, n0, n), prv)
        return prv

    par = par_ref[...]
    mu_r, mu_k, mu_v = par[0:1], par[1:2], par[2:3]
    w0, a0, k_k, k_a, r_k = par[3:4], par[4:5], par[5:6], par[6:7], par[7:8]
    lnw, lnb = par[8:9], par[9:10]

    rz = wide(zr_ref, 0, NPAIR)
    kz = wide(zr_ref, NPAIR, NPAIR)
    vz = wide(zr_ref, 2 * NPAIR, NPAIR)
    wl_raw = lo_ref[0]
    al_raw = lo_ref[1]
    r = rz + (prev_rows(rz, 0, NPAIR) - rz) * mu_r
    k = kz + (prev_rows(kz, NPAIR, NPAIR) - kz) * mu_k
    v = vz + (prev_rows(vz, 2 * NPAIR, NPAIR) - vz) * mu_v
    tw = jnp.tanh(wl_raw + (prev_rows(wl_raw, 3 * NPAIR, 1) - wl_raw) * mulo_ref[0:1, :])
    xa = al_raw + (prev_rows(al_raw, 3 * NPAIR + 1, 1) - al_raw) * mulo_ref[1:2, :]
    if nchunk > 1:
        for s in range(nsb):
            last = s * C + C - 1
            for j in range(3 * NPAIR):
                prev_ref[s, j:j + 1, :] = zr_ref[j, last:last + 1, :]
            prev_ref[s, 3 * NPAIR:3 * NPAIR + 1, :] = wl_raw[last:last + 1]
            prev_ref[s, 3 * NPAIR + 1:3 * NPAIR + 2, :] = al_raw[last:last + 1]

    gi = _iota((LANE, LANE), 0)
    gj = _iota((LANE, LANE), 1)
    seg01 = jnp.where((gi < H) == (gj < H), 1.0, 0.0).astype(BF16)
    ti = _iota((R, R), 0)
    tj = _iota((R, R), 1)
    tri = jnp.where((tj <= ti) & ((ti & -C) == (tj & -C)), 1.0, 0.0).astype(BF16)
    ri = _iota((2 * C, 2 * C), 0)
    ci = _iota((2 * C, 2 * C), 1)
    rt_ = jnp.where(ri >= C, ri - C, ri)
    ct_ = jnp.where(ci >= C, ci - C, ci)
    same = (ri >= C) == (ci >= C)
    strict = same & (ct_ < rt_)
    incl = same & (ct_ <= rt_)
    eye = jnp.where(ri == ci, 1.0, 0.0)
    left = _iota((C, LANE), 1) < H
    n_dbl = C.bit_length() - 2

    def seg(x):
        pm = jnp.concatenate([x[:, p * LANE:(p + 1) * LANE] for p in range(NPAIR)], axis=0)
        sm = _sel_right(pm, seg01)
        return jnp.concatenate([sm[p * R:(p + 1) * R] for p in range(NPAIR)], axis=1)

    wl = w0 + _mm3(tw, w2_ref[...])
    ld = -jnp.exp(-_softplus(-wl) - 0.5)
    al = _sigmoid(a0 + _mm3(xa, a2_ref[...]))
    kkv = k * k_k
    kkn = kkv / jnp.maximum(jnp.sqrt(seg(kkv * kkv)), 1e-12)
    k2 = k * (1.0 + (al - 1.0) * k_a)
    bonus = seg(r * k2 * r_k) * v

    ld = jnp.where(valid, ld, 0.0)
    L = _cumsum_rows(tri, ld)
    center = C >= 32 and nsb == 1
    lref = L[C // 2 - 1:C // 2] if center else jnp.zeros((1, W), F32)
    e_fwd = jnp.exp(L - lref)
    e_bwd = jnp.exp(lref - L)
    at_ref[...] = -kkn * jnp.exp(L - ld - lref)
    rt_ref[...] = r * e_fwd
    bt_ref[...] = jnp.where(valid, kkn * al, 0.0) * e_bwd
    kt_ref[...] = jnp.where(valid, k2, 0.0) * e_bwd
    vm_ref[...] = jnp.where(valid, v, 0.0)
    e_mid = jnp.exp(lref)
    e_last = [e_fwd[s * C + C - 1:s * C + C] for s in range(nsb)]
    p_last = [jnp.exp(L[s * C + C - 1:s * C + C]) for s in range(nsb)]

    def sl(ref, s, p):
        return ref[s * C:(s + 1) * C, p * LANE:(p + 1) * LANE]

    def lanes(x, p):
        return x[:, p * LANE:(p + 1) * LANE]

    def stack(x):
        return jnp.concatenate([jnp.where(left, x, 0.0), jnp.where(left, 0.0, x)], axis=0)

    def blockdiag(x, mask):
        return jnp.where(mask, jnp.concatenate([x, x], axis=0), 0.0)

    r4, a_ab, a_ak, a_rb, a_rk, inv, apow = {}, {}, {}, {}, {}, {}, {}
    for u in units:
        s, p = u
        at, rt = sl(at_ref, s, p), sl(rt_ref, s, p)
        r4[u] = jnp.concatenate([stack(sl(bt_ref, s, p)), stack(sl(kt_ref, s, p))], axis=0).astype(BF16)
        g = _dot(jnp.concatenate([at, rt], axis=0).astype(BF16), r4[u], NT)
        a_ab[u] = blockdiag(g[:C, :2 * C], strict)
        a_ak[u] = blockdiag(g[:C, 2 * C:], strict)
        a_rb[u] = blockdiag(g[C:, :2 * C], incl)
        a_rk[u] = blockdiag(g[C:, 2 * C:], incl)
        inv[u] = eye + a_ab[u]
        apow[u] = a_ab[u]
    for _ in range(n_dbl):
        for u in units:
            apow[u] = _mm(apow[u], apow[u])
        for u in units:
            inv[u] = inv[u] + _mm(inv[u], apow[u])

    x2, av2, vst, ust = {}, {}, {}, {}
    for u in units:
        s, p = u
        bdm = bd_ref[s * NPAIR + p] * lanes(e_mid, p)
        lhs = jnp.concatenate([stack(sl(at_ref, s, p)), stack(sl(rt_ref, s, p))], axis=0)
        x2[u] = _mm(lhs, bdm, NT)
    for u in units:
        s, p = u
        vst[u] = stack(sl(vm_ref, s, p)).astype(BF16)
        av2[u] = _dot(jnp.concatenate([a_ak[u], a_rk[u]], axis=0).astype(BF16), vst[u])
    for u in units:
        ust[u] = _mm(inv[u], x2[u][:2 * C] + av2[u][:2 * C])
    for u in units:
        s, p = u
        yst = x2[u][2 * C:] + av2[u][2 * C:] + _mm(a_rb[u], ust[u])
        y_ref[s * C:(s + 1) * C, p * LANE:(p + 1) * LANE] = yst[:C] + yst[C:]
    for u in units:
        s, p = u
        upd = _dot(jnp.concatenate([ust[u].astype(BF16), vst[u]], axis=0), r4[u], TN)
        bd_ref[s * NPAIR + p] = bd_ref[s * NPAIR + p] * lanes(p_last[s], p) + upd * lanes(e_last[s], p)

    y = y_ref[...]
    yc = y - seg(y) * (1.0 / H)
    var = seg(yc * yc) * (1.0 / H)
    out = (yc * lax.rsqrt(var + LNX_EPS) * lnw + lnb + bonus) * _silu(wide(za_ref, 0, NPAIR))
    for p in range(NPAIR):
        oa_ref[p] = lanes(out, p)

    @pl.when(c == nchunk - 1)
    def _fin():
        for s, p in units:
            bd = bd_ref[s * NPAIR + p]
            sout_ref[s, 2 * p] = bd[:H, :H]
            sout_ref[s, 2 * p + 1] = bd[H:, H:]


def _rwkv(z3, par, mulo, w2, a2, buf, state, *, nseq, nsb, nchunk, C, row0, valid_lo, valid_hi):
    assert nsb == 1 or nchunk == 1
    R = nsb * C
    m = nseq * nchunk * C
    rb = lambda n, c: row0 + n * nchunk + c
    kern = functools.partial(_rwkv_kernel, C=C, nsb=nsb, nchunk=nchunk, valid_lo=valid_lo, valid_hi=valid_hi)
    wide_scratch = pltpu.VMEM((R, D_MODEL), F32)
    return pl.pallas_call(
        kern,
        grid=(nseq // nsb, nchunk),
        in_specs=[
            pl.BlockSpec((3 * NPAIR, R, LANE), lambda n, c: (0, rb(n, c), 0)),
            pl.BlockSpec((NPAIR, R, LANE), lambda n, c: (SL_ZA // NPAIR, rb(n, c), 0)),
            pl.BlockSpec((2, R, LANE), lambda n, c: (SL_WLO // 2, rb(n, c), 0)),
            pl.BlockSpec((N_PAR_ROWS, D_MODEL), lambda n, c: (0, 0)),
            pl.BlockSpec((8, LANE), lambda n, c: (0, 0)),
            pl.BlockSpec((LANE, D_MODEL), lambda n, c: (0, 0)),
            pl.BlockSpec((LANE, D_MODEL), lambda n, c: (0, 0)),
            pl.BlockSpec((nsb, N_BUF_ROWS, LANE), lambda n, c: (n, 0, 0)),
            pl.BlockSpec((nsb, A_HEADS, A_HEAD, A_HEAD), lambda n, c: (n, 0, 0, 0)),
        ],
        out_specs=[
            pl.BlockSpec((NPAIR, R, LANE), lambda n, c: (0, n * nchunk + c, 0)),
            pl.BlockSpec((nsb, A_HEADS, A_HEAD, A_HEAD), lambda n, c: (n, 0, 0, 0)),
        ],
        out_shape=[
            jax.ShapeDtypeStruct((NPAIR, m, LANE), F32),
            jax.ShapeDtypeStruct((nseq, A_HEADS, A_HEAD, A_HEAD), F32),
        ],
        scratch_shapes=[
            pltpu.VMEM((nsb * NPAIR, LANE, LANE), F32),
            pltpu.VMEM((nsb, N_BUF_ROWS, LANE), F32),
        ] + [wide_scratch] * 6,
        compiler_params=pltpu.CompilerParams(
            dimension_semantics=("arbitrary", "arbitrary"), vmem_limit_bytes=VMEM_LIMIT),
        name="rwkv_c%d" % C,
    )(z3, z3, z3, par, mulo, w2, a2, buf, state)


def _gla_kernel(q_ref, k_ref, v_ref, zb_ref, gk_ref, gw_ref, gwt_ref, nw_ref, sin_ref,
                ob_ref, sout_ref, s_ref, o_ref, *, C, SB, valid_lo, valid_hi):
    c = pl.program_id(2)

    @pl.when(c == 0)
    def _init():
        s_ref[...] = sin_ref[0, 0]

    pos = c * C + _iota((C, 1), 0)
    valid = (pos >= valid_lo) & (pos < valid_hi)

    glo = gk_ref[0]
    glo1 = jnp.where(_iota((C, LANE), 1) == GATE_LORA, 1.0, glo)
    g = jnp.where(valid, _log_sigmoid(_mm_hi(glo1, gw_ref[...])) / GATE_TAU, 0.0)
    ti = _iota((C, C), 0)
    tj = _iota((C, C), 1)
    tri = jnp.where(tj <= ti, 1.0, 0.0).astype(BF16)
    b = _cumsum_rows(tri, g)

    posc = c * C + _iota((1, C), 1)
    validc = (posc >= valid_lo) & (posc < valid_hi)
    gt = jnp.where(validc, _log_sigmoid(_mm_hi(gwt_ref[...], glo1, NT)) / GATE_TAU, 0.0)
    b_last_col = jnp.sum(gt, axis=1, keepdims=True)

    q = jnp.concatenate([q_ref[0], q_ref[1]], axis=1) * (B_HK ** -0.5)
    k = jnp.where(valid, jnp.concatenate([k_ref[0], k_ref[1]], axis=1), 0.0)
    v = jnp.where(valid, jnp.concatenate([v_ref[j] for j in range(4)], axis=1), 0.0)
    s = s_ref[...]

    o_ref[...] = _mm(q * jnp.exp(b), s)

    nb = C // SB
    for i in range(1, nb):
        lo, hi = SB * i, SB * (i + 1)
        beta = b[lo - 1:lo]
        qi = q[lo:hi] * jnp.exp(b[lo:hi] - beta)
        kj = k[:lo] * jnp.exp(beta - b[:lo])
        att = _mm(qi, kj, NT)
        o_ref[lo:hi, :] += _mm(att, v[:lo])

    srow = _iota((SB, 1), 0)
    for i in range(nb):
        lo = SB * i
        bb = b[lo:lo + SB]
        kb = k[lo:lo + SB]
        vb = v[lo:lo + SB]
        for t in range(SB):
            e = jnp.exp(jnp.where(srow <= t, bb[t:t + 1] - bb, -jnp.inf))
            pcol = jnp.sum(q[lo + t:lo + t + 1] * kb * e, axis=-1, keepdims=True)
            o_ref[lo + t:lo + t + 1, :] += jnp.sum(pcol * vb, axis=0, keepdims=True)

    o = o_ref[...]
    ms = jnp.mean(o * o, axis=-1, keepdims=True)
    on = o * lax.rsqrt(ms + NORM_EPS) * nw_ref[...]
    for j in range(4):
        ob_ref[j] = on[:, j * LANE:(j + 1) * LANE] * _silu(zb_ref[j])

    kd = k * jnp.exp(b[C - 1:C] - b)
    s_ref[...] = s * jnp.exp(b_last_col) + _mm(kd, v, TN)

    @pl.when(c == pl.num_programs(2) - 1)
    def _fin():
        sout_ref[0, 0] = s_ref[...]


def _gla(z3, gw, gwt, nw, state, *, nseq, nchunk, C, SB, row0, valid_lo, valid_hi):
    m = nseq * nchunk * C
    rb = lambda n, c: row0 + n * nchunk + c
    kern = functools.partial(_gla_kernel, C=C, SB=SB, valid_lo=valid_lo, valid_hi=valid_hi)
    return pl.pallas_call(
        kern,
        grid=(nseq, B_HEADS, nchunk),
        in_specs=[
            pl.BlockSpec((2, C, LANE), lambda n, h, c: (SL_QB // 2 + h, rb(n, c), 0)),
            pl.BlockSpec((2, C, LANE), lambda n, h, c: (SL_KB // 2 + h, rb(n, c), 0)),
            pl.BlockSpec((4, C, LANE), lambda n, h, c: (SL_VB // 4 + h, rb(n, c), 0)),
            pl.BlockSpec((4, C, LANE), lambda n, h, c: (SL_ZB // 4 + h, rb(n, c), 0)),
            pl.BlockSpec((1, C, LANE), lambda n, h, c: (SL_GK, rb(n, c), 0)),
            pl.BlockSpec((LANE, B_HK), lambda n, h, c: (0, h)),
            pl.BlockSpec((B_HK, LANE), lambda n, h, c: (h, 0)),
            pl.BlockSpec((1, B_HV), lambda n, h, c: (0, 0)),
            pl.BlockSpec((1, 1, B_HK, B_HV), lambda n, h, c: (n, h, 0, 0)),
        ],
        out_specs=[
            pl.BlockSpec((4, C, LANE), lambda n, h, c: (h, n * nchunk + c, 0)),
            pl.BlockSpec((1, 1, B_HK, B_HV), lambda n, h, c: (n, h, 0, 0)),
        ],
        out_shape=[
            jax.ShapeDtypeStruct((NPAIR, m, LANE), F32),
            jax.ShapeDtypeStruct((nseq, B_HEADS, B_HK, B_HV), F32),
        ],
        scratch_shapes=[
            pltpu.VMEM((B_HK, B_HV), F32),
            pltpu.VMEM((C, B_HV), F32),
        ],
        compiler_params=pltpu.CompilerParams(
            dimension_semantics=("arbitrary", "arbitrary", "arbitrary"), vmem_limit_bytes=VMEM_LIMIT),
        name="gla_c%d" % C,
    )(z3, z3, z3, z3, z3, gw, gwt, nw, state)


def _outproj_kernel(oa_ref, ob_ref, ga_ref, gb_ref, x_ref, w_ref, g_ref, o_ref, m_ref):
    for q in range(NPAIR):
        m = _sigmoid(ga_ref[q]) * oa_ref[q] + _sigmoid(gb_ref[q]) * ob_ref[q]
        m_ref[:, q * LANE:(q + 1) * LANE] = m.astype(BF16)
    y = jnp.dot(m_ref[...], w_ref[...], preferred_element_type=F32)
    ms = jnp.mean(y * y, axis=-1, keepdims=True)
    o_ref[...] = x_ref[...] + y * lax.rsqrt(ms + NORM_EPS) * g_ref[...]


def _outproj(oa3, ob3, z3, x, w, g, tile_m):
    m = x.shape[0]
    return pl.pallas_call(
        _outproj_kernel,
        grid=(m // tile_m,),
        in_specs=[
            pl.BlockSpec((NPAIR, tile_m, LANE), lambda i: (0, i, 0)),
            pl.BlockSpec((NPAIR, tile_m, LANE), lambda i: (0, i, 0)),
            pl.BlockSpec((NPAIR, tile_m, LANE), lambda i: (SL_GA // NPAIR, i, 0)),
            pl.BlockSpec((NPAIR, tile_m, LANE), lambda i: (SL_GB // NPAIR, i, 0)),
            pl.BlockSpec((tile_m, D_MODEL), lambda i: (i, 0)),
            pl.BlockSpec((D_MODEL, D_MODEL), lambda i: (0, 0)),
            pl.BlockSpec((1, D_MODEL), lambda i: (0, 0)),
        ],
        out_specs=pl.BlockSpec((tile_m, D_MODEL), lambda i: (i, 0)),
        out_shape=jax.ShapeDtypeStruct((m, D_MODEL), F32),
        scratch_shapes=[pltpu.VMEM((tile_m, D_MODEL), BF16)],
        compiler_params=pltpu.CompilerParams(
            dimension_semantics=("arbitrary",), vmem_limit_bytes=VMEM_LIMIT),
        name="outproj",
    )(oa3, ob3, z3, z3, x, w, g)


def _pick_tile(m, cap, mult):
    best = mult
    for t in range(mult, cap + 1, mult):
        if m % t == 0:
            best = t
    assert m % best == 0
    return best


def _pad_lanes(x):
    return jnp.pad(x, [(0, 0)] * (x.ndim - 1) + [(0, LANE - x.shape[-1])])


def _pack_w_in(w):
    aw = D_MODEL
    s = 3 * aw + 2 * LORA
    za = w[:, s:s + aw]
    qb = w[:, s + aw:s + aw + 1024]
    kb = w[:, s + aw + 1024:s + aw + 2048]
    vb = w[:, s + aw + 2048:s + aw + 4096]
    gk = w[:, s + aw + 4096:s + aw + 4096 + GATE_LORA]
    o = s + aw + 4096 + GATE_LORA
    zb = w[:, o:o + 2048]
    ga = w[:, o + 2048:o + 4096]
    gb = w[:, o + 4096:o + 6144]
    wlo = w[:, 3 * aw:3 * aw + LORA]
    alo = w[:, 3 * aw + LORA:s]
    parts = [w[:, :3 * aw], za, qb, kb, vb, zb, ga, gb, _pad_lanes(wlo), _pad_lanes(alo), _pad_lanes(gk)]
    packed = jnp.concatenate(parts, axis=1)
    return jnp.pad(packed, ((0, 0), (0, NCOLP - packed.shape[1]))).astype(BF16)


def _shift_rows(zrows):
    n = zrows.shape[1]
    rkv = zrows[:3 * NPAIR].transpose(1, 0, 2).reshape(n, 3 * D_MODEL)
    return jnp.concatenate([rkv, zrows[SL_WLO, :, :LORA], zrows[SL_ALO, :, :LORA]], axis=1)


def kernel(x_prompt, x_sample, state_rwkv, state_shift, state_gla, meta_tokens, norm_pre, w_in, rwkv_mu, rwkv_w0, rwkv_w2, rwkv_a0, rwkv_a2, rwkv_k_k, rwkv_k_a, rwkv_r_k, rwkv_lnx_w, rwkv_lnx_b, gla_gk_w2, gla_gk_b, gla_norm_w, w_out, norm_post):
    depth = w_in.shape[0]
    bp, seq, _ = x_prompt.shape
    bs, dseq, _ = x_sample.shape
    front = CHUNK - N_META
    tp = front + N_META + seq
    assert tp % CHUNK == 0 and dseq <= SAMPLE_ROWS and bs % RWKV_SAMPLE_SEQS == 0
    nchunk = tp // CHUNK
    mp = bp * tp
    ms = bs * SAMPLE_ROWS
    mtot = mp + ms
    tile_in = _pick_tile(mtot, 1184, 16)
    tile_out = _pick_tile(mtot, 296, 8)

    meta = jnp.broadcast_to(meta_tokens[None], (bp, N_META, D_MODEL))
    xp = jnp.concatenate([jnp.zeros((bp, front, D_MODEL), F32), meta, x_prompt], axis=1).reshape(mp, D_MODEL)
    xs = jnp.pad(x_sample, ((0, 0), (0, SAMPLE_ROWS - dseq), (0, 0))).reshape(ms, D_MODEL)
    x = jnp.concatenate([xp, xs], axis=0)

    zero_buf = jnp.zeros((bp, N_BUF_ROWS, LANE), F32)
    zero_rwkv = jnp.zeros((bp, A_HEADS, A_HEAD, A_HEAD), F32)
    zero_gla = jnp.zeros((bp, B_HEADS, B_HK, B_HV), F32)

    outs = {k: [] for k in ("pr", "ps", "pg", "sr", "ss", "sg")}
    for l in range(depth):
        w_packed = _pack_w_in(w_in[l])
        mu = rwkv_mu[l]
        par = jnp.stack([mu[:D_MODEL], mu[D_MODEL:2 * D_MODEL], mu[2 * D_MODEL:3 * D_MODEL],
                         rwkv_w0[l], rwkv_a0[l], rwkv_k_k[l], rwkv_k_a[l], rwkv_r_k[l].reshape(-1),
                         rwkv_lnx_w[l], rwkv_lnx_b[l]], axis=0)
        par = jnp.pad(par, ((0, N_PAR_ROWS - par.shape[0]), (0, 0)))
        mulo = jnp.pad(jnp.stack([_pad_lanes(mu[3 * D_MODEL:3 * D_MODEL + LORA]),
                                  _pad_lanes(mu[3 * D_MODEL + LORA:])], axis=0), ((0, 6), (0, 0)))
        w2 = jnp.pad(rwkv_w2[l], ((0, LANE - LORA), (0, 0)))
        a2 = jnp.pad(rwkv_a2[l], ((0, LANE - LORA), (0, 0)))
        gw = jnp.concatenate([gla_gk_w2[l], gla_gk_b[l][None]], axis=0)
        gw = jnp.pad(gw, ((0, LANE - gw.shape[0]), (0, 0)))
        gwt = gw.T
        nw = gla_norm_w[l][None]
        sh = state_shift[l]
        buf_s = jnp.concatenate([sh[:, :3 * D_MODEL].reshape(bs, 3 * NPAIR, LANE),
                                 _pad_lanes(sh[:, 3 * D_MODEL:3 * D_MODEL + LORA])[:, None],
                                 _pad_lanes(sh[:, 3 * D_MODEL + LORA:])[:, None]], axis=1)
        buf_s = jnp.pad(buf_s, ((0, 0), (0, N_BUF_ROWS - buf_s.shape[1]), (0, 0)))

        z3 = _inproj(x, norm_pre[l][None], w_packed, tile_in)

        rw = dict(par=par, mulo=mulo, w2=w2, a2=a2)
        oa_p, sr_p = _rwkv(z3, buf=zero_buf, state=zero_rwkv, nseq=bp, nsb=1, nchunk=nchunk, C=CHUNK, row0=0,
                           valid_lo=front, valid_hi=tp, **rw)
        oa_s, sr_s = _rwkv(z3, buf=buf_s, state=state_rwkv[l], nseq=bs, nsb=RWKV_SAMPLE_SEQS, nchunk=1,
                           C=SAMPLE_ROWS, row0=mp // (SAMPLE_ROWS * RWKV_SAMPLE_SEQS), valid_lo=0, valid_hi=dseq,
                           **rw)
        ob_p, sg_p = _gla(z3, gw, gwt, nw, zero_gla, nseq=bp, nchunk=nchunk, C=CHUNK, SB=GLA_SUB, row0=0,
                          valid_lo=front, valid_hi=tp)
        ob_s, sg_s = _gla(z3, gw, gwt, nw, state_gla[l], nseq=bs, nchunk=1, C=SAMPLE_ROWS, SB=SAMPLE_ROWS,
                          row0=mp // SAMPLE_ROWS, valid_lo=0, valid_hi=dseq)
        oa3 = jnp.concatenate([oa_p, oa_s], axis=1)
        ob3 = jnp.concatenate([ob_p, ob_s], axis=1)

        x = _outproj(oa3, ob3, z3, x, w_out[l].astype(BF16), norm_post[l][None], tile_out)

        last_p = jnp.arange(bp) * tp + tp - 1
        last_s = mp + jnp.arange(bs) * SAMPLE_ROWS + dseq - 1
        outs["pr"].append(sr_p)
        outs["ps"].append(_shift_rows(z3[:, last_p, :]))
        outs["pg"].append(sg_p)
        outs["sr"].append(sr_s)
        outs["ss"].append(_shift_rows(z3[:, last_s, :]))
        outs["sg"].append(sg_s)

    y_prompt = x[:mp].reshape(bp, tp, D_MODEL)[:, front + N_META:]
    y_sample = x[mp:].reshape(bs, SAMPLE_ROWS, D_MODEL)[:, :dseq]
    st = lambda k: jnp.stack(outs[k])
    return (y_prompt, y_sample, st("pr"), st("ps"), st("pg"), st("sr"), st("ss"), st("sg"))
```

```python
import functools

import jax
import jax.numpy as jnp
from jax import lax
from jax.experimental import pallas as pl
from jax.experimental.pallas import tpu as pltpu

F32 = jnp.float32
BF16 = jnp.bfloat16

D_MODEL = 2048
N_META = 16
NORM_EPS = 1e-6
LNX_EPS = 64e-5
A_HEAD = 64
A_HEADS = D_MODEL // A_HEAD
LORA = 96
B_HEADS = 4
B_HK = 256
B_HV = 512
GATE_LORA = 16
GATE_TAU = 16.0
CHUNK = 64
SAMPLE_ROWS = 8
GLA_SUB = 16
RWKV_SAMPLE_SEQS = 2

LANE = 128
NPAIR = D_MODEL // LANE
SLABS_PER_TILE = 8
TILE_N = SLABS_PER_TILE * LANE
SL_R, SL_K, SL_V, SL_ZA = 0, 16, 32, 48
SL_QB, SL_KB, SL_VB, SL_ZB = 64, 72, 80, 96
SL_GA, SL_GB = 112, 128
SL_WLO, SL_ALO, SL_GK = 144, 145, 146
NSLAB = 152
NCOLP = NSLAB * LANE
N_BUF_ROWS = 56
N_PAR_ROWS = 16
VMEM_LIMIT = 56 * 1024 * 1024

NN = ((1,), (0,))
NT = ((1,), (1,))
TN = ((0,), (0,))


def _dot(a, b, dims=NN):
    return lax.dot_general(a, b, (dims, ((), ())), preferred_element_type=F32)


def _mm(a, b, dims=NN):
    return _dot(a.astype(BF16), b.astype(BF16), dims)


def _split2(x):
    hi = x.astype(BF16)
    lo = (x - hi.astype(F32)).astype(BF16)
    return hi, lo


def _split3(x):
    hi = x.astype(BF16)
    r1 = x - hi.astype(F32)
    mid = r1.astype(BF16)
    lo = (r1 - mid.astype(F32)).astype(BF16)
    return hi, mid, lo


def _mm3(a, b, dims=NN):
    ah, al = _split2(a)
    bh, bl = _split2(b)
    return _dot(ah, bh, dims) + _dot(al, bh, dims) + _dot(ah, bl, dims)


def _sel_right(x, m01):
    hi, lo = _split2(x)
    return _dot(hi, m01) + _dot(lo, m01)


def _cumsum_rows(tri, x):
    hi, mid, lo = _split3(x)
    return _dot(tri, hi) + _dot(tri, mid) + _dot(tri, lo)


def _softplus(x):
    return jnp.maximum(x, 0.0) + jnp.log(1.0 + jnp.exp(-jnp.abs(x)))


def _sigmoid(x):
    return 1.0 / (1.0 + jnp.exp(-x))


def _silu(x):
    return x * _sigmoid(x)


def _log_sigmoid(x):
    return -_softplus(-x)


def _iota(shape, dim):
    return lax.broadcasted_iota(jnp.int32, shape, dim)


W_RKV = 3 * D_MODEL
W_SHIFT = W_RKV + 2 * LORA
W_MID = 3 * D_MODEL
W_GK = W_SHIFT + W_MID
W_TAIL = 3 * D_MODEL
PACK_ROWS = 128
PACK_PIECE = 512


def _pack_kernel(w_ref, o_ref):
    def copy(dst, src, n):
        for off in range(0, n, PACK_PIECE):
            w = min(PACK_PIECE, n - off)
            o_ref[0, :, dst + off:dst + off + w] = w_ref[0, :, src + off:src + off + w].astype(BF16)

    copy(0, 0, W_RKV)
    copy(SL_ZA * LANE, W_SHIFT, W_MID)
    copy(SL_ZB * LANE, W_GK + GATE_LORA, W_TAIL)
    o_ref[0, :, SL_WLO * LANE:] = jnp.zeros((PACK_ROWS, NCOLP - SL_WLO * LANE), BF16)
    copy(SL_WLO * LANE, W_RKV, LORA)
    copy(SL_ALO * LANE, W_RKV + LORA, LORA)
    copy(SL_GK * LANE, W_GK, GATE_LORA)


def _pack_w_in(w_in):
    depth, d, ncol = w_in.shape
    assert ncol == W_GK + GATE_LORA + W_TAIL and d % PACK_ROWS == 0
    return pl.pallas_call(
        _pack_kernel,
        grid=(depth, d // PACK_ROWS),
        in_specs=[pl.BlockSpec((1, PACK_ROWS, ncol), lambda l, i: (l, i, 0))],
        out_specs=pl.BlockSpec((1, PACK_ROWS, NCOLP), lambda l, i: (l, i, 0)),
        out_shape=jax.ShapeDtypeStruct((depth, d, NCOLP), BF16),
        compiler_params=pltpu.CompilerParams(
            dimension_semantics=("arbitrary", "arbitrary"), vmem_limit_bytes=VMEM_LIMIT),
        name="pack_w_in",
    )(w_in)


def _inproj_kernel(x_ref, g_ref, w_ref, o_ref, h_ref):
    @pl.when(pl.program_id(1) == 0)
    def _():
        x = x_ref[...]
        ms = jnp.mean(x * x, axis=-1, keepdims=True)
        h_ref[...] = (x * lax.rsqrt(ms + NORM_EPS) * g_ref[...]).astype(BF16)

    acc = jnp.dot(h_ref[...], w_ref[0], preferred_element_type=F32)
    for q in range(SLABS_PER_TILE):
        o_ref[q] = acc[:, q * LANE:(q + 1) * LANE]


def _inproj(x, g, w, layer, tile_m):
    m = x.shape[0]
    return pl.pallas_call(
        _inproj_kernel,
        grid=(m // tile_m, NCOLP // TILE_N),
        in_specs=[
            pl.BlockSpec((tile_m, D_MODEL), lambda i, j: (i, 0)),
            pl.BlockSpec((1, D_MODEL), lambda i, j: (0, 0)),
            pl.BlockSpec((1, D_MODEL, TILE_N), lambda i, j: (layer, 0, j)),
        ],
        out_specs=pl.BlockSpec((SLABS_PER_TILE, tile_m, LANE), lambda i, j: (j, i, 0)),
        out_shape=jax.ShapeDtypeStruct((NSLAB, m, LANE), F32),
        scratch_shapes=[pltpu.VMEM((tile_m, D_MODEL), BF16)],
        compiler_params=pltpu.CompilerParams(
            dimension_semantics=("arbitrary", "arbitrary"), vmem_limit_bytes=VMEM_LIMIT),
        name="inproj",
    )(x, g, w)


def _rwkv_kernel(*refs, C, nsb, nchunk, valid_lo, valid_hi, zero_init, n_alias):
    it = iter(refs)
    zr_ref, za_ref, lo_ref, par_ref, mulo_ref, w2_ref, a2_ref, buf_ref = (next(it) for _ in range(8))
    sin_ref = None if zero_init else next(it)
    for _ in range(n_alias):
        next(it)
    oa_ref, sout_ref, bd_ref, prev_ref, at_ref, rt_ref, bt_ref, kt_ref, vm_ref, y_ref = it
    c = pl.program_id(1)
    H = A_HEAD
    R = nsb * C
    W = D_MODEL
    units = [(s, p) for s in range(nsb) for p in range(NPAIR)]

    @pl.when(c == 0)
    def _init():
        prev_ref[...] = buf_ref[...]
        z = jnp.zeros((H, H), F32)
        for s, p in units:
            if zero_init:
                bd_ref[s * NPAIR + p] = jnp.zeros((LANE, LANE), F32)
            else:
                se = sin_ref[0, s, 2 * p]
                so = sin_ref[0, s, 2 * p + 1]
                bd_ref[s * NPAIR + p] = jnp.concatenate(
                    [jnp.concatenate([se, z], axis=1), jnp.concatenate([z, so], axis=1)], axis=0)

    row = _iota((R, 1), 0)
    rin = row & (C - 1)
    pos = c * C + rin
    valid = (pos >= valid_lo) & (pos < valid_hi)
    first = pos == valid_lo

    def wide(ref, n0, n):
        return jnp.concatenate([ref[n0 + j] for j in range(n)], axis=1)

    def rowcat(ref, s, n0, n):
        return jnp.concatenate([ref[s, n0 + j:n0 + j + 1, :] for j in range(n)], axis=1)

    def prev_rows(x, n0, n):
        prv = pltpu.roll(x, 1, axis=0)
        for s in range(nsb):
            prv = jnp.where(row == s * C, rowcat(prev_ref, s, n0, n), prv)
            if valid_lo > 0:
                in_s = (row >= s * C) & (row < (s + 1) * C)
                prv = jnp.where(first & in_s, rowcat(buf_ref, s, n0, n), prv)
        return prv

    par = par_ref[...]
    mu_r, mu_k, mu_v = par[0:1], par[1:2], par[2:3]
    w0, a0, k_k, k_a, r_k = par[3:4], par[4:5], par[5:6], par[6:7], par[7:8]
    lnw, lnb = par[8:9], par[9:10]

    rz = wide(zr_ref, 0, NPAIR)
    kz = wide(zr_ref, NPAIR, NPAIR)
    vz = wide(zr_ref, 2 * NPAIR, NPAIR)
    wl_raw = lo_ref[0]
    al_raw = lo_ref[1]
    r = rz + (prev_rows(rz, 0, NPAIR) - rz) * mu_r
    k = kz + (prev_rows(kz, NPAIR, NPAIR) - kz) * mu_k
    v = vz + (prev_rows(vz, 2 * NPAIR, NPAIR) - vz) * mu_v
    tw = jnp.tanh(wl_raw + (prev_rows(wl_raw, 3 * NPAIR, 1) - wl_raw) * mulo_ref[0:1, :])
    xa = al_raw + (prev_rows(al_raw, 3 * NPAIR + 1, 1) - al_raw) * mulo_ref[1:2, :]
    if nchunk > 1:
        for s in range(nsb):
            last = s * C + C - 1
            for j in range(3 * NPAIR):
                prev_ref[s, j:j + 1, :] = zr_ref[j, last:last + 1, :]
            prev_ref[s, 3 * NPAIR:3 * NPAIR + 1, :] = wl_raw[last:last + 1]
            prev_ref[s, 3 * NPAIR + 1:3 * NPAIR + 2, :] = al_raw[last:last + 1]

    gi = _iota((LANE, LANE), 0)
    gj = _iota((LANE, LANE), 1)
    seg01 = jnp.where((gi < H) == (gj < H), 1.0, 0.0).astype(BF16)
    ti = _iota((R, R), 0)
    tj = _iota((R, R), 1)
    tri = jnp.where((tj <= ti) & ((ti & -C) == (tj & -C)), 1.0, 0.0).astype(BF16)
    ri = _iota((2 * C, 2 * C), 0)
    ci = _iota((2 * C, 2 * C), 1)
    rt_ = jnp.where(ri >= C, ri - C, ri)
    ct_ = jnp.where(ci >= C, ci - C, ci)
    same = (ri >= C) == (ci >= C)
    strict = same & (ct_ < rt_)
    incl = same & (ct_ <= rt_)
    eye = jnp.where(ri == ci, 1.0, 0.0)
    left = _iota((C, LANE), 1) < H
    n_dbl = C.bit_length() - 2

    def seg(x):
        pm = jnp.concatenate([x[:, p * LANE:(p + 1) * LANE] for p in range(NPAIR)], axis=0)
        sm = _sel_right(pm, seg01)
        return jnp.concatenate([sm[p * R:(p + 1) * R] for p in range(NPAIR)], axis=1)

    wl = w0 + _mm3(tw, w2_ref[...])
    ld = -jnp.exp(-_softplus(-wl) - 0.5)
    al = _sigmoid(a0 + _mm3(xa, a2_ref[...]))
    kkv = k * k_k
    kkn = kkv / jnp.maximum(jnp.sqrt(seg(kkv * kkv)), 1e-12)
    k2 = k * (1.0 + (al - 1.0) * k_a)
    bonus = seg(r * k2 * r_k) * v

    ld = jnp.where(valid, ld, 0.0)
    L = _cumsum_rows(tri, ld)
    center = C >= 32 and nsb == 1
    lref = L[C // 2 - 1:C // 2] if center else jnp.zeros((1, W), F32)
    e_fwd = jnp.exp(L - lref)
    e_bwd = jnp.exp(lref - L)
    at_ref[...] = -kkn * jnp.exp(L - ld - lref)
    rt_ref[...] = r * e_fwd
    bt_ref[...] = jnp.where(valid, kkn * al, 0.0) * e_bwd
    kt_ref[...] = jnp.where(valid, k2, 0.0) * e_bwd
    vm_ref[...] = jnp.where(valid, v, 0.0)
    e_mid = jnp.exp(lref)
    e_last = [e_fwd[s * C + C - 1:s * C + C] for s in range(nsb)]
    p_last = [jnp.exp(L[s * C + C - 1:s * C + C]) for s in range(nsb)]

    def sl(ref, s, p):
        return ref[s * C:(s + 1) * C, p * LANE:(p + 1) * LANE]

    def lanes(x, p):
        return x[:, p * LANE:(p + 1) * LANE]

    def stack(x):
        return jnp.concatenate([jnp.where(left, x, 0.0), jnp.where(left, 0.0, x)], axis=0)

    def blockdiag(x, mask):
        return jnp.where(mask, jnp.concatenate([x, x], axis=0), 0.0)

    r4, a_ab, a_ak, a_rb, a_rk, inv, apow = {}, {}, {}, {}, {}, {}, {}
    for u in units:
        s, p = u
        at, rt = sl(at_ref, s, p), sl(rt_ref, s, p)
        r4[u] = jnp.concatenate([stack(sl(bt_ref, s, p)), stack(sl(kt_ref, s, p))], axis=0).astype(BF16)
        g = _dot(jnp.concatenate([at, rt], axis=0).astype(BF16), r4[u], NT)
        a_ab[u] = blockdiag(g[:C, :2 * C], strict)
        a_ak[u] = blockdiag(g[:C, 2 * C:], strict)
        a_rb[u] = blockdiag(g[C:, :2 * C], incl)
        a_rk[u] = blockdiag(g[C:, 2 * C:], incl)
        inv[u] = eye + a_ab[u]
        apow[u] = a_ab[u]
    for _ in range(n_dbl):
        for u in units:
            apow[u] = _mm(apow[u], apow[u])
        for u in units:
            inv[u] = inv[u] + _mm(inv[u], apow[u])

    x2, av2, vst, ust = {}, {}, {}, {}
    for u in units:
        s, p = u
        bdm = bd_ref[s * NPAIR + p] * lanes(e_mid, p)
        lhs = jnp.concatenate([stack(sl(at_ref, s, p)), stack(sl(rt_ref, s, p))], axis=0)
        x2[u] = _mm(lhs, bdm, NT)
    for u in units:
        s, p = u
        vst[u] = stack(sl(vm_ref, s, p)).astype(BF16)
        av2[u] = _dot(jnp.concatenate([a_ak[u], a_rk[u]], axis=0).astype(BF16), vst[u])
    for u in units:
        ust[u] = _mm(inv[u], x2[u][:2 * C] + av2[u][:2 * C])
    for u in units:
        s, p = u
        yst = x2[u][2 * C:] + av2[u][2 * C:] + _mm(a_rb[u], ust[u])
        y_ref[s * C:(s + 1) * C, p * LANE:(p + 1) * LANE] = yst[:C] + yst[C:]
    for u in units:
        s, p = u
        upd = _dot(jnp.concatenate([ust[u].astype(BF16), vst[u]], axis=0), r4[u], TN)
        bd_ref[s * NPAIR + p] = bd_ref[s * NPAIR + p] * lanes(p_last[s], p) + upd * lanes(e_last[s], p)

    y = y_ref[...]
    yc = y - seg(y) * (1.0 / H)
    var = seg(yc * yc) * (1.0 / H)
    out = (yc * lax.rsqrt(var + LNX_EPS) * lnw + lnb + bonus) * _silu(wide(za_ref, 0, NPAIR))
    for p in range(NPAIR):
        oa_ref[p] = lanes(out, p)

    @pl.when(c == nchunk - 1)
    def _fin():
        for s, p in units:
            bd = bd_ref[s * NPAIR + p]
            sout_ref[0, s, 2 * p] = bd[:H, :H]
            sout_ref[0, s, 2 * p + 1] = bd[H:, H:]


def _rwkv(z3, par, mulo, w2, a2, buf, *, state, prev_oa, prev_state, layer, depth, nseq, nsb, nchunk, C, row0,
          valid_lo, valid_hi):
    assert nsb == 1 or nchunk == 1
    R = nsb * C
    rb = lambda n, c: row0 + n * nchunk + c
    state_block = (1, nsb, A_HEADS, A_HEAD, A_HEAD)
    state_map = lambda n, c: (layer, n, 0, 0, 0)
    in_specs = [
        pl.BlockSpec((3 * NPAIR, R, LANE), lambda n, c: (0, rb(n, c), 0)),
        pl.BlockSpec((NPAIR, R, LANE), lambda n, c: (SL_ZA // NPAIR, rb(n, c), 0)),
        pl.BlockSpec((2, R, LANE), lambda n, c: (SL_WLO // 2, rb(n, c), 0)),
        pl.BlockSpec((N_PAR_ROWS, D_MODEL), lambda n, c: (0, 0)),
        pl.BlockSpec((8, LANE), lambda n, c: (0, 0)),
        pl.BlockSpec((LANE, D_MODEL), lambda n, c: (0, 0)),
        pl.BlockSpec((LANE, D_MODEL), lambda n, c: (0, 0)),
        pl.BlockSpec((nsb, N_BUF_ROWS, LANE), lambda n, c: (n, 0, 0)),
    ]
    args = [z3, z3, z3, par, mulo, w2, a2, buf]
    if state is not None:
        in_specs.append(pl.BlockSpec(state_block, state_map))
        args.append(state)
    aliases = {}
    for out_idx, prev in enumerate((prev_oa, prev_state)):
        if prev is not None:
            aliases[len(args)] = out_idx
            in_specs.append(pl.BlockSpec(memory_space=pl.ANY))
            args.append(prev)
    kern = functools.partial(_rwkv_kernel, C=C, nsb=nsb, nchunk=nchunk, valid_lo=valid_lo, valid_hi=valid_hi,
                             zero_init=state is None, n_alias=len(aliases))
    wide_scratch = pltpu.VMEM((R, D_MODEL), F32)
    return pl.pallas_call(
        kern,
        grid=(nseq // nsb, nchunk),
        in_specs=in_specs,
        out_specs=[
            pl.BlockSpec((NPAIR, R, LANE), lambda n, c: (0, rb(n, c), 0)),
            pl.BlockSpec(state_block, state_map),
        ],
        out_shape=[
            jax.ShapeDtypeStruct((NPAIR, z3.shape[1], LANE), F32),
            jax.ShapeDtypeStruct((depth, nseq, A_HEADS, A_HEAD, A_HEAD), F32),
        ],
        input_output_aliases=aliases,
        scratch_shapes=[
            pltpu.VMEM((nsb * NPAIR, LANE, LANE), F32),
            pltpu.VMEM((nsb, N_BUF_ROWS, LANE), F32),
        ] + [wide_scratch] * 6,
        compiler_params=pltpu.CompilerParams(
            dimension_semantics=("arbitrary", "arbitrary"), vmem_limit_bytes=VMEM_LIMIT),
        name="rwkv_c%d" % C,
    )(*args)


def _gla_kernel(*refs, C, SB, nchunk, valid_lo, valid_hi, zero_init, n_alias):
    it = iter(refs)
    q_ref, k_ref, v_ref, zb_ref, gk_ref, gw_ref, nw_ref = (next(it) for _ in range(7))
    sin_ref = None if zero_init else next(it)
    for _ in range(n_alias):
        next(it)
    ob_ref, sout_ref, s_ref, att_ref = it
    c = pl.program_id(1)
    heads = range(B_HEADS)
    QS = B_HK // LANE
    VS = B_HV // LANE

    @pl.when(c == 0)
    def _init():
        for h in heads:
            s_ref[h] = jnp.zeros((B_HK, B_HV), F32) if zero_init else sin_ref[0, 0, h]

    pos = c * C + _iota((C, 1), 0)
    valid = (pos >= valid_lo) & (pos < valid_hi)

    glo1 = jnp.where(_iota((C, LANE), 1) == GATE_LORA, 1.0, gk_ref[0])
    g = jnp.where(valid, _log_sigmoid(_mm3(glo1, gw_ref[...])) / GATE_TAU, 0.0)
    ti = _iota((C, C), 0)
    tj = _iota((C, C), 1)
    tri = jnp.where(tj <= ti, 1.0, 0.0).astype(BF16)
    ones = jnp.ones((C, LANE), BF16)
    parts = _split3(g)
    b = _dot(tri, parts[0]) + _dot(tri, parts[1]) + _dot(tri, parts[2])
    b_col = _dot(parts[0], ones, TN) + _dot(parts[1], ones, TN) + _dot(parts[2], ones, TN)
    eb = jnp.exp(b)
    ed = jnp.exp(b[C - 1:C] - b)

    def hk(x, h):
        return x[:, h * B_HK:(h + 1) * B_HK]

    q = [jnp.concatenate([q_ref[QS * h + j] for j in range(QS)], axis=1) * (B_HK ** -0.5) for h in heads]
    k = [jnp.where(valid, jnp.concatenate([k_ref[QS * h + j] for j in range(QS)], axis=1), 0.0) for h in heads]
    v = [jnp.where(valid, jnp.concatenate([v_ref[VS * h + j] for j in range(VS)], axis=1), 0.0).astype(BF16)
         for h in heads]

    o = [_mm(q[h] * hk(eb, h), s_ref[h]) for h in heads]

    att_ref[...] = jnp.zeros(att_ref.shape, F32)
    nb = C // SB
    for i in range(1, nb):
        lo, hi = SB * i, SB * (i + 1)
        for h in heads:
            bh = hk(b, h)
            beta = bh[lo - 1:lo]
            qi = q[h][lo:hi] * jnp.exp(bh[lo:hi] - beta)
            kj = k[h][:lo] * jnp.exp(beta - bh[:lo])
            att_ref[h, :lo, lo:hi] = _mm(kj, qi, NT)

    srow = _iota((SB, 1), 0)
    for i in range(nb):
        lo = SB * i
        for h in heads:
            bb = hk(b, h)[lo:lo + SB]
            kb = k[h][lo:lo + SB]
            for t in range(SB):
                e = jnp.exp(jnp.where(srow <= t, bb[t:t + 1] - bb, -jnp.inf))
                att_ref[h, lo:lo + SB, lo + t:lo + t + 1] = jnp.sum(
                    q[h][lo + t:lo + t + 1] * kb * e, axis=-1, keepdims=True)

    for h in heads:
        o[h] = o[h] + _dot(att_ref[h].astype(BF16), v[h], TN)

    for h in heads:
        ms = jnp.mean(o[h] * o[h], axis=-1, keepdims=True)
        on = o[h] * lax.rsqrt(ms + NORM_EPS) * nw_ref[...]
        for j in range(VS):
            ob_ref[VS * h + j] = on[:, j * LANE:(j + 1) * LANE] * _silu(zb_ref[VS * h + j])

    for h in heads:
        decay = jnp.exp(b_col[h * B_HK:(h + 1) * B_HK])
        decay = jnp.concatenate([decay] * VS, axis=1)
        s_ref[h] = s_ref[h] * decay + _dot((k[h] * hk(ed, h)).astype(BF16), v[h], TN)

    @pl.when(c == nchunk - 1)
    def _fin():
        for h in heads:
            sout_ref[0, 0, h] = s_ref[h]


def _gla(z3, gw, nw, *, state, prev_ob, prev_state, layer, depth, nseq, nchunk, C, SB, row0, valid_lo, valid_hi):
    rb = lambda n, c: row0 + n * nchunk + c
    state_block = (1, 1, B_HEADS, B_HK, B_HV)
    state_map = lambda n, c: (layer, n, 0, 0, 0)
    qs = B_HEADS * B_HK // LANE
    vs = B_HEADS * B_HV // LANE
    in_specs = [
        pl.BlockSpec((qs, C, LANE), lambda n, c: (SL_QB // qs, rb(n, c), 0)),
        pl.BlockSpec((qs, C, LANE), lambda n, c: (SL_KB // qs, rb(n, c), 0)),
        pl.BlockSpec((vs, C, LANE), lambda n, c: (SL_VB // vs, rb(n, c), 0)),
        pl.BlockSpec((vs, C, LANE), lambda n, c: (SL_ZB // vs, rb(n, c), 0)),
        pl.BlockSpec((1, C, LANE), lambda n, c: (SL_GK, rb(n, c), 0)),
        pl.BlockSpec((LANE, B_HEADS * B_HK), lambda n, c: (0, 0)),
        pl.BlockSpec((1, B_HV), lambda n, c: (0, 0)),
    ]
    args = [z3, z3, z3, z3, z3, gw, nw]
    if state is not None:
        in_specs.append(pl.BlockSpec(state_block, state_map))
        args.append(state)
    aliases = {}
    for out_idx, prev in enumerate((prev_ob, prev_state)):
        if prev is not None:
            aliases[len(args)] = out_idx
            in_specs.append(pl.BlockSpec(memory_space=pl.ANY))
            args.append(prev)
    kern = functools.partial(_gla_kernel, C=C, SB=SB, nchunk=nchunk, valid_lo=valid_lo, valid_hi=valid_hi,
                             zero_init=state is None, n_alias=len(aliases))
    return pl.pallas_call(
        kern,
        grid=(nseq, nchunk),
        in_specs=in_specs,
        out_specs=[
            pl.BlockSpec((vs, C, LANE), lambda n, c: (0, rb(n, c), 0)),
            pl.BlockSpec(state_block, state_map),
        ],
        out_shape=[
            jax.ShapeDtypeStruct((NPAIR, z3.shape[1], LANE), F32),
            jax.ShapeDtypeStruct((depth, nseq, B_HEADS, B_HK, B_HV), F32),
        ],
        input_output_aliases=aliases,
        scratch_shapes=[
            pltpu.VMEM((B_HEADS, B_HK, B_HV), F32),
            pltpu.VMEM((B_HEADS, C, C), F32),
        ],
        compiler_params=pltpu.CompilerParams(
            dimension_semantics=("arbitrary", "arbitrary"), vmem_limit_bytes=VMEM_LIMIT),
        name="gla_c%d" % C,
    )(*args)


def _outproj_kernel(oa_ref, ob_ref, ga_ref, gb_ref, x_ref, w_ref, g_ref, o_ref, m_ref):
    for q in range(NPAIR):
        m = _sigmoid(ga_ref[q]) * oa_ref[q] + _sigmoid(gb_ref[q]) * ob_ref[q]
        m_ref[:, q * LANE:(q + 1) * LANE] = m.astype(BF16)
    y = jnp.dot(m_ref[...], w_ref[...], preferred_element_type=F32)
    ms = jnp.mean(y * y, axis=-1, keepdims=True)
    o_ref[...] = x_ref[...] + y * lax.rsqrt(ms + NORM_EPS) * g_ref[...]


def _outproj(oa3, ob3, z3, x, w, g, tile_m):
    m = x.shape[0]
    return pl.pallas_call(
        _outproj_kernel,
        grid=(m // tile_m,),
        in_specs=[
            pl.BlockSpec((NPAIR, tile_m, LANE), lambda i: (0, i, 0)),
            pl.BlockSpec((NPAIR, tile_m, LANE), lambda i: (0, i, 0)),
            pl.BlockSpec((NPAIR, tile_m, LANE), lambda i: (SL_GA // NPAIR, i, 0)),
            pl.BlockSpec((NPAIR, tile_m, LANE), lambda i: (SL_GB // NPAIR, i, 0)),
            pl.BlockSpec((tile_m, D_MODEL), lambda i: (i, 0)),
            pl.BlockSpec((D_MODEL, D_MODEL), lambda i: (0, 0)),
            pl.BlockSpec((1, D_MODEL), lambda i: (0, 0)),
        ],
        out_specs=pl.BlockSpec((tile_m, D_MODEL), lambda i: (i, 0)),
        out_shape=jax.ShapeDtypeStruct((m, D_MODEL), F32),
        scratch_shapes=[pltpu.VMEM((tile_m, D_MODEL), BF16)],
        compiler_params=pltpu.CompilerParams(
            dimension_semantics=("arbitrary",), vmem_limit_bytes=VMEM_LIMIT),
        name="outproj",
    )(oa3, ob3, z3, z3, x, w, g)


def _pick_tile(m, cap, mult):
    best = mult
    for t in range(mult, cap + 1, mult):
        if m % t == 0:
            best = t
    assert m % best == 0
    return best


def _pad_lanes(x):
    return jnp.pad(x, [(0, 0)] * (x.ndim - 1) + [(0, LANE - x.shape[-1])])


def _shift_rows(zrows):
    n = zrows.shape[1]
    rkv = zrows[:3 * NPAIR].transpose(1, 0, 2).reshape(n, 3 * D_MODEL)
    return jnp.concatenate([rkv, zrows[SL_WLO, :, :LORA], zrows[SL_ALO, :, :LORA]], axis=1)


def kernel(x_prompt, x_sample, state_rwkv, state_shift, state_gla, meta_tokens, norm_pre, w_in, rwkv_mu, rwkv_w0, rwkv_w2, rwkv_a0, rwkv_a2, rwkv_k_k, rwkv_k_a, rwkv_r_k, rwkv_lnx_w, rwkv_lnx_b, gla_gk_w2, gla_gk_b, gla_norm_w, w_out, norm_post):
    depth = w_in.shape[0]
    bp, seq, _ = x_prompt.shape
    bs, dseq, _ = x_sample.shape
    front = CHUNK - N_META
    tp = front + N_META + seq
    assert tp % CHUNK == 0 and dseq <= SAMPLE_ROWS and bs % RWKV_SAMPLE_SEQS == 0
    nchunk = tp // CHUNK
    mp = bp * tp
    ms = bs * SAMPLE_ROWS
    mtot = mp + ms
    tile_in = _pick_tile(mtot, 1184, 16)
    tile_out = _pick_tile(mtot, 296, 8)

    meta = jnp.broadcast_to(meta_tokens[None], (bp, N_META, D_MODEL))
    xp = jnp.concatenate([jnp.zeros((bp, front, D_MODEL), F32), meta, x_prompt], axis=1).reshape(mp, D_MODEL)
    xs = jnp.pad(x_sample, ((0, 0), (0, SAMPLE_ROWS - dseq), (0, 0))).reshape(ms, D_MODEL)
    x = jnp.concatenate([xp, xs], axis=0)

    w_packed = _pack_w_in(w_in)
    w_out_bf = w_out.astype(BF16)
    zero_buf = jnp.zeros((bp, N_BUF_ROWS, LANE), F32)
    last_p = jnp.arange(bp) * tp + tp - 1
    last_s = mp + jnp.arange(bs) * SAMPLE_ROWS + dseq - 1

    sr_p = sr_s = sg_p = sg_s = None
    shift_p, shift_s = [], []
    for l in range(depth):
        mu = rwkv_mu[l]
        par = jnp.stack([mu[:D_MODEL], mu[D_MODEL:2 * D_MODEL], mu[2 * D_MODEL:3 * D_MODEL],
                         rwkv_w0[l], rwkv_a0[l], rwkv_k_k[l], rwkv_k_a[l], rwkv_r_k[l].reshape(-1),
                         rwkv_lnx_w[l], rwkv_lnx_b[l]], axis=0)
        par = jnp.pad(par, ((0, N_PAR_ROWS - par.shape[0]), (0, 0)))
        mulo = jnp.pad(jnp.stack([_pad_lanes(mu[3 * D_MODEL:3 * D_MODEL + LORA]),
                                  _pad_lanes(mu[3 * D_MODEL + LORA:])], axis=0), ((0, 6), (0, 0)))
        w2 = jnp.pad(rwkv_w2[l], ((0, LANE - LORA), (0, 0)))
        a2 = jnp.pad(rwkv_a2[l], ((0, LANE - LORA), (0, 0)))
        gw = jnp.concatenate([gla_gk_w2[l], gla_gk_b[l][None]], axis=0)
        gw = jnp.pad(gw, ((0, LANE - gw.shape[0]), (0, 0)))
        nw = gla_norm_w[l][None]
        sh = state_shift[l]
        buf_s = jnp.concatenate([sh[:, :3 * D_MODEL].reshape(bs, 3 * NPAIR, LANE),
                                 _pad_lanes(sh[:, 3 * D_MODEL:3 * D_MODEL + LORA])[:, None],
                                 _pad_lanes(sh[:, 3 * D_MODEL + LORA:])[:, None]], axis=1)
        buf_s = jnp.pad(buf_s, ((0, 0), (0, N_BUF_ROWS - buf_s.shape[1]), (0, 0)))

        z3 = _inproj(x, norm_pre[l][None], w_packed, l, tile_in)

        prompt = dict(layer=l, depth=depth, nseq=bp, nchunk=nchunk, C=CHUNK, row0=0, valid_lo=front, valid_hi=tp)
        sample = dict(layer=l, depth=depth, nseq=bs, nchunk=1, C=SAMPLE_ROWS, valid_lo=0, valid_hi=dseq)
        oa, sr_p = _rwkv(z3, par, mulo, w2, a2, zero_buf, state=None, prev_oa=None, prev_state=sr_p, nsb=1,
                         **prompt)
        oa, sr_s = _rwkv(z3, par, mulo, w2, a2, buf_s, state=state_rwkv, prev_oa=oa, prev_state=sr_s,
                         nsb=RWKV_SAMPLE_SEQS, row0=mp // (SAMPLE_ROWS * RWKV_SAMPLE_SEQS), **sample)
        ob, sg_p = _gla(z3, gw, nw, state=None, prev_ob=None, prev_state=sg_p, SB=GLA_SUB, **prompt)
        ob, sg_s = _gla(z3, gw, nw, state=state_gla, prev_ob=ob, prev_state=sg_s, SB=SAMPLE_ROWS,
                        row0=mp // SAMPLE_ROWS, **sample)

        x = _outproj(oa, ob, z3, x, w_out_bf[l], norm_post[l][None], tile_out)
        shift_p.append(_shift_rows(z3[:, last_p, :]))
        shift_s.append(_shift_rows(z3[:, last_s, :]))

    y_prompt = x[:mp].reshape(bp, tp, D_MODEL)[:, front + N_META:]
    y_sample = x[mp:].reshape(bs, SAMPLE_ROWS, D_MODEL)[:, :dseq]
    return (y_prompt, y_sample, sr_p, jnp.stack(shift_p), sg_p, sr_s, jnp.stack(shift_s), sg_s)
```

```python
import functools

import jax
import jax.numpy as jnp
from jax import lax
from jax.experimental import pallas as pl
from jax.experimental.pallas import tpu as pltpu

F32 = jnp.float32
BF16 = jnp.bfloat16

D_MODEL = 2048
N_META = 16
NORM_EPS = 1e-6
LNX_EPS = 64e-5
A_HEAD = 64
A_HEADS = D_MODEL // A_HEAD
LORA = 96
B_HEADS = 4
B_HK = 256
B_HV = 512
GATE_LORA = 16
GATE_TAU = 16.0
CHUNK = 64
SAMPLE_ROWS = 8
GLA_SUB = 16
RWKV_SAMPLE_SEQS = 2

LANE = 128
NPAIR = D_MODEL // LANE
SLABS_PER_TILE = 8
TILE_N = SLABS_PER_TILE * LANE
SL_R, SL_K, SL_V, SL_ZA = 0, 16, 32, 48
SL_QB, SL_KB, SL_VB, SL_ZB = 64, 72, 80, 96
SL_GA, SL_GB = 112, 128
SL_WLO, SL_ALO, SL_GK = 144, 145, 146
NSLAB = 152
NCOLP = NSLAB * LANE
N_BUF_ROWS = 56
N_PAR_ROWS = 16
VMEM_LIMIT = 56 * 1024 * 1024

NN = ((1,), (0,))
NT = ((1,), (1,))
TN = ((0,), (0,))


def _dot(a, b, dims=NN):
    return lax.dot_general(a, b, (dims, ((), ())), preferred_element_type=F32)


def _mm(a, b, dims=NN):
    return _dot(a.astype(BF16), b.astype(BF16), dims)


def _split2(x):
    hi = x.astype(BF16)
    lo = (x - hi.astype(F32)).astype(BF16)
    return hi, lo


def _split3(x):
    hi = x.astype(BF16)
    r1 = x - hi.astype(F32)
    mid = r1.astype(BF16)
    lo = (r1 - mid.astype(F32)).astype(BF16)
    return hi, mid, lo


def _mm3(a, b, dims=NN):
    ah, al = _split2(a)
    bh, bl = _split2(b)
    return _dot(ah, bh, dims) + _dot(al, bh, dims) + _dot(ah, bl, dims)


def _sel_right(x, m01):
    hi, lo = _split2(x)
    return _dot(hi, m01) + _dot(lo, m01)


def _cumsum_rows(tri, x):
    hi, mid, lo = _split3(x)
    return _dot(tri, hi) + _dot(tri, mid) + _dot(tri, lo)


def _softplus(x):
    return jnp.maximum(x, 0.0) + jnp.log(1.0 + jnp.exp(-jnp.abs(x)))


def _sigmoid(x):
    return 1.0 / (1.0 + jnp.exp(-x))


def _silu(x):
    return x * _sigmoid(x)


def _log_sigmoid(x):
    return -_softplus(-x)


def _iota(shape, dim):
    return lax.broadcasted_iota(jnp.int32, shape, dim)


W_RKV = 3 * D_MODEL
W_SHIFT = W_RKV + 2 * LORA
W_MID = 3 * D_MODEL
W_GK = W_SHIFT + W_MID
W_TAIL = 3 * D_MODEL


def _pack_w_in(w_in):
    depth, d, ncol = w_in.shape
    assert ncol == W_GK + GATE_LORA + W_TAIL
    wt = jnp.swapaxes(w_in, 1, 2)

    def rows(a, n, pad_to=None):
        x = wt[:, a:a + n]
        return x if pad_to is None else jnp.pad(x, ((0, 0), (0, pad_to - n), (0, 0)))

    parts = [rows(0, W_RKV), rows(W_SHIFT, W_MID), rows(W_GK + GATE_LORA, W_TAIL),
             rows(W_RKV, LORA, LANE), rows(W_RKV + LORA, LORA, LANE), rows(W_GK, GATE_LORA, LANE)]
    packed = jnp.concatenate(parts, axis=1)
    return jnp.pad(packed, ((0, 0), (0, NCOLP - packed.shape[1]), (0, 0))).astype(BF16)


def _inproj_kernel(x_ref, g_ref, w_ref, o_ref, h_ref):
    @pl.when(pl.program_id(1) == 0)
    def _():
        x = x_ref[...]
        ms = jnp.mean(x * x, axis=-1, keepdims=True)
        h_ref[...] = (x * lax.rsqrt(ms + NORM_EPS) * g_ref[...]).astype(BF16)

    acc = _dot(h_ref[...], w_ref[0], NT)
    for q in range(SLABS_PER_TILE):
        o_ref[q] = acc[:, q * LANE:(q + 1) * LANE]


def _inproj(x, g, w, layer, tile_m):
    m = x.shape[0]
    return pl.pallas_call(
        _inproj_kernel,
        grid=(m // tile_m, NCOLP // TILE_N),
        in_specs=[
            pl.BlockSpec((tile_m, D_MODEL), lambda i, j: (i, 0)),
            pl.BlockSpec((1, D_MODEL), lambda i, j: (0, 0)),
            pl.BlockSpec((1, TILE_N, D_MODEL), lambda i, j: (layer, j, 0)),
        ],
        out_specs=pl.BlockSpec((SLABS_PER_TILE, tile_m, LANE), lambda i, j: (j, i, 0)),
        out_shape=jax.ShapeDtypeStruct((NSLAB, m, LANE), F32),
        scratch_shapes=[pltpu.VMEM((tile_m, D_MODEL), BF16)],
        compiler_params=pltpu.CompilerParams(
            dimension_semantics=("arbitrary", "arbitrary"), vmem_limit_bytes=VMEM_LIMIT),
        name="inproj",
    )(x, g, w)


def _rwkv_kernel(*refs, C, nsb, nchunk, valid_lo, valid_hi, zero_init, n_alias):
    it = iter(refs)
    zr_ref, za_ref, lo_ref, par_ref, mulo_ref, w2_ref, a2_ref, buf_ref = (next(it) for _ in range(8))
    sin_ref = None if zero_init else next(it)
    for _ in range(n_alias):
        next(it)
    oa_ref, sout_ref, shift_ref, bd_ref, prev_ref, at_ref, rt_ref, bt_ref, kt_ref, vm_ref, y_ref = it
    c = pl.program_id(1)
    H = A_HEAD
    R = nsb * C
    W = D_MODEL
    units = [(s, p) for s in range(nsb) for p in range(NPAIR)]

    @pl.when(c == 0)
    def _init():
        prev_ref[...] = buf_ref[...]
        z = jnp.zeros((H, H), F32)
        for s, p in units:
            if zero_init:
                bd_ref[s * NPAIR + p] = jnp.zeros((LANE, LANE), F32)
            else:
                se = sin_ref[0, s, 2 * p]
                so = sin_ref[0, s, 2 * p + 1]
                bd_ref[s * NPAIR + p] = jnp.concatenate(
                    [jnp.concatenate([se, z], axis=1), jnp.concatenate([z, so], axis=1)], axis=0)

    row = _iota((R, 1), 0)
    rin = row & (C - 1)
    pos = c * C + rin
    valid = (pos >= valid_lo) & (pos < valid_hi)
    first = pos == valid_lo

    def wide(ref, n0, n):
        return jnp.concatenate([ref[n0 + j] for j in range(n)], axis=1)

    def rowcat(ref, s, n0, n):
        return jnp.concatenate([ref[s, n0 + j:n0 + j + 1, :] for j in range(n)], axis=1)

    def prev_rows(x, n0, n):
        prv = pltpu.roll(x, 1, axis=0)
        for s in range(nsb):
            prv = jnp.where(row == s * C, rowcat(prev_ref, s, n0, n), prv)
            if valid_lo > 0:
                in_s = (row >= s * C) & (row < (s + 1) * C)
                prv = jnp.where(first & in_s, rowcat(buf_ref, s, n0, n), prv)
        return prv

    par = par_ref[...]
    mu_r, mu_k, mu_v = par[0:1], par[1:2], par[2:3]
    w0, a0, k_k, k_a, r_k = par[3:4], par[4:5], par[5:6], par[6:7], par[7:8]
    lnw, lnb = par[8:9], par[9:10]

    rz = wide(zr_ref, 0, NPAIR)
    kz = wide(zr_ref, NPAIR, NPAIR)
    vz = wide(zr_ref, 2 * NPAIR, NPAIR)
    wl_raw = lo_ref[0]
    al_raw = lo_ref[1]
    r = rz + (prev_rows(rz, 0, NPAIR) - rz) * mu_r
    k = kz + (prev_rows(kz, NPAIR, NPAIR) - kz) * mu_k
    v = vz + (prev_rows(vz, 2 * NPAIR, NPAIR) - vz) * mu_v
    tw = jnp.tanh(wl_raw + (prev_rows(wl_raw, 3 * NPAIR, 1) - wl_raw) * mulo_ref[0:1, :])
    xa = al_raw + (prev_rows(al_raw, 3 * NPAIR + 1, 1) - al_raw) * mulo_ref[1:2, :]
    if nchunk > 1:
        for s in range(nsb):
            last = s * C + C - 1
            for j in range(3 * NPAIR):
                prev_ref[s, j:j + 1, :] = zr_ref[j, last:last + 1, :]
            prev_ref[s, 3 * NPAIR:3 * NPAIR + 1, :] = wl_raw[last:last + 1]
            prev_ref[s, 3 * NPAIR + 1:3 * NPAIR + 2, :] = al_raw[last:last + 1]

    gi = _iota((LANE, LANE), 0)
    gj = _iota((LANE, LANE), 1)
    seg01 = jnp.where((gi < H) == (gj < H), 1.0, 0.0).astype(BF16)
    ti = _iota((R, R), 0)
    tj = _iota((R, R), 1)
    tri = jnp.where((tj <= ti) & ((ti & -C) == (tj & -C)), 1.0, 0.0).astype(BF16)
    ri = _iota((2 * C, 2 * C), 0)
    ci = _iota((2 * C, 2 * C), 1)
    rt_ = jnp.where(ri >= C, ri - C, ri)
    ct_ = jnp.where(ci >= C, ci - C, ci)
    same = (ri >= C) == (ci >= C)
    strict = same & (ct_ < rt_)
    incl = same & (ct_ <= rt_)
    eye = jnp.where(ri == ci, 1.0, 0.0)
    left = _iota((C, LANE), 1) < H
    n_dbl = C.bit_length() - 2

    def seg(x):
        pm = jnp.concatenate([x[:, p * LANE:(p + 1) * LANE] for p in range(NPAIR)], axis=0)
        sm = _sel_right(pm, seg01)
        return jnp.concatenate([sm[p * R:(p + 1) * R] for p in range(NPAIR)], axis=1)

    wl = w0 + _mm3(tw, w2_ref[...])
    ld = -jnp.exp(-_softplus(-wl) - 0.5)
    al = _sigmoid(a0 + _mm3(xa, a2_ref[...]))
    kkv = k * k_k
    kkn = kkv / jnp.maximum(jnp.sqrt(seg(kkv * kkv)), 1e-12)
    k2 = k * (1.0 + (al - 1.0) * k_a)
    bonus = seg(r * k2 * r_k) * v

    ld = jnp.where(valid, ld, 0.0)
    L = _cumsum_rows(tri, ld)
    center = C >= 32 and nsb == 1
    lref = L[C // 2 - 1:C // 2] if center else jnp.zeros((1, W), F32)
    e_fwd = jnp.exp(L - lref)
    e_bwd = jnp.exp(lref - L)
    at_ref[...] = -kkn * jnp.exp(L - ld - lref)
    rt_ref[...] = r * e_fwd
    bt_ref[...] = jnp.where(valid, kkn * al, 0.0) * e_bwd
    kt_ref[...] = jnp.where(valid, k2, 0.0) * e_bwd
    vm_ref[...] = jnp.where(valid, v, 0.0)
    e_mid = jnp.exp(lref)
    e_last = [e_fwd[s * C + C - 1:s * C + C] for s in range(nsb)]
    p_last = [jnp.exp(L[s * C + C - 1:s * C + C]) for s in range(nsb)]

    def sl(ref, s, p):
        return ref[s * C:(s + 1) * C, p * LANE:(p + 1) * LANE]

    def lanes(x, p):
        return x[:, p * LANE:(p + 1) * LANE]

    def stack(x):
        return jnp.concatenate([jnp.where(left, x, 0.0), jnp.where(left, 0.0, x)], axis=0)

    def blockdiag(x, mask):
        return jnp.where(mask, jnp.concatenate([x, x], axis=0), 0.0)

    r4, a_ab, a_ak, a_rb, a_rk, inv, apow = {}, {}, {}, {}, {}, {}, {}
    for u in units:
        s, p = u
        at, rt = sl(at_ref, s, p), sl(rt_ref, s, p)
        r4[u] = jnp.concatenate([stack(sl(bt_ref, s, p)), stack(sl(kt_ref, s, p))], axis=0).astype(BF16)
        g = _dot(jnp.concatenate([at, rt], axis=0).astype(BF16), r4[u], NT)
        a_ab[u] = blockdiag(g[:C, :2 * C], strict)
        a_ak[u] = blockdiag(g[:C, 2 * C:], strict)
        a_rb[u] = blockdiag(g[C:, :2 * C], incl)
        a_rk[u] = blockdiag(g[C:, 2 * C:], incl)
        inv[u] = eye + a_ab[u]
        apow[u] = a_ab[u]
    for _ in range(n_dbl):
        for u in units:
            apow[u] = _mm(apow[u], apow[u])
        for u in units:
            inv[u] = inv[u] + _mm(inv[u], apow[u])

    x2, av2, vst, ust = {}, {}, {}, {}
    for u in units:
        s, p = u
        bdm = bd_ref[s * NPAIR + p] * lanes(e_mid, p)
        lhs = jnp.concatenate([stack(sl(at_ref, s, p)), stack(sl(rt_ref, s, p))], axis=0)
        x2[u] = _mm(lhs, bdm, NT)
    for u in units:
        s, p = u
        vst[u] = stack(sl(vm_ref, s, p)).astype(BF16)
        av2[u] = _dot(jnp.concatenate([a_ak[u], a_rk[u]], axis=0).astype(BF16), vst[u])
    for u in units:
        ust[u] = _mm(inv[u], x2[u][:2 * C] + av2[u][:2 * C])
    for u in units:
        s, p = u
        yst = x2[u][2 * C:] + av2[u][2 * C:] + _mm(a_rb[u], ust[u])
        y_ref[s * C:(s + 1) * C, p * LANE:(p + 1) * LANE] = yst[:C] + yst[C:]
    for u in units:
        s, p = u
        upd = _dot(jnp.concatenate([ust[u].astype(BF16), vst[u]], axis=0), r4[u], TN)
        bd_ref[s * NPAIR + p] = bd_ref[s * NPAIR + p] * lanes(p_last[s], p) + upd * lanes(e_last[s], p)

    y = y_ref[...]
    yc = y - seg(y) * (1.0 / H)
    var = seg(yc * yc) * (1.0 / H)
    out = (yc * lax.rsqrt(var + LNX_EPS) * lnw + lnb + bonus) * _silu(wide(za_ref, 0, NPAIR))
    for p in range(NPAIR):
        oa_ref[p] = lanes(out, p)

    @pl.when(c == nchunk - 1)
    def _fin():
        for s, p in units:
            bd = bd_ref[s * NPAIR + p]
            sout_ref[0, s, 2 * p] = bd[:H, :H]
            sout_ref[0, s, 2 * p + 1] = bd[H:, H:]
        shift_ref[...] = jnp.zeros(shift_ref.shape, F32)
        for s in range(nsb):
            last = s * C + (valid_hi - 1) - (nchunk - 1) * C
            for j in range(3 * NPAIR):
                shift_ref[s, j:j + 1, :] = zr_ref[j, last:last + 1, :]
            shift_ref[s, 3 * NPAIR:3 * NPAIR + 1, :] = wl_raw[last:last + 1]
            shift_ref[s, 3 * NPAIR + 1:3 * NPAIR + 2, :] = al_raw[last:last + 1]


def _rwkv(z3, par, mulo, w2, a2, buf, *, state, prev_oa, prev_state, layer, depth, nseq, nsb, nchunk, C, row0,
          valid_lo, valid_hi):
    assert nsb == 1 or nchunk == 1
    R = nsb * C
    rb = lambda n, c: row0 + n * nchunk + c
    state_block = (1, nsb, A_HEADS, A_HEAD, A_HEAD)
    state_map = lambda n, c: (layer, n, 0, 0, 0)
    in_specs = [
        pl.BlockSpec((3 * NPAIR, R, LANE), lambda n, c: (0, rb(n, c), 0)),
        pl.BlockSpec((NPAIR, R, LANE), lambda n, c: (SL_ZA // NPAIR, rb(n, c), 0)),
        pl.BlockSpec((2, R, LANE), lambda n, c: (SL_WLO // 2, rb(n, c), 0)),
        pl.BlockSpec((N_PAR_ROWS, D_MODEL), lambda n, c: (0, 0)),
        pl.BlockSpec((8, LANE), lambda n, c: (0, 0)),
        pl.BlockSpec((LANE, D_MODEL), lambda n, c: (0, 0)),
        pl.BlockSpec((LANE, D_MODEL), lambda n, c: (0, 0)),
        pl.BlockSpec((nsb, N_BUF_ROWS, LANE), lambda n, c: (n, 0, 0)),
    ]
    args = [z3, z3, z3, par, mulo, w2, a2, buf]
    if state is not None:
        in_specs.append(pl.BlockSpec(state_block, state_map))
        args.append(state)
    aliases = {}
    for out_idx, prev in enumerate((prev_oa, prev_state)):
        if prev is not None:
            aliases[len(args)] = out_idx
            in_specs.append(pl.BlockSpec(memory_space=pl.ANY))
            args.append(prev)
    kern = functools.partial(_rwkv_kernel, C=C, nsb=nsb, nchunk=nchunk, valid_lo=valid_lo, valid_hi=valid_hi,
                             zero_init=state is None, n_alias=len(aliases))
    wide_scratch = pltpu.VMEM((R, D_MODEL), F32)
    return pl.pallas_call(
        kern,
        grid=(nseq // nsb, nchunk),
        in_specs=in_specs,
        out_specs=[
            pl.BlockSpec((NPAIR, R, LANE), lambda n, c: (0, rb(n, c), 0)),
            pl.BlockSpec(state_block, state_map),
            pl.BlockSpec((nsb, N_BUF_ROWS, LANE), lambda n, c: (n, 0, 0)),
        ],
        out_shape=[
            jax.ShapeDtypeStruct((NPAIR, z3.shape[1], LANE), F32),
            jax.ShapeDtypeStruct((depth, nseq, A_HEADS, A_HEAD, A_HEAD), F32),
            jax.ShapeDtypeStruct((nseq, N_BUF_ROWS, LANE), F32),
        ],
        input_output_aliases=aliases,
        scratch_shapes=[
            pltpu.VMEM((nsb * NPAIR, LANE, LANE), F32),
            pltpu.VMEM((nsb, N_BUF_ROWS, LANE), F32),
        ] + [wide_scratch] * 6,
        compiler_params=pltpu.CompilerParams(
            dimension_semantics=("arbitrary", "arbitrary"), vmem_limit_bytes=VMEM_LIMIT),
        name="rwkv_c%d" % C,
    )(*args)


def _gla_kernel(*refs, C, SB, nchunk, valid_lo, valid_hi, zero_init, n_alias):
    it = iter(refs)
    q_ref, k_ref, v_ref, zb_ref, gk_ref, gw_ref, nw_ref = (next(it) for _ in range(7))
    sin_ref = None if zero_init else next(it)
    for _ in range(n_alias):
        next(it)
    ob_ref, sout_ref, s_ref, att_ref = it
    c = pl.program_id(1)
    heads = range(B_HEADS)
    QS = B_HK // LANE
    VS = B_HV // LANE

    @pl.when(c == 0)
    def _init():
        for h in heads:
            s_ref[h] = jnp.zeros((B_HK, B_HV), F32) if zero_init else sin_ref[0, 0, h]

    pos = c * C + _iota((C, 1), 0)
    valid = (pos >= valid_lo) & (pos < valid_hi)

    glo1 = jnp.where(_iota((C, LANE), 1) == GATE_LORA, 1.0, gk_ref[0])
    g = jnp.where(valid, _log_sigmoid(_mm3(glo1, gw_ref[...])) / GATE_TAU, 0.0)
    ti = _iota((C, C), 0)
    tj = _iota((C, C), 1)
    tri = jnp.where(tj <= ti, 1.0, 0.0).astype(BF16)
    ones = jnp.ones((C, LANE), BF16)
    parts = _split3(g)
    b = _dot(tri, parts[0]) + _dot(tri, parts[1]) + _dot(tri, parts[2])
    b_col = _dot(parts[0], ones, TN) + _dot(parts[1], ones, TN) + _dot(parts[2], ones, TN)
    eb = jnp.exp(b)
    ed = jnp.exp(b[C - 1:C] - b)

    def hk(x, h):
        return x[:, h * B_HK:(h + 1) * B_HK]

    q = [jnp.concatenate([q_ref[QS * h + j] for j in range(QS)], axis=1) * (B_HK ** -0.5) for h in heads]
    k = [jnp.where(valid, jnp.concatenate([k_ref[QS * h + j] for j in range(QS)], axis=1), 0.0) for h in heads]
    v = [jnp.where(valid, jnp.concatenate([v_ref[VS * h + j] for j in range(VS)], axis=1), 0.0).astype(BF16)
         for h in heads]

    o = [_mm(q[h] * hk(eb, h), s_ref[h]) for h in heads]

    att_ref[...] = jnp.zeros(att_ref.shape, F32)
    nb = C // SB
    for i in range(1, nb):
        lo, hi = SB * i, SB * (i + 1)
        for h in heads:
            bh = hk(b, h)
            beta = bh[lo - 1:lo]
            qi = q[h][lo:hi] * jnp.exp(bh[lo:hi] - beta)
            kj = k[h][:lo] * jnp.exp(beta - bh[:lo])
            att_ref[h, :lo, lo:hi] = _mm(kj, qi, NT)

    srow = _iota((SB, 1), 0)
    for i in range(nb):
        lo = SB * i
        for h in heads:
            bb = hk(b, h)[lo:lo + SB]
            kb = k[h][lo:lo + SB]
            for t in range(SB):
                e = jnp.exp(jnp.where(srow <= t, bb[t:t + 1] - bb, -jnp.inf))
                att_ref[h, lo:lo + SB, lo + t:lo + t + 1] = jnp.sum(
                    q[h][lo + t:lo + t + 1] * kb * e, axis=-1, keepdims=True)

    for h in heads:
        o[h] = o[h] + _dot(att_ref[h].astype(BF16), v[h], TN)

    for h in heads:
        ms = jnp.mean(o[h] * o[h], axis=-1, keepdims=True)
        on = o[h] * lax.rsqrt(ms + NORM_EPS) * nw_ref[...]
        for j in range(VS):
            ob_ref[VS * h + j] = on[:, j * LANE:(j + 1) * LANE] * _silu(zb_ref[VS * h + j])

    for h in heads:
        decay = jnp.exp(b_col[h * B_HK:(h + 1) * B_HK])
        decay = jnp.concatenate([decay] * VS, axis=1)
        s_ref[h] = s_ref[h] * decay + _dot((k[h] * hk(ed, h)).astype(BF16), v[h], TN)

    @pl.when(c == nchunk - 1)
    def _fin():
        for h in heads:
            sout_ref[0, 0, h] = s_ref[h]


def _gla(z3, gw, nw, *, state, prev_ob, prev_state, layer, depth, nseq, nchunk, C, SB, row0, valid_lo, valid_hi):
    rb = lambda n, c: row0 + n * nchunk + c
    state_block = (1, 1, B_HEADS, B_HK, B_HV)
    state_map = lambda n, c: (layer, n, 0, 0, 0)
    qs = B_HEADS * B_HK // LANE
    vs = B_HEADS * B_HV // LANE
    in_specs = [
        pl.BlockSpec((qs, C, LANE), lambda n, c: (SL_QB // qs, rb(n, c), 0)),
        pl.BlockSpec((qs, C, LANE), lambda n, c: (SL_KB // qs, rb(n, c), 0)),
        pl.BlockSpec((vs, C, LANE), lambda n, c: (SL_VB // vs, rb(n, c), 0)),
        pl.BlockSpec((vs, C, LANE), lambda n, c: (SL_ZB // vs, rb(n, c), 0)),
        pl.BlockSpec((1, C, LANE), lambda n, c: (SL_GK, rb(n, c), 0)),
        pl.BlockSpec((LANE, B_HEADS * B_HK), lambda n, c: (0, 0)),
        pl.BlockSpec((1, B_HV), lambda n, c: (0, 0)),
    ]
    args = [z3, z3, z3, z3, z3, gw, nw]
    if state is not None:
        in_specs.append(pl.BlockSpec(state_block, state_map))
        args.append(state)
    aliases = {}
    for out_idx, prev in enumerate((prev_ob, prev_state)):
        if prev is not None:
            aliases[len(args)] = out_idx
            in_specs.append(pl.BlockSpec(memory_space=pl.ANY))
            args.append(prev)
    kern = functools.partial(_gla_kernel, C=C, SB=SB, nchunk=nchunk, valid_lo=valid_lo, valid_hi=valid_hi,
                             zero_init=state is None, n_alias=len(aliases))
    return pl.pallas_call(
        kern,
        grid=(nseq, nchunk),
        in_specs=in_specs,
        out_specs=[
            pl.BlockSpec((vs, C, LANE), lambda n, c: (0, rb(n, c), 0)),
            pl.BlockSpec(state_block, state_map),
        ],
        out_shape=[
            jax.ShapeDtypeStruct((NPAIR, z3.shape[1], LANE), F32),
            jax.ShapeDtypeStruct((depth, nseq, B_HEADS, B_HK, B_HV), F32),
        ],
        input_output_aliases=aliases,
        scratch_shapes=[
            pltpu.VMEM((B_HEADS, B_HK, B_HV), F32),
            pltpu.VMEM((B_HEADS, C, C), F32),
        ],
        compiler_params=pltpu.CompilerParams(
            dimension_semantics=("arbitrary", "arbitrary"), vmem_limit_bytes=VMEM_LIMIT),
        name="gla_c%d" % C,
    )(*args)


def _outproj_kernel(oa_ref, ob_ref, ga_ref, gb_ref, x_ref, w_ref, g_ref, o_ref, m_ref):
    for q in range(NPAIR):
        m = _sigmoid(ga_ref[q]) * oa_ref[q] + _sigmoid(gb_ref[q]) * ob_ref[q]
        m_ref[:, q * LANE:(q + 1) * LANE] = m.astype(BF16)
    y = jnp.dot(m_ref[...], w_ref[...], preferred_element_type=F32)
    ms = jnp.mean(y * y, axis=-1, keepdims=True)
    o_ref[...] = x_ref[...] + y * lax.rsqrt(ms + NORM_EPS) * g_ref[...]


def _outproj(oa3, ob3, z3, x, w, g, tile_m):
    m = x.shape[0]
    return pl.pallas_call(
        _outproj_kernel,
        grid=(m // tile_m,),
        in_specs=[
            pl.BlockSpec((NPAIR, tile_m, LANE), lambda i: (0, i, 0)),
            pl.BlockSpec((NPAIR, tile_m, LANE), lambda i: (0, i, 0)),
            pl.BlockSpec((NPAIR, tile_m, LANE), lambda i: (SL_GA // NPAIR, i, 0)),
            pl.BlockSpec((NPAIR, tile_m, LANE), lambda i: (SL_GB // NPAIR, i, 0)),
            pl.BlockSpec((tile_m, D_MODEL), lambda i: (i, 0)),
            pl.BlockSpec((D_MODEL, D_MODEL), lambda i: (0, 0)),
            pl.BlockSpec((1, D_MODEL), lambda i: (0, 0)),
        ],
        out_specs=pl.BlockSpec((tile_m, D_MODEL), lambda i: (i, 0)),
        out_shape=jax.ShapeDtypeStruct((m, D_MODEL), F32),
        scratch_shapes=[pltpu.VMEM((tile_m, D_MODEL), BF16)],
        compiler_params=pltpu.CompilerParams(
            dimension_semantics=("arbitrary",), vmem_limit_bytes=VMEM_LIMIT),
        name="outproj",
    )(oa3, ob3, z3, z3, x, w, g)


def _pick_tile(m, cap, mult):
    best = mult
    for t in range(mult, cap + 1, mult):
        if m % t == 0:
            best = t
    assert m % best == 0
    return best


def _pad_lanes(x):
    return jnp.pad(x, [(0, 0)] * (x.ndim - 1) + [(0, LANE - x.shape[-1])])


def _shift_rows(buf):
    n = buf.shape[0]
    rkv = buf[:, :3 * NPAIR].reshape(n, 3 * D_MODEL)
    return jnp.concatenate([rkv, buf[:, 3 * NPAIR, :LORA], buf[:, 3 * NPAIR + 1, :LORA]], axis=1)


def kernel(x_prompt, x_sample, state_rwkv, state_shift, state_gla, meta_tokens, norm_pre, w_in, rwkv_mu, rwkv_w0, rwkv_w2, rwkv_a0, rwkv_a2, rwkv_k_k, rwkv_k_a, rwkv_r_k, rwkv_lnx_w, rwkv_lnx_b, gla_gk_w2, gla_gk_b, gla_norm_w, w_out, norm_post):
    depth = w_in.shape[0]
    bp, seq, _ = x_prompt.shape
    bs, dseq, _ = x_sample.shape
    front = CHUNK - N_META
    tp = front + N_META + seq
    assert tp % CHUNK == 0 and dseq <= SAMPLE_ROWS and bs % RWKV_SAMPLE_SEQS == 0
    nchunk = tp // CHUNK
    mp = bp * tp
    ms = bs * SAMPLE_ROWS
    mtot = mp + ms
    tile_in = _pick_tile(mtot, 1184, 16)
    tile_out = _pick_tile(mtot, 296, 8)

    meta = jnp.broadcast_to(meta_tokens[None], (bp, N_META, D_MODEL))
    xp = jnp.concatenate([jnp.zeros((bp, front, D_MODEL), F32), meta, x_prompt], axis=1).reshape(mp, D_MODEL)
    xs = jnp.pad(x_sample, ((0, 0), (0, SAMPLE_ROWS - dseq), (0, 0))).reshape(ms, D_MODEL)
    x = jnp.concatenate([xp, xs], axis=0)

    w_packed = _pack_w_in(w_in)
    w_out_bf = w_out.astype(BF16)
    zero_buf = jnp.zeros((bp, N_BUF_ROWS, LANE), F32)

    sr_p = sr_s = sg_p = sg_s = None
    shift_p, shift_s = [], []
    for l in range(depth):
        mu = rwkv_mu[l]
        par = jnp.stack([mu[:D_MODEL], mu[D_MODEL:2 * D_MODEL], mu[2 * D_MODEL:3 * D_MODEL],
                         rwkv_w0[l], rwkv_a0[l], rwkv_k_k[l], rwkv_k_a[l], rwkv_r_k[l].reshape(-1),
                         rwkv_lnx_w[l], rwkv_lnx_b[l]], axis=0)
        par = jnp.pad(par, ((0, N_PAR_ROWS - par.shape[0]), (0, 0)))
        mulo = jnp.pad(jnp.stack([_pad_lanes(mu[3 * D_MODEL:3 * D_MODEL + LORA]),
                                  _pad_lanes(mu[3 * D_MODEL + LORA:])], axis=0), ((0, 6), (0, 0)))
        w2 = jnp.pad(rwkv_w2[l], ((0, LANE - LORA), (0, 0)))
        a2 = jnp.pad(rwkv_a2[l], ((0, LANE - LORA), (0, 0)))
        gw = jnp.concatenate([gla_gk_w2[l], gla_gk_b[l][None]], axis=0)
        gw = jnp.pad(gw, ((0, LANE - gw.shape[0]), (0, 0)))
        nw = gla_norm_w[l][None]
        sh = state_shift[l]
        buf_s = jnp.concatenate([sh[:, :3 * D_MODEL].reshape(bs, 3 * NPAIR, LANE),
                                 _pad_lanes(sh[:, 3 * D_MODEL:3 * D_MODEL + LORA])[:, None],
                                 _pad_lanes(sh[:, 3 * D_MODEL + LORA:])[:, None]], axis=1)
        buf_s = jnp.pad(buf_s, ((0, 0), (0, N_BUF_ROWS - buf_s.shape[1]), (0, 0)))

        z3 = _inproj(x, norm_pre[l][None], w_packed, l, tile_in)

        prompt = dict(layer=l, depth=depth, nseq=bp, nchunk=nchunk, C=CHUNK, row0=0, valid_lo=front, valid_hi=tp)
        sample = dict(layer=l, depth=depth, nseq=bs, nchunk=1, C=SAMPLE_ROWS, valid_lo=0, valid_hi=dseq)
        oa, sr_p, sh_p = _rwkv(z3, par, mulo, w2, a2, zero_buf, state=None, prev_oa=None, prev_state=sr_p, nsb=1,
                               **prompt)
        oa, sr_s, sh_s = _rwkv(z3, par, mulo, w2, a2, buf_s, state=state_rwkv, prev_oa=oa, prev_state=sr_s,
                               nsb=RWKV_SAMPLE_SEQS, row0=mp // (SAMPLE_ROWS * RWKV_SAMPLE_SEQS), **sample)
        ob, sg_p = _gla(z3, gw, nw, state=None, prev_ob=None, prev_state=sg_p, SB=GLA_SUB, **prompt)
        ob, sg_s = _gla(z3, gw, nw, state=state_gla, prev_ob=ob, prev_state=sg_s, SB=SAMPLE_ROWS,
                        row0=mp // SAMPLE_ROWS, **sample)

        x = _outproj(oa, ob, z3, x, w_out_bf[l], norm_post[l][None], tile_out)
        shift_p.append(_shift_rows(sh_p))
        shift_s.append(_shift_rows(sh_s))

    y_prompt = x[:mp].reshape(bp, tp, D_MODEL)[:, front + N_META:]
    y_sample = x[mp:].reshape(bs, SAMPLE_ROWS, D_MODEL)[:, :dseq]
    return (y_prompt, y_sample, sr_p, jnp.stack(shift_p), sg_p, sr_s, jnp.stack(shift_s), sg_s)
```

```python
import functools

import jax
import jax.numpy as jnp
from jax import lax
from jax.experimental import pallas as pl
from jax.experimental.pallas import tpu as pltpu

F32 = jnp.float32
BF16 = jnp.bfloat16

D_MODEL = 2048
N_META = 16
NORM_EPS = 1e-6
LNX_EPS = 64e-5
A_HEAD = 64
A_HEADS = D_MODEL // A_HEAD
LORA = 96
B_HEADS = 4
B_HK = 256
B_HV = 512
GATE_LORA = 16
GATE_TAU = 16.0
CHUNK = 64
SAMPLE_ROWS = 8
GLA_SUB = 16
RWKV_SAMPLE_SEQS = 2
RWKV_PROMPT_SEQS = 2
RWKV_UNIT_HEADS = 4

LANE = 128
NPAIR = D_MODEL // LANE
SLABS_PER_TILE = 8
TILE_N = SLABS_PER_TILE * LANE
SL_R, SL_K, SL_V, SL_ZA = 0, 16, 32, 48
SL_QB, SL_KB, SL_VB, SL_ZB = 64, 72, 80, 96
SL_GA, SL_GB = 112, 128
SL_WLO, SL_ALO, SL_GK = 144, 145, 146
NSLAB = 152
NCOLP = NSLAB * LANE
N_BUF_ROWS = 56
N_PAR_ROWS = 16
VMEM_LIMIT = 56 * 1024 * 1024

NN = ((1,), (0,))
NT = ((1,), (1,))
TN = ((0,), (0,))


def _dot(a, b, dims=NN):
    return lax.dot_general(a, b, (dims, ((), ())), preferred_element_type=F32)


def _mm(a, b, dims=NN):
    return _dot(a.astype(BF16), b.astype(BF16), dims)


def _split2(x):
    hi = x.astype(BF16)
    lo = (x - hi.astype(F32)).astype(BF16)
    return hi, lo


def _mm3(a, b_hi, b_lo, dims=NN):
    ah, al = _split2(a)
    return _dot(ah, b_hi, dims) + _dot(al, b_hi, dims) + _dot(ah, b_lo, dims)


def _cumsum_rows(tri, x):
    hi, lo = _split2(x)
    return _dot(tri, hi) + _dot(tri, lo)


def _sigmoid(x):
    return 1.0 / (1.0 + jnp.exp(-x))


def _silu(x):
    return x * _sigmoid(x)


def _log_sigmoid(x):
    return jnp.minimum(x, 0.0) - jnp.log(1.0 + jnp.exp(-jnp.abs(x)))


def _iota(shape, dim):
    return lax.broadcasted_iota(jnp.int32, shape, dim)


W_RKV = 3 * D_MODEL
W_SHIFT = W_RKV + 2 * LORA
W_MID = 3 * D_MODEL
W_GK = W_SHIFT + W_MID
W_TAIL = 3 * D_MODEL


def _pack_w_in(w_in):
    depth, d, ncol = w_in.shape
    assert ncol == W_GK + GATE_LORA + W_TAIL
    wt = jnp.swapaxes(w_in, 1, 2)

    def rows(a, n, pad_to=None):
        x = wt[:, a:a + n]
        return x if pad_to is None else jnp.pad(x, ((0, 0), (0, pad_to - n), (0, 0)))

    parts = [rows(0, W_RKV), rows(W_SHIFT, W_MID), rows(W_GK + GATE_LORA, W_TAIL),
             rows(W_RKV, LORA, LANE), rows(W_RKV + LORA, LORA, LANE), rows(W_GK, GATE_LORA, LANE)]
    packed = jnp.concatenate(parts, axis=1)
    return jnp.pad(packed, ((0, 0), (0, NCOLP - packed.shape[1]), (0, 0))).astype(BF16)


def _inproj_kernel(x_ref, g_ref, w_ref, o_ref, h_ref):
    @pl.when(pl.program_id(1) == 0)
    def _():
        x = x_ref[...]
        ms = jnp.mean(x * x, axis=-1, keepdims=True)
        h_ref[...] = (x * lax.rsqrt(ms + NORM_EPS) * g_ref[...]).astype(BF16)

    acc = _dot(h_ref[...], w_ref[0], NT)
    for q in range(SLABS_PER_TILE):
        o_ref[q] = acc[:, q * LANE:(q + 1) * LANE]


def _inproj(x, g, w, layer, tile_m):
    m = x.shape[0]
    return pl.pallas_call(
        _inproj_kernel,
        grid=(m // tile_m, NCOLP // TILE_N),
        in_specs=[
            pl.BlockSpec((tile_m, D_MODEL), lambda i, j: (i, 0)),
            pl.BlockSpec((1, D_MODEL), lambda i, j: (0, 0)),
            pl.BlockSpec((1, TILE_N, D_MODEL), lambda i, j: (layer, j, 0)),
        ],
        out_specs=pl.BlockSpec((SLABS_PER_TILE, tile_m, LANE), lambda i, j: (j, i, 0)),
        out_shape=jax.ShapeDtypeStruct((NSLAB, m, LANE), F32),
        scratch_shapes=[pltpu.VMEM((tile_m, D_MODEL), BF16)],
        compiler_params=pltpu.CompilerParams(
            dimension_semantics=("arbitrary", "arbitrary"), vmem_limit_bytes=VMEM_LIMIT),
        name="inproj",
    )(x, g, w)


def _rwkv_kernel(*refs, C, nsb, nchunk, valid_lo, valid_hi, zero_init, n_alias):
    it = iter(refs)
    zr_ref, za_ref, lo_ref, par_ref, mulo_ref, w2h_ref, w2l_ref, a2h_ref, a2l_ref, buf_ref = (
        next(it) for _ in range(10))
    sin_ref = None if zero_init else next(it)
    for _ in range(n_alias):
        next(it)
    oa_ref, sout_ref, shift_ref, bd_ref, prev_ref, at_ref, rt_ref, bt_ref, kt_ref, vm_ref, y_ref = it
    c = pl.program_id(1)
    H = A_HEAD
    R = nsb * C
    W = D_MODEL
    NH = RWKV_UNIT_HEADS
    QL = NH * H
    NQ = W // QL
    units = [(s, q) for s in range(nsb) for q in range(NQ)]

    @pl.when(c == 0)
    def _init():
        prev_ref[...] = buf_ref[...]
        for s, q in units:
            if zero_init:
                bd_ref[s * NQ + q] = jnp.zeros((QL, QL), F32)
            else:
                blocks = []
                for h in range(NH):
                    pieces = [jnp.zeros((H, H), F32)] * NH
                    pieces[h] = sin_ref[0, s, NH * q + h]
                    blocks.append(jnp.concatenate(pieces, axis=1))
                bd_ref[s * NQ + q] = jnp.concatenate(blocks, axis=0)

    row = _iota((R, 1), 0)
    rin = row & (C - 1)
    pos = c * C + rin
    valid = (pos >= valid_lo) & (pos < valid_hi)

    def wide(ref, n0, n):
        return jnp.concatenate([ref[n0 + j] for j in range(n)], axis=1)

    def rowcat(ref, s, n0, n):
        return jnp.concatenate([ref[s, n0 + j:n0 + j + 1, :] for j in range(n)], axis=1)

    SUB = 8
    row8 = _iota((SUB, 1), 0)

    def prev_rows(x, n0, n):
        rolled = pltpu.roll(x, 1, axis=0)
        patches = {}
        for s in range(nsb):
            t0 = s * C
            patches[t0] = jnp.where(row8 == 0, rowcat(prev_ref, s, n0, n), rolled[t0:t0 + SUB])
            if valid_lo > 0:
                t1 = s * C + (valid_lo % C) // SUB * SUB
                hit = (c == valid_lo // C) & (row8 == valid_lo % SUB)
                patches[t1] = jnp.where(hit, rowcat(buf_ref, s, n0, n), patches.get(t1, rolled[t1:t1 + SUB]))
        pieces, t = [], 0
        for t0 in sorted(patches):
            if t0 > t:
                pieces.append(rolled[t:t0])
            pieces.append(patches[t0])
            t = t0 + SUB
        if t < R:
            pieces.append(rolled[t:])
        return jnp.concatenate(pieces, axis=0)

    par = par_ref[...]
    mu_r, mu_k, mu_v = par[0:1], par[1:2], par[2:3]
    w0, a0, k_k, k_a, r_k = par[3:4], par[4:5], par[5:6], par[6:7], par[7:8]
    lnw, lnb = par[8:9], par[9:10]

    rz = wide(zr_ref, 0, NPAIR)
    kz = wide(zr_ref, NPAIR, NPAIR)
    vz = wide(zr_ref, 2 * NPAIR, NPAIR)
    wl_raw = lo_ref[0]
    al_raw = lo_ref[1]
    r = rz + (prev_rows(rz, 0, NPAIR) - rz) * mu_r
    k = kz + (prev_rows(kz, NPAIR, NPAIR) - kz) * mu_k
    v = vz + (prev_rows(vz, 2 * NPAIR, NPAIR) - vz) * mu_v
    tw = jnp.tanh(wl_raw + (prev_rows(wl_raw, 3 * NPAIR, 1) - wl_raw) * mulo_ref[0:1, :])
    xa = al_raw + (prev_rows(al_raw, 3 * NPAIR + 1, 1) - al_raw) * mulo_ref[1:2, :]
    if nchunk > 1:
        for s in range(nsb):
            last = s * C + C - 1
            for j in range(3 * NPAIR):
                prev_ref[s, j:j + 1, :] = zr_ref[j, last:last + 1, :]
            prev_ref[s, 3 * NPAIR:3 * NPAIR + 1, :] = wl_raw[last:last + 1]
            prev_ref[s, 3 * NPAIR + 1:3 * NPAIR + 2, :] = al_raw[last:last + 1]

    gi = _iota((LANE, LANE), 0)
    gj = _iota((LANE, LANE), 1)
    seg01 = jnp.where((gi < H) == (gj < H), 1.0, 0.0).astype(BF16)
    ti = _iota((R, R), 0)
    tj = _iota((R, R), 1)
    tri = jnp.where((tj <= ti) & ((ti & -C) == (tj & -C)), 1.0, 0.0).astype(BF16)
    SW = NH * C
    ri = _iota((C, SW), 0)
    ci = _iota((C, SW), 1) & (C - 1)
    strict = ci < ri
    incl = ci <= ri
    eye = jnp.where(ri == ci, 1.0, 0.0)
    lane_i = _iota((1, SW), 1)
    lane_c = _iota((1, QL), 1)
    head_i = [(lane_i >= C * h) & (lane_i < C * (h + 1)) for h in range(NH)]
    head_c = [(lane_c >= H * h) & (lane_c < H * (h + 1)) for h in range(NH)]
    bd_mask = (_iota((QL, QL), 0) & -H) == (_iota((QL, QL), 1) & -H)
    n_dbl = C.bit_length() - 2

    def seg(x):
        pm = jnp.concatenate([x[:, p * LANE:(p + 1) * LANE] for p in range(NPAIR)], axis=0)
        sm = _dot(pm.astype(BF16), seg01)
        return jnp.concatenate([sm[p * R:(p + 1) * R] for p in range(NPAIR)], axis=1)

    wl = w0 + _mm3(tw, w2h_ref[...], w2l_ref[...])
    ld = -jnp.exp(_log_sigmoid(wl) - 0.5)
    al = _sigmoid(a0 + _mm3(xa, a2h_ref[...], a2l_ref[...]))
    kkv = k * k_k
    kkn = kkv / jnp.maximum(jnp.sqrt(seg(kkv * kkv)), 1e-12)
    k2 = k * (1.0 + (al - 1.0) * k_a)
    bonus = seg(r * k2 * r_k) * v

    ld = jnp.where(valid, ld, 0.0)
    L = _cumsum_rows(tri, ld)
    if C >= 32:
        mid = [L[s * C + C // 2 - 1:s * C + C // 2] for s in range(nsb)]
        lref = mid[0]
        for s in range(1, nsb):
            lref = jnp.where(row >= s * C, mid[s], lref)
    else:
        mid = [jnp.zeros((1, W), F32)] * nsb
        lref = mid[0]
    e_fwd = jnp.exp(L - lref)
    e_bwd = jnp.exp(lref - L)
    at_ref[...] = -kkn * jnp.exp(L - ld - lref)
    rt_ref[...] = r * e_fwd
    bt_ref[...] = jnp.where(valid, kkn * al, 0.0) * e_bwd
    kt_ref[...] = jnp.where(valid, k2, 0.0) * e_bwd
    vm_ref[...] = jnp.where(valid, v, 0.0)
    e_mid = [jnp.exp(m) for m in mid]
    e_last = [e_fwd[s * C + C - 1:s * C + C] for s in range(nsb)]
    p_last = [jnp.exp(L[s * C + C - 1:s * C + C]) for s in range(nsb)]

    def sl(ref, s, q):
        return ref[s * C:(s + 1) * C, q * QL:(q + 1) * QL]

    def lanes(x, q):
        return x[:, q * QL:(q + 1) * QL]

    def blockdiag(x, masks):
        return jnp.concatenate([jnp.where(m, x, jnp.zeros_like(x)) for m in masks], axis=0)

    lhs_ar, rhs_bk, a_ab, a_ak, a_rb, a_rk, inv, apow = {}, {}, {}, {}, {}, {}, {}, {}
    for u in units:
        s, q = u
        lhs_ar[u] = jnp.concatenate([sl(at_ref, s, q), sl(rt_ref, s, q)], axis=0).astype(BF16)
        bt = sl(bt_ref, s, q).astype(BF16)
        kt = sl(kt_ref, s, q).astype(BF16)
        rhs_bk[u] = jnp.concatenate([bt, kt], axis=0)
        g = _dot(lhs_ar[u], jnp.concatenate([blockdiag(bt, head_c), blockdiag(kt, head_c)], axis=0), NT)
        a_ab[u] = jnp.where(strict, g[:C, :SW], 0.0)
        a_ak[u] = jnp.where(strict, g[:C, SW:], 0.0)
        a_rb[u] = jnp.where(incl, g[C:, :SW], 0.0)
        a_rk[u] = jnp.where(incl, g[C:, SW:], 0.0)
        inv[u] = eye + a_ab[u]
        apow[u] = a_ab[u]
    for u in units:
        pw = apow[u].astype(BF16)
        apow[u] = _dot(pw, blockdiag(pw, head_i))
    for lvl in range(n_dbl):
        for u in units:
            pw = apow[u].astype(BF16)
            rhs = blockdiag(pw, head_i)
            if lvl < n_dbl - 1:
                both = _dot(jnp.concatenate([pw, inv[u].astype(BF16)], axis=0), rhs)
                apow[u] = both[:C]
                inv[u] = inv[u] + both[C:]
            else:
                inv[u] = inv[u] + _dot(inv[u].astype(BF16), rhs)

    x2, av2, vm, uu = {}, {}, {}, {}
    for u in units:
        s, q = u
        bdm = bd_ref[s * NQ + q] * lanes(e_mid[s], q)
        x2[u] = _dot(lhs_ar[u], bdm.astype(BF16), NT)
    for u in units:
        s, q = u
        vm[u] = sl(vm_ref, s, q).astype(BF16)
        av2[u] = _dot(jnp.concatenate([a_ak[u], a_rk[u]], axis=0).astype(BF16), blockdiag(vm[u], head_c))
    for u in units:
        w1a = (x2[u][:C] + av2[u][:C]).astype(BF16)
        uu[u] = _dot(inv[u].astype(BF16), blockdiag(w1a, head_c)).astype(BF16)
    for u in units:
        s, q = u
        y_ref[s * C:(s + 1) * C, q * QL:(q + 1) * QL] = (
            x2[u][C:] + av2[u][C:] + _dot(a_rb[u].astype(BF16), blockdiag(uu[u], head_c)))
    for u in units:
        s, q = u
        upd = _dot(jnp.concatenate([uu[u], vm[u]], axis=0), rhs_bk[u], TN)
        bd = bd_ref[s * NQ + q] * lanes(p_last[s], q) + upd * lanes(e_last[s], q)
        bd_ref[s * NQ + q] = jnp.where(bd_mask, bd, 0.0)

    y = y_ref[...]
    yc = y - seg(y) * (1.0 / H)
    var = seg(yc * yc) * (1.0 / H)
    out = (yc * lax.rsqrt(var + LNX_EPS) * lnw + lnb + bonus) * _silu(wide(za_ref, 0, NPAIR))
    for p in range(NPAIR):
        oa_ref[p] = out[:, p * LANE:(p + 1) * LANE]

    @pl.when(c == nchunk - 1)
    def _fin():
        for s, q in units:
            bd = bd_ref[s * NQ + q]
            for h in range(NH):
                sout_ref[0, s, NH * q + h] = bd[h * H:(h + 1) * H, h * H:(h + 1) * H]
        shift_ref[...] = jnp.zeros(shift_ref.shape, F32)
        for s in range(nsb):
            last = s * C + (valid_hi - 1) - (nchunk - 1) * C
            for j in range(3 * NPAIR):
                shift_ref[s, j:j + 1, :] = zr_ref[j, last:last + 1, :]
            shift_ref[s, 3 * NPAIR:3 * NPAIR + 1, :] = wl_raw[last:last + 1]
            shift_ref[s, 3 * NPAIR + 1:3 * NPAIR + 2, :] = al_raw[last:last + 1]


def _rwkv(z3, par, mulo, w2, a2, buf, *, state, prev_oa, prev_state, layer, depth, nseq, nsb, nchunk, C, row0,
          valid_lo, valid_hi):
    R = nsb * C
    rb = lambda n, c: row0 + c * (nseq // nsb) + n
    state_block = (1, nsb, A_HEADS, A_HEAD, A_HEAD)
    state_map = lambda n, c: (layer, n, 0, 0, 0)
    in_specs = [
        pl.BlockSpec((3 * NPAIR, R, LANE), lambda n, c: (0, rb(n, c), 0)),
        pl.BlockSpec((NPAIR, R, LANE), lambda n, c: (SL_ZA // NPAIR, rb(n, c), 0)),
        pl.BlockSpec((2, R, LANE), lambda n, c: (SL_WLO // 2, rb(n, c), 0)),
        pl.BlockSpec((N_PAR_ROWS, D_MODEL), lambda n, c: (0, 0)),
        pl.BlockSpec((8, LANE), lambda n, c: (0, 0)),
        pl.BlockSpec((LANE, D_MODEL), lambda n, c: (0, 0)),
        pl.BlockSpec((LANE, D_MODEL), lambda n, c: (0, 0)),
        pl.BlockSpec((LANE, D_MODEL), lambda n, c: (0, 0)),
        pl.BlockSpec((LANE, D_MODEL), lambda n, c: (0, 0)),
        pl.BlockSpec((nsb, N_BUF_ROWS, LANE), lambda n, c: (n, 0, 0)),
    ]
    args = [z3, z3, z3, par, mulo, *w2, *a2, buf]
    if state is not None:
        in_specs.append(pl.BlockSpec(state_block, state_map))
        args.append(state)
    aliases = {}
    for out_idx, prev in enumerate((prev_oa, prev_state)):
        if prev is not None:
            aliases[len(args)] = out_idx
            in_specs.append(pl.BlockSpec(memory_space=pl.ANY))
            args.append(prev)
    kern = functools.partial(_rwkv_kernel, C=C, nsb=nsb, nchunk=nchunk, valid_lo=valid_lo, valid_hi=valid_hi,
                             zero_init=state is None, n_alias=len(aliases))
    wide_scratch = pltpu.VMEM((R, D_MODEL), F32)
    unit_lanes = RWKV_UNIT_HEADS * A_HEAD
    return pl.pallas_call(
        kern,
        grid=(nseq // nsb, nchunk),
        in_specs=in_specs,
        out_specs=[
            pl.BlockSpec((NPAIR, R, LANE), lambda n, c: (0, rb(n, c), 0)),
            pl.BlockSpec(state_block, state_map),
            pl.BlockSpec((nsb, N_BUF_ROWS, LANE), lambda n, c: (n, 0, 0)),
        ],
        out_shape=[
            jax.ShapeDtypeStruct((NPAIR, z3.shape[1], LANE), F32),
            jax.ShapeDtypeStruct((depth, nseq, A_HEADS, A_HEAD, A_HEAD), F32),
            jax.ShapeDtypeStruct((nseq, N_BUF_ROWS, LANE), F32),
        ],
        input_output_aliases=aliases,
        scratch_shapes=[
            pltpu.VMEM((nsb * D_MODEL // unit_lanes, unit_lanes, unit_lanes), F32),
            pltpu.VMEM((nsb, N_BUF_ROWS, LANE), F32),
        ] + [wide_scratch] * 6,
        compiler_params=pltpu.CompilerParams(
            dimension_semantics=("arbitrary", "arbitrary"), vmem_limit_bytes=VMEM_LIMIT),
        name="rwkv_c%d" % C,
    )(*args)


def _gla_kernel(*refs, C, SB, nchunk, valid_lo, valid_hi, zero_init, n_alias):
    it = iter(refs)
    q_ref, k_ref, v_ref, zb_ref, gk_ref, gwh_ref, gwl_ref, nw_ref = (next(it) for _ in range(8))
    sin_ref = None if zero_init else next(it)
    for _ in range(n_alias):
        next(it)
    ob_ref, sout_ref, s_ref, att_ref = it
    c = pl.program_id(1)
    heads = range(B_HEADS)
    QS = B_HK // LANE
    VS = B_HV // LANE

    @pl.when(c == 0)
    def _init():
        for h in heads:
            s_ref[h] = jnp.zeros((B_HK, B_HV), F32) if zero_init else sin_ref[0, 0, h]

    pos = c * C + _iota((C, 1), 0)
    valid = (pos >= valid_lo) & (pos < valid_hi)

    glo1 = jnp.where(_iota((C, LANE), 1) == GATE_LORA, 1.0, gk_ref[0])
    g = jnp.where(valid, _log_sigmoid(_mm3(glo1, gwh_ref[...], gwl_ref[...])) / GATE_TAU, 0.0)
    ti = _iota((C, C), 0)
    tj = _iota((C, C), 1)
    tri = jnp.where(tj <= ti, 1.0, 0.0).astype(BF16)
    ones = jnp.ones((C, LANE), BF16)
    parts = _split2(g)
    b = _dot(tri, parts[0]) + _dot(tri, parts[1])
    b_col = _dot(parts[0], ones, TN) + _dot(parts[1], ones, TN)
    eb = jnp.exp(b)
    ed = jnp.exp(b[C - 1:C] - b)

    def hk(x, h):
        return x[:, h * B_HK:(h + 1) * B_HK]

    q = [jnp.concatenate([q_ref[QS * h + j] for j in range(QS)], axis=1) * (B_HK ** -0.5) for h in heads]
    k = [jnp.where(valid, jnp.concatenate([k_ref[QS * h + j] for j in range(QS)], axis=1), 0.0) for h in heads]
    v = [jnp.where(valid, jnp.concatenate([v_ref[VS * h + j] for j in range(VS)], axis=1), 0.0).astype(BF16)
         for h in heads]

    o = [_mm(q[h] * hk(eb, h), s_ref[h]) for h in heads]

    att_ref[...] = jnp.zeros(att_ref.shape, F32)
    nb = C // SB
    for i in range(1, nb):
        lo, hi = SB * i, SB * (i + 1)
        for h in heads:
            bh = hk(b, h)
            beta = bh[lo - 1:lo]
            qi = q[h][lo:hi] * jnp.exp(bh[lo:hi] - beta)
            kj = k[h][:lo] * jnp.exp(beta - bh[:lo])
            att_ref[h, :lo, lo:hi] = _mm(kj, qi, NT)

    srow = _iota((SB, 1), 0)
    for i in range(nb):
        lo = SB * i
        for h in heads:
            bb = hk(b, h)[lo:lo + SB]
            kb = k[h][lo:lo + SB]
            for t in range(SB):
                e = jnp.exp(jnp.where(srow <= t, bb[t:t + 1] - bb, -jnp.inf))
                att_ref[h, lo:lo + SB, lo + t:lo + t + 1] = jnp.sum(
                    q[h][lo + t:lo + t + 1] * kb * e, axis=-1, keepdims=True)

    for h in heads:
        o[h] = o[h] + _dot(att_ref[h].astype(BF16), v[h], TN)

    for h in heads:
        ms = jnp.mean(o[h] * o[h], axis=-1, keepdims=True)
        on = o[h] * lax.rsqrt(ms + NORM_EPS) * nw_ref[...]
        for j in range(VS):
            ob_ref[VS * h + j] = on[:, j * LANE:(j + 1) * LANE] * _silu(zb_ref[VS * h + j])

    for h in heads:
        decay = jnp.exp(b_col[h * B_HK:(h + 1) * B_HK])
        decay = jnp.concatenate([decay] * VS, axis=1)
        s_ref[h] = s_ref[h] * decay + _dot((k[h] * hk(ed, h)).astype(BF16), v[h], TN)

    @pl.when(c == nchunk - 1)
    def _fin():
        for h in heads:
            sout_ref[0, 0, h] = s_ref[h]


def _gla(z3, gw, nw, *, state, prev_ob, prev_state, layer, depth, nseq, nchunk, C, SB, row0, valid_lo, valid_hi):
    rb = lambda n, c: row0 + c * nseq + n
    state_block = (1, 1, B_HEADS, B_HK, B_HV)
    state_map = lambda n, c: (layer, n, 0, 0, 0)
    qs = B_HEADS * B_HK // LANE
    vs = B_HEADS * B_HV // LANE
    in_specs = [
        pl.BlockSpec((qs, C, LANE), lambda n, c: (SL_QB // qs, rb(n, c), 0)),
        pl.BlockSpec((qs, C, LANE), lambda n, c: (SL_KB // qs, rb(n, c), 0)),
        pl.BlockSpec((vs, C, LANE), lambda n, c: (SL_VB // vs, rb(n, c), 0)),
        pl.BlockSpec((vs, C, LANE), lambda n, c: (SL_ZB // vs, rb(n, c), 0)),
        pl.BlockSpec((1, C, LANE), lambda n, c: (SL_GK, rb(n, c), 0)),
        pl.BlockSpec((LANE, B_HEADS * B_HK), lambda n, c: (0, 0)),
        pl.BlockSpec((LANE, B_HEADS * B_HK), lambda n, c: (0, 0)),
        pl.BlockSpec((1, B_HV), lambda n, c: (0, 0)),
    ]
    args = [z3, z3, z3, z3, z3, *gw, nw]
    if state is not None:
        in_specs.append(pl.BlockSpec(state_block, state_map))
        args.append(state)
    aliases = {}
    for out_idx, prev in enumerate((prev_ob, prev_state)):
        if prev is not None:
            aliases[len(args)] = out_idx
            in_specs.append(pl.BlockSpec(memory_space=pl.ANY))
            args.append(prev)
    kern = functools.partial(_gla_kernel, C=C, SB=SB, nchunk=nchunk, valid_lo=valid_lo, valid_hi=valid_hi,
                             zero_init=state is None, n_alias=len(aliases))
    return pl.pallas_call(
        kern,
        grid=(nseq, nchunk),
        in_specs=in_specs,
        out_specs=[
            pl.BlockSpec((vs, C, LANE), lambda n, c: (0, rb(n, c), 0)),
            pl.BlockSpec(state_block, state_map),
        ],
        out_shape=[
            jax.ShapeDtypeStruct((NPAIR, z3.shape[1], LANE), F32),
            jax.ShapeDtypeStruct((depth, nseq, B_HEADS, B_HK, B_HV), F32),
        ],
        input_output_aliases=aliases,
        scratch_shapes=[
            pltpu.VMEM((B_HEADS, B_HK, B_HV), F32),
            pltpu.VMEM((B_HEADS, C, C), F32),
        ],
        compiler_params=pltpu.CompilerParams(
            dimension_semantics=("arbitrary", "arbitrary"), vmem_limit_bytes=VMEM_LIMIT),
        name="gla_c%d" % C,
    )(*args)


def _outproj_kernel(oa_ref, ob_ref, ga_ref, gb_ref, x_ref, w_ref, g_ref, o_ref, m_ref):
    for q in range(NPAIR):
        m = _sigmoid(ga_ref[q]) * oa_ref[q] + _sigmoid(gb_ref[q]) * ob_ref[q]
        m_ref[:, q * LANE:(q + 1) * LANE] = m.astype(BF16)
    y = jnp.dot(m_ref[...], w_ref[...], preferred_element_type=F32)
    ms = jnp.mean(y * y, axis=-1, keepdims=True)
    o_ref[...] = x_ref[...] + y * lax.rsqrt(ms + NORM_EPS) * g_ref[...]


def _outproj(oa3, ob3, z3, x, w, g, tile_m):
    m = x.shape[0]
    return pl.pallas_call(
        _outproj_kernel,
        grid=(m // tile_m,),
        in_specs=[
            pl.BlockSpec((NPAIR, tile_m, LANE), lambda i: (0, i, 0)),
            pl.BlockSpec((NPAIR, tile_m, LANE), lambda i: (0, i, 0)),
            pl.BlockSpec((NPAIR, tile_m, LANE), lambda i: (SL_GA // NPAIR, i, 0)),
            pl.BlockSpec((NPAIR, tile_m, LANE), lambda i: (SL_GB // NPAIR, i, 0)),
            pl.BlockSpec((tile_m, D_MODEL), lambda i: (i, 0)),
            pl.BlockSpec((D_MODEL, D_MODEL), lambda i: (0, 0)),
            pl.BlockSpec((1, D_MODEL), lambda i: (0, 0)),
        ],
        out_specs=pl.BlockSpec((tile_m, D_MODEL), lambda i: (i, 0)),
        out_shape=jax.ShapeDtypeStruct((m, D_MODEL), F32),
        scratch_shapes=[pltpu.VMEM((tile_m, D_MODEL), BF16)],
        compiler_params=pltpu.CompilerParams(
            dimension_semantics=("arbitrary",), vmem_limit_bytes=VMEM_LIMIT),
        name="outproj",
    )(oa3, ob3, z3, z3, x, w, g)


def _pick_tile(m, cap, mult):
    best = mult
    for t in range(mult, cap + 1, mult):
        if m % t == 0:
            best = t
    assert m % best == 0
    return best


def _pad_lanes(x):
    return jnp.pad(x, [(0, 0)] * (x.ndim - 1) + [(0, LANE - x.shape[-1])])


def _shift_rows(buf):
    n = buf.shape[0]
    rkv = buf[:, :3 * NPAIR].reshape(n, 3 * D_MODEL)
    return jnp.concatenate([rkv, buf[:, 3 * NPAIR, :LORA], buf[:, 3 * NPAIR + 1, :LORA]], axis=1)


def kernel(x_prompt, x_sample, state_rwkv, state_shift, state_gla, meta_tokens, norm_pre, w_in, rwkv_mu, rwkv_w0, rwkv_w2, rwkv_a0, rwkv_a2, rwkv_k_k, rwkv_k_a, rwkv_r_k, rwkv_lnx_w, rwkv_lnx_b, gla_gk_w2, gla_gk_b, gla_norm_w, w_out, norm_post):
    depth = w_in.shape[0]
    bp, seq, _ = x_prompt.shape
    bs, dseq, _ = x_sample.shape
    front = CHUNK - N_META
    tp = front + N_META + seq
    assert tp % CHUNK == 0 and dseq <= SAMPLE_ROWS and bs % RWKV_SAMPLE_SEQS == 0
    nchunk = tp // CHUNK
    mp = bp * tp
    ms = bs * SAMPLE_ROWS
    mtot = mp + ms
    tile_in = _pick_tile(mtot, 1184, 16)
    tile_out = _pick_tile(mtot, 296, 8)

    meta = jnp.broadcast_to(meta_tokens[None], (bp, N_META, D_MODEL))
    xp = jnp.concatenate([jnp.zeros((bp, front, D_MODEL), F32), meta, x_prompt], axis=1)
    xp = xp.reshape(bp, nchunk, CHUNK, D_MODEL).transpose(1, 0, 2, 3).reshape(mp, D_MODEL)
    xs = jnp.pad(x_sample, ((0, 0), (0, SAMPLE_ROWS - dseq), (0, 0))).reshape(ms, D_MODEL)
    x = jnp.concatenate([xp, xs], axis=0)

    w_packed = _pack_w_in(w_in)
    w_out_bf = w_out.astype(BF16)
    zero_buf = jnp.zeros((bp, N_BUF_ROWS, LANE), F32)

    sr_p = sr_s = sg_p = sg_s = None
    shift_p, shift_s = [], []
    for l in range(depth):
        mu = rwkv_mu[l]
        par = jnp.stack([mu[:D_MODEL], mu[D_MODEL:2 * D_MODEL], mu[2 * D_MODEL:3 * D_MODEL],
                         rwkv_w0[l], rwkv_a0[l], rwkv_k_k[l], rwkv_k_a[l], rwkv_r_k[l].reshape(-1),
                         rwkv_lnx_w[l], rwkv_lnx_b[l]], axis=0)
        par = jnp.pad(par, ((0, N_PAR_ROWS - par.shape[0]), (0, 0)))
        mulo = jnp.pad(jnp.stack([_pad_lanes(mu[3 * D_MODEL:3 * D_MODEL + LORA]),
                                  _pad_lanes(mu[3 * D_MODEL + LORA:])], axis=0), ((0, 6), (0, 0)))
        w2 = _split2(jnp.pad(rwkv_w2[l], ((0, LANE - LORA), (0, 0))))
        a2 = _split2(jnp.pad(rwkv_a2[l], ((0, LANE - LORA), (0, 0))))
        gw = jnp.concatenate([gla_gk_w2[l], gla_gk_b[l][None]], axis=0)
        gw = _split2(jnp.pad(gw, ((0, LANE - gw.shape[0]), (0, 0))))
        nw = gla_norm_w[l][None]
        sh = state_shift[l]
        buf_s = jnp.concatenate([sh[:, :3 * D_MODEL].reshape(bs, 3 * NPAIR, LANE),
                                 _pad_lanes(sh[:, 3 * D_MODEL:3 * D_MODEL + LORA])[:, None],
                                 _pad_lanes(sh[:, 3 * D_MODEL + LORA:])[:, None]], axis=1)
        buf_s = jnp.pad(buf_s, ((0, 0), (0, N_BUF_ROWS - buf_s.shape[1]), (0, 0)))

        z3 = _inproj(x, norm_pre[l][None], w_packed, l, tile_in)

        prompt = dict(layer=l, depth=depth, nseq=bp, nchunk=nchunk, C=CHUNK, row0=0, valid_lo=front, valid_hi=tp)
        sample = dict(layer=l, depth=depth, nseq=bs, nchunk=1, C=SAMPLE_ROWS, valid_lo=0, valid_hi=dseq)
        oa, sr_p, sh_p = _rwkv(z3, par, mulo, w2, a2, zero_buf, state=None, prev_oa=None, prev_state=sr_p,
                               nsb=RWKV_PROMPT_SEQS, **prompt)
        oa, sr_s, sh_s = _rwkv(z3, par, mulo, w2, a2, buf_s, state=state_rwkv, prev_oa=oa, prev_state=sr_s,
                               nsb=RWKV_SAMPLE_SEQS, row0=mp // (SAMPLE_ROWS * RWKV_SAMPLE_SEQS), **sample)
        ob, sg_p = _gla(z3, gw, nw, state=None, prev_ob=None, prev_state=sg_p, SB=GLA_SUB, **prompt)
        ob, sg_s = _gla(z3, gw, nw, state=state_gla, prev_ob=ob, prev_state=sg_s, SB=SAMPLE_ROWS,
                        row0=mp // SAMPLE_ROWS, **sample)

        x = _outproj(oa, ob, z3, x, w_out_bf[l], norm_post[l][None], tile_out)
        shift_p.append(_shift_rows(sh_p))
        shift_s.append(_shift_rows(sh_s))

    y_prompt = x[:mp].reshape(nchunk, bp, CHUNK, D_MODEL).transpose(1, 0, 2, 3).reshape(bp, tp, D_MODEL)
    y_prompt = y_prompt[:, front + N_META:]
    y_sample = x[mp:].reshape(bs, SAMPLE_ROWS, D_MODEL)[:, :dseq]
    return (y_prompt, y_sample, sr_p, jnp.stack(shift_p), sg_p, sr_s, jnp.stack(shift_s), sg_s)
```

```python
import functools

import jax
import jax.numpy as jnp
from jax import lax
from jax.experimental import pallas as pl
from jax.experimental.pallas import tpu as pltpu

F32 = jnp.float32
BF16 = jnp.bfloat16

D_MODEL = 2048
N_META = 16
NORM_EPS = 1e-6
LNX_EPS = 64e-5
A_HEAD = 64
A_HEADS = D_MODEL // A_HEAD
LORA = 96
B_HEADS = 4
B_HK = 256
B_HV = 512
GATE_LORA = 16
GATE_TAU = 16.0
CHUNK = 64
SAMPLE_ROWS = 8
GLA_SUB = 16
GLA_FACTOR_MAX_DECAY = 60.0
RWKV_SAMPLE_SEQS = 4
RWKV_PROMPT_SEQS = 2
RWKV_UNIT_HEADS = 4
RWKV_LANE_GROUPS = 2

LANE = 128
NPAIR = D_MODEL // LANE
SLABS_PER_TILE = 8
TILE_N = SLABS_PER_TILE * LANE
SL_R, SL_K, SL_V, SL_ZA = 0, 16, 32, 48
SL_QB, SL_KB, SL_VB, SL_ZB = 64, 72, 80, 96
SL_GA, SL_GB = 112, 128
SL_WLO, SL_ALO, SL_GK = 144, 145, 146
NSLAB = 152
NCOLP = NSLAB * LANE
N_BUF_ROWS = 56
N_PAR_ROWS = 16
VMEM_LIMIT = 56 * 1024 * 1024

NN = ((1,), (0,))
NT = ((1,), (1,))
TN = ((0,), (0,))


def _dot(a, b, dims=NN):
    return lax.dot_general(a, b, (dims, ((), ())), preferred_element_type=F32)


def _mm(a, b, dims=NN):
    return _dot(a.astype(BF16), b.astype(BF16), dims)


def _split2(x):
    hi = x.astype(BF16)
    lo = (x - hi.astype(F32)).astype(BF16)
    return hi, lo


def _mm3(a, b_hi, b_lo, dims=NN):
    ah, al = _split2(a)
    return _dot(ah, b_hi, dims) + _dot(al, b_hi, dims) + _dot(ah, b_lo, dims)


def _cumsum_rows(tri, x):
    hi, lo = _split2(x)
    return _dot(tri, hi) + _dot(tri, lo)


def _sigmoid(x):
    return 1.0 / (1.0 + jnp.exp(-x))


def _silu(x):
    return x * _sigmoid(x)


def _log_sigmoid(x):
    return jnp.minimum(x, 0.0) - jnp.log(1.0 + jnp.exp(-jnp.abs(x)))


def _iota(shape, dim):
    return lax.broadcasted_iota(jnp.int32, shape, dim)


W_RKV = 3 * D_MODEL
W_SHIFT = W_RKV + 2 * LORA
W_MID = 3 * D_MODEL
W_GK = W_SHIFT + W_MID
W_TAIL = 3 * D_MODEL


def _pack_w_in(w_in):
    depth, d, ncol = w_in.shape
    assert ncol == W_GK + GATE_LORA + W_TAIL
    wt = jnp.swapaxes(w_in, 1, 2)

    def rows(a, n, pad_to=None):
        x = wt[:, a:a + n]
        return x if pad_to is None else jnp.pad(x, ((0, 0), (0, pad_to - n), (0, 0)))

    parts = [rows(0, W_RKV), rows(W_SHIFT, W_MID), rows(W_GK + GATE_LORA, W_TAIL),
             rows(W_RKV, LORA, LANE), rows(W_RKV + LORA, LORA, LANE), rows(W_GK, GATE_LORA, LANE)]
    packed = jnp.concatenate(parts, axis=1)
    return jnp.pad(packed, ((0, 0), (0, NCOLP - packed.shape[1]), (0, 0))).astype(BF16)


def _inproj_kernel(x_ref, g_ref, w_ref, o_ref, h_ref):
    @pl.when(pl.program_id(1) == 0)
    def _():
        x = x_ref[...]
        ms = jnp.mean(x * x, axis=-1, keepdims=True)
        h_ref[...] = (x * lax.rsqrt(ms + NORM_EPS) * g_ref[...]).astype(BF16)

    acc = _dot(h_ref[...], w_ref[0], NT)
    for q in range(SLABS_PER_TILE):
        o_ref[q] = acc[:, q * LANE:(q + 1) * LANE]


def _inproj(x, g, w, layer, tile_m):
    m = x.shape[0]
    return pl.pallas_call(
        _inproj_kernel,
        grid=(m // tile_m, NCOLP // TILE_N),
        in_specs=[
            pl.BlockSpec((tile_m, D_MODEL), lambda i, j: (i, 0)),
            pl.BlockSpec((1, D_MODEL), lambda i, j: (0, 0)),
            pl.BlockSpec((1, TILE_N, D_MODEL), lambda i, j: (layer, j, 0)),
        ],
        out_specs=pl.BlockSpec((SLABS_PER_TILE, tile_m, LANE), lambda i, j: (j, i, 0)),
        out_shape=jax.ShapeDtypeStruct((NSLAB, m, LANE), F32),
        scratch_shapes=[pltpu.VMEM((tile_m, D_MODEL), BF16)],
        compiler_params=pltpu.CompilerParams(
            dimension_semantics=("arbitrary", "arbitrary"), vmem_limit_bytes=VMEM_LIMIT),
        name="inproj",
    )(x, g, w)


def _rwkv_kernel(*refs, C, nsb, nchunk, ngroup, valid_lo, valid_hi, zero_init, n_alias):
    it = iter(refs)
    zr_ref, za_ref, lo_ref, par_ref, mulo_ref, w2_ref, a2_ref, buf_ref = (next(it) for _ in range(8))
    sin_ref = None if zero_init else next(it)
    for _ in range(n_alias):
        next(it)
    oa_ref, sout_ref, shift_ref, bd_ref, prev_ref, at_ref, rt_ref, bt_ref, kt_ref, vm_ref, y_ref = it
    c = pl.program_id(1)
    H = A_HEAD
    R = nsb * C
    W = D_MODEL
    NH = RWKV_UNIT_HEADS
    NG = ngroup
    QL = NH * H
    NQ = W // QL
    units = [(s, q) for s in range(nsb) for q in range(NQ)]

    @pl.when(c == 0)
    def _init():
        prev_ref[...] = buf_ref[...]
        for s, q in units:
            if zero_init:
                bd_ref[s * NQ + q] = jnp.zeros((QL, QL), F32)
            else:
                blocks = []
                for h in range(NH):
                    pieces = [jnp.zeros((H, H), F32)] * NH
                    pieces[h] = sin_ref[0, s, NH * q + h]
                    blocks.append(jnp.concatenate(pieces, axis=1))
                bd_ref[s * NQ + q] = jnp.concatenate(blocks, axis=0)

    row = _iota((R, 1), 0)
    rin = row & (C - 1)
    pos = c * C + rin
    valid = (pos >= valid_lo) & (pos < valid_hi)

    def wide(ref, n0, n):
        return jnp.concatenate([ref[n0 + j] for j in range(n)], axis=1)

    def rowcat(ref, s, n0, n):
        return jnp.concatenate([ref[s, n0 + j:n0 + j + 1, :] for j in range(n)], axis=1)

    SUB = 8
    row8 = _iota((SUB, 1), 0)

    def prev_rows(x, n0, n):
        rolled = pltpu.roll(x, 1, axis=0)
        patches = {}
        for s in range(nsb):
            t0 = s * C
            patches[t0] = jnp.where(row8 == 0, rowcat(prev_ref, s, n0, n), rolled[t0:t0 + SUB])
            if valid_lo > 0:
                t1 = s * C + (valid_lo % C) // SUB * SUB
                hit = (c == valid_lo // C) & (row8 == valid_lo % SUB)
                patches[t1] = jnp.where(hit, rowcat(buf_ref, s, n0, n), patches.get(t1, rolled[t1:t1 + SUB]))
        pieces, t = [], 0
        for t0 in sorted(patches):
            if t0 > t:
                pieces.append(rolled[t:t0])
            pieces.append(patches[t0])
            t = t0 + SUB
        if t < R:
            pieces.append(rolled[t:])
        return jnp.concatenate(pieces, axis=0)

    par = par_ref[...]
    wl_raw = lo_ref[0]
    al_raw = lo_ref[1]
    tw = jnp.tanh(wl_raw + (prev_rows(wl_raw, 3 * NPAIR, 1) - wl_raw) * mulo_ref[0:1, :])
    xa = al_raw + (prev_rows(al_raw, 3 * NPAIR + 1, 1) - al_raw) * mulo_ref[1:2, :]

    seg01 = jnp.where((_iota((QL, QL), 0) & -H) == (_iota((QL, QL), 1) & -H), 1.0, 0.0).astype(BF16)
    ti = _iota((R, R), 0)
    tj = _iota((R, R), 1)
    tri = jnp.where((tj <= ti) & ((ti & -C) == (tj & -C)), 1.0, 0.0).astype(BF16)
    SW = NH * C
    ri = _iota((C, SW), 0)
    ci = _iota((C, SW), 1) & (C - 1)
    strict = ci < ri
    incl = ci <= ri
    eye = jnp.where(ri == ci, 1.0, 0.0)
    lane_i = _iota((1, SW), 1)
    lane_c = _iota((1, QL), 1)
    head_i = [(lane_i >= C * h) & (lane_i < C * (h + 1)) for h in range(NH)]
    head_c = [(lane_c >= H * h) & (lane_c < H * (h + 1)) for h in range(NH)]
    bd_mask = (_iota((QL, QL), 0) & -H) == (_iota((QL, QL), 1) & -H)
    n_dbl = C.bit_length() - 2

    GW = W // NG
    GP = GW // LANE
    GQ = GW // QL
    st = [dict() for _ in range(NG)]

    def glanes(x, g):
        return x[:, g * GW:(g + 1) * GW]

    def seg(x):
        pm = jnp.concatenate([x[:, q * QL:(q + 1) * QL] for q in range(GQ)], axis=0)
        sm = _dot(pm.astype(BF16), seg01)
        return jnp.concatenate([sm[q * R:(q + 1) * R] for q in range(GQ)], axis=1)

    def wide1(g):
        d = st[g]
        pg = glanes(par, g)
        d["par"] = pg
        rz = wide(zr_ref, g * GP, GP)
        kz = wide(zr_ref, NPAIR + g * GP, GP)
        vz = wide(zr_ref, 2 * NPAIR + g * GP, GP)
        d["r"] = rz + (prev_rows(rz, g * GP, GP) - rz) * pg[0:1]
        d["k"] = kz + (prev_rows(kz, NPAIR + g * GP, GP) - kz) * pg[1:2]
        d["v"] = vz + (prev_rows(vz, 2 * NPAIR + g * GP, GP) - vz) * pg[2:3]
        d["wl"] = pg[3:4] + _dot(tw.astype(BF16), w2_ref[:, g * GW:(g + 1) * GW])
        d["alp"] = pg[4:5] + _dot(xa.astype(BF16), a2_ref[:, g * GW:(g + 1) * GW])

    def wide2(g):
        d = st[g]
        pg = d["par"]
        ld = -jnp.exp(_log_sigmoid(d["wl"]) - 0.5)
        d["ld"] = jnp.where(valid, ld, 0.0)
        d["al"] = _sigmoid(d["alp"])
        d["kkv"] = d["k"] * pg[5:6]
        d["k2"] = d["k"] * (1.0 + (d["al"] - 1.0) * pg[6:7])
        d["ssq"] = seg(d["kkv"] * d["kkv"])
        d["L"] = _cumsum_rows(tri, d["ld"])
        d["bsum"] = seg(d["r"] * d["k2"] * pg[7:8])

    def wide3(g):
        d = st[g]
        kkn = d["kkv"] / jnp.maximum(jnp.sqrt(d["ssq"]), 1e-12)
        d["bonus"] = d["bsum"] * d["v"]
        L, ld = d["L"], d["ld"]
        if C >= 32:
            mid = [L[s * C + C // 2 - 1:s * C + C // 2] for s in range(nsb)]
            lref = mid[0]
            for s in range(1, nsb):
                lref = jnp.where(row >= s * C, mid[s], lref)
        else:
            mid = [jnp.zeros((1, GW), F32)] * nsb
            lref = mid[0]
        e_fwd = jnp.exp(L - lref)
        e_bwd = jnp.exp(lref - L)
        gs = slice(g * GW, (g + 1) * GW)
        at_ref[:, gs] = -kkn * jnp.exp(L - ld - lref)
        rt_ref[:, gs] = d["r"] * e_fwd
        bt_ref[:, gs] = jnp.where(valid, kkn * d["al"], 0.0) * e_bwd
        kt_ref[:, gs] = jnp.where(valid, d["k2"], 0.0) * e_bwd
        vm_ref[:, gs] = jnp.where(valid, d["v"], 0.0)
        d["e_mid"] = [jnp.exp(m) for m in mid]
        d["e_last"] = [e_fwd[s * C + C - 1:s * C + C] for s in range(nsb)]
        d["p_last"] = [jnp.exp(L[s * C + C - 1:s * C + C]) for s in range(nsb)]
        for key in ("wl", "alp", "ld", "al", "kkv", "k2", "ssq", "L", "bsum", "k", "r", "v"):
            del d[key]

    def sl(ref, s, q):
        return ref[s * C:(s + 1) * C, q * QL:(q + 1) * QL]

    def blockdiag(x, masks):
        return jnp.concatenate([jnp.where(m, x, jnp.zeros_like(x)) for m in masks], axis=0)

    def units_of(g):
        return [(s, g * GQ + j) for s in range(nsb) for j in range(GQ)]

    unit_keys = ("lhs_ar", "rhs_bk", "a_ab", "a_ak", "a_rb", "a_rk", "inv", "apow", "x2", "av2", "vm", "uu")

    def unit_g(g):
        d = st[g]
        for name in unit_keys:
            d[name] = {}
        for u in units_of(g):
            s, q = u
            d["lhs_ar"][u] = jnp.concatenate([sl(at_ref, s, q), sl(rt_ref, s, q)], axis=0).astype(BF16)
            bt = sl(bt_ref, s, q).astype(BF16)
            kt = sl(kt_ref, s, q).astype(BF16)
            d["rhs_bk"][u] = jnp.concatenate([bt, kt], axis=0)
            gm = _dot(d["lhs_ar"][u],
                      jnp.concatenate([blockdiag(bt, head_c), blockdiag(kt, head_c)], axis=0), NT)
            d["a_ab"][u] = jnp.where(strict, gm[:C, :SW], 0.0)
            d["a_ak"][u] = jnp.where(strict, gm[:C, SW:], 0.0)
            d["a_rb"][u] = jnp.where(incl, gm[C:, :SW], 0.0)
            d["a_rk"][u] = jnp.where(incl, gm[C:, SW:], 0.0)
            d["inv"][u] = eye + d["a_ab"][u]

    def unit_p1(g):
        d = st[g]
        for u in units_of(g):
            pw = d["a_ab"][u].astype(BF16)
            d["apow"][u] = _dot(pw, blockdiag(pw, head_i))

    def unit_level(g, lvl):
        d = st[g]
        for u in units_of(g):
            pw = d["apow"][u].astype(BF16)
            rhs = blockdiag(pw, head_i)
            if lvl < n_dbl - 1:
                both = _dot(jnp.concatenate([pw, d["inv"][u].astype(BF16)], axis=0), rhs)
                d["apow"][u] = both[:C]
                d["inv"][u] = d["inv"][u] + both[C:]
            else:
                d["inv"][u] = d["inv"][u] + _dot(d["inv"][u].astype(BF16), rhs)

    def unit_x2(g):
        d = st[g]
        for u in units_of(g):
            s, q = u
            lo = (q - g * GQ) * QL
            bdm = bd_ref[s * NQ + q] * d["e_mid"][s][:, lo:lo + QL]
            d["x2"][u] = _dot(d["lhs_ar"][u], bdm.astype(BF16), NT)

    def unit_av2(g):
        d = st[g]
        for u in units_of(g):
            s, q = u
            d["vm"][u] = sl(vm_ref, s, q).astype(BF16)
            d["av2"][u] = _dot(jnp.concatenate([d["a_ak"][u], d["a_rk"][u]], axis=0).astype(BF16),
                               blockdiag(d["vm"][u], head_c))

    def unit_u(g):
        d = st[g]
        for u in units_of(g):
            w1a = (d["x2"][u][:C] + d["av2"][u][:C]).astype(BF16)
            d["uu"][u] = _dot(d["inv"][u].astype(BF16), blockdiag(w1a, head_c)).astype(BF16)

    def unit_y(g):
        d = st[g]
        for u in units_of(g):
            s, q = u
            y_ref[s * C:(s + 1) * C, q * QL:(q + 1) * QL] = (
                d["x2"][u][C:] + d["av2"][u][C:] + _dot(d["a_rb"][u].astype(BF16), blockdiag(d["uu"][u], head_c)))

    def unit_upd(g):
        d = st[g]
        for u in units_of(g):
            s, q = u
            lo = (q - g * GQ) * QL
            upd = _dot(jnp.concatenate([d["uu"][u], d["vm"][u]], axis=0), d["rhs_bk"][u], TN)
            bd = bd_ref[s * NQ + q] * d["p_last"][s][:, lo:lo + QL] + upd * d["e_last"][s][:, lo:lo + QL]
            bd_ref[s * NQ + q] = jnp.where(bd_mask, bd, 0.0)
        for name in unit_keys:
            del d[name]

    def post1(g):
        d = st[g]
        d["y"] = y_ref[:, g * GW:(g + 1) * GW]
        d["mean"] = seg(d["y"]) * (1.0 / H)

    def post2(g):
        d = st[g]
        d["yc"] = d["y"] - d["mean"]
        d["var"] = seg(d["yc"] * d["yc"]) * (1.0 / H)

    def post3(g):
        d = st[g]
        pg = d["par"]
        out = ((d["yc"] * lax.rsqrt(d["var"] + LNX_EPS) * pg[8:9] + pg[9:10] + d["bonus"])
               * _silu(wide(za_ref, g * GP, GP)))
        for p in range(GP):
            oa_ref[g * GP + p] = out[:, p * LANE:(p + 1) * LANE]

    unit_stages = ([unit_g, unit_p1] + [functools.partial(unit_level, lvl=i) for i in range(n_dbl)]
                   + [unit_x2, unit_av2, unit_u, unit_y, unit_upd])
    for f in (wide1, wide2, wide3):
        f(0)
    for g in range(NG):
        side = []
        if g + 1 < NG:
            side += [(wide1, g + 1), (wide2, g + 1), (wide3, g + 1)]
        if g > 0:
            side += [(post1, g - 1), (post2, g - 1), (post3, g - 1)]
        every = max(1, len(unit_stages) // max(1, len(side)))
        for j, stage in enumerate(unit_stages):
            stage(g)
            if side and (j + 1) % every == 0:
                f, gg = side.pop(0)
                f(gg)
        for f, gg in side:
            f(gg)
    for f in (post1, post2, post3):
        f(NG - 1)

    if nchunk > 1:
        for s in range(nsb):
            last = s * C + C - 1
            for j in range(3 * NPAIR):
                prev_ref[s, j:j + 1, :] = zr_ref[j, last:last + 1, :]
            prev_ref[s, 3 * NPAIR:3 * NPAIR + 1, :] = wl_raw[last:last + 1]
            prev_ref[s, 3 * NPAIR + 1:3 * NPAIR + 2, :] = al_raw[last:last + 1]

    @pl.when(c == nchunk - 1)
    def _fin():
        for s, q in units:
            bd = bd_ref[s * NQ + q]
            for h in range(NH):
                sout_ref[0, s, NH * q + h] = bd[h * H:(h + 1) * H, h * H:(h + 1) * H]
        shift_ref[...] = jnp.zeros(shift_ref.shape, F32)
        for s in range(nsb):
            last = s * C + (valid_hi - 1) - (nchunk - 1) * C
            for j in range(3 * NPAIR):
                shift_ref[s, j:j + 1, :] = zr_ref[j, last:last + 1, :]
            shift_ref[s, 3 * NPAIR:3 * NPAIR + 1, :] = wl_raw[last:last + 1]
            shift_ref[s, 3 * NPAIR + 1:3 * NPAIR + 2, :] = al_raw[last:last + 1]


def _rwkv(z3, par, mulo, w2, a2, buf, *, state, prev_oa, prev_state, layer, depth, nseq, nsb, nchunk, ngroup, C,
          row0, valid_lo, valid_hi):
    R = nsb * C
    rb = lambda n, c: row0 + c * (nseq // nsb) + n
    state_block = (1, nsb, A_HEADS, A_HEAD, A_HEAD)
    state_map = lambda n, c: (layer, n, 0, 0, 0)
    in_specs = [
        pl.BlockSpec((3 * NPAIR, R, LANE), lambda n, c: (0, rb(n, c), 0)),
        pl.BlockSpec((NPAIR, R, LANE), lambda n, c: (SL_ZA // NPAIR, rb(n, c), 0)),
        pl.BlockSpec((2, R, LANE), lambda n, c: (SL_WLO // 2, rb(n, c), 0)),
        pl.BlockSpec((N_PAR_ROWS, D_MODEL), lambda n, c: (0, 0)),
        pl.BlockSpec((8, LANE), lambda n, c: (0, 0)),
        pl.BlockSpec((LANE, D_MODEL), lambda n, c: (0, 0)),
        pl.BlockSpec((LANE, D_MODEL), lambda n, c: (0, 0)),
        pl.BlockSpec((nsb, N_BUF_ROWS, LANE), lambda n, c: (n, 0, 0)),
    ]
    args = [z3, z3, z3, par, mulo, w2, a2, buf]
    if state is not None:
        in_specs.append(pl.BlockSpec(state_block, state_map))
        args.append(state)
    aliases = {}
    for out_idx, prev in enumerate((prev_oa, prev_state)):
        if prev is not None:
            aliases[len(args)] = out_idx
            in_specs.append(pl.BlockSpec(memory_space=pl.ANY))
            args.append(prev)
    kern = functools.partial(_rwkv_kernel, C=C, nsb=nsb, nchunk=nchunk, ngroup=ngroup, valid_lo=valid_lo,
                             valid_hi=valid_hi, zero_init=state is None, n_alias=len(aliases))
    wide_scratch = pltpu.VMEM((R, D_MODEL), F32)
    unit_lanes = RWKV_UNIT_HEADS * A_HEAD
    return pl.pallas_call(
        kern,
        grid=(nseq // nsb, nchunk),
        in_specs=in_specs,
        out_specs=[
            pl.BlockSpec((NPAIR, R, LANE), lambda n, c: (0, rb(n, c), 0)),
            pl.BlockSpec(state_block, state_map),
            pl.BlockSpec((nsb, N_BUF_ROWS, LANE), lambda n, c: (n, 0, 0)),
        ],
        out_shape=[
            jax.ShapeDtypeStruct((NPAIR, z3.shape[1], LANE), F32),
            jax.ShapeDtypeStruct((depth, nseq, A_HEADS, A_HEAD, A_HEAD), F32),
            jax.ShapeDtypeStruct((nseq, N_BUF_ROWS, LANE), F32),
        ],
        input_output_aliases=aliases,
        scratch_shapes=[
            pltpu.VMEM((nsb * D_MODEL // unit_lanes, unit_lanes, unit_lanes), F32),
            pltpu.VMEM((nsb, N_BUF_ROWS, LANE), F32),
        ] + [wide_scratch] * 6,
        compiler_params=pltpu.CompilerParams(
            dimension_semantics=("arbitrary", "arbitrary"), vmem_limit_bytes=VMEM_LIMIT),
        name="rwkv_c%d" % C,
    )(*args)


def _gla_kernel(*refs, C, SB, nchunk, valid_lo, valid_hi, zero_init, n_alias):
    it = iter(refs)
    q_ref, k_ref, v_ref, zb_ref, gk_ref, gwh_ref, gwl_ref, nw_ref = (next(it) for _ in range(8))
    sin_ref = None if zero_init else next(it)
    for _ in range(n_alias):
        next(it)
    ob_ref, sout_ref, s_ref, att_ref = it
    c = pl.program_id(1)
    heads = range(B_HEADS)
    QS = B_HK // LANE
    VS = B_HV // LANE

    @pl.when(c == 0)
    def _init():
        for h in heads:
            s_ref[h] = jnp.zeros((B_HK, B_HV), F32) if zero_init else sin_ref[0, 0, h]

    pos = c * C + _iota((C, 1), 0)
    valid = (pos >= valid_lo) & (pos < valid_hi)

    glo1 = jnp.where(_iota((C, LANE), 1) == GATE_LORA, 1.0, gk_ref[0])
    g = jnp.where(valid, _log_sigmoid(_mm3(glo1, gwh_ref[...], gwl_ref[...])) / GATE_TAU, 0.0)
    ti = _iota((C, C), 0)
    tj = _iota((C, C), 1)
    tri = jnp.where(tj <= ti, 1.0, 0.0).astype(BF16)
    ones = jnp.ones((C, LANE), BF16)
    parts = _split2(g)
    b = _dot(tri, parts[0]) + _dot(tri, parts[1])
    b_col = _dot(parts[0], ones, TN) + _dot(parts[1], ones, TN)
    eb = jnp.exp(b)
    ed = jnp.exp(b[C - 1:C] - b)

    def hk(x, h):
        return x[:, h * B_HK:(h + 1) * B_HK]

    q = [jnp.concatenate([q_ref[QS * h + j] for j in range(QS)], axis=1) * (B_HK ** -0.5) for h in heads]
    k = [jnp.where(valid, jnp.concatenate([k_ref[QS * h + j] for j in range(QS)], axis=1), 0.0) for h in heads]
    v = [jnp.where(valid, jnp.concatenate([v_ref[VS * h + j] for j in range(VS)], axis=1), 0.0).astype(BF16)
         for h in heads]

    o = [_mm(q[h] * hk(eb, h), s_ref[h]) for h in heads]

    att_ref[...] = jnp.zeros(att_ref.shape, F32)
    nb = C // SB
    for i in range(1, nb):
        lo, hi = SB * i, SB * (i + 1)
        for h in heads:
            bh = hk(b, h)
            beta = bh[lo - 1:lo]
            qi = q[h][lo:hi] * jnp.exp(bh[lo:hi] - beta)
            kj = k[h][:lo] * jnp.exp(beta - bh[:lo])
            att_ref[h, :lo, lo:hi] = _mm(kj, qi, NT)

    srow = _iota((SB, 1), 0)

    def _diag_matmul():
        causal = srow <= _iota((SB, SB), 1)
        for i in range(nb):
            lo = SB * i
            for h in heads:
                bb = hk(b, h)[lo:lo + SB]
                qh = q[h][lo:lo + SB] * jnp.exp(bb - bb[0:1])
                kh = k[h][lo:lo + SB] * jnp.exp(bb[0:1] - bb)
                att_ref[h, lo:lo + SB, lo:lo + SB] = jnp.where(causal, _mm(kh, qh, NT), 0.0)

    def _diag_exact():
        for i in range(nb):
            lo = SB * i
            for h in heads:
                bb = hk(b, h)[lo:lo + SB]
                kb = k[h][lo:lo + SB]
                for t in range(SB):
                    e = jnp.exp(jnp.where(srow <= t, bb[t:t + 1] - bb, -jnp.inf))
                    att_ref[h, lo:lo + SB, lo + t:lo + t + 1] = jnp.sum(
                        q[h][lo + t:lo + t + 1] * kb * e, axis=-1, keepdims=True)

    if nb == 1:
        _diag_exact()
    else:
        span = jnp.max(jnp.concatenate(
            [b[SB * i:SB * i + 1] - b[SB * i + SB - 1:SB * i + SB] for i in range(nb)], axis=0))
        pl.when(span < GLA_FACTOR_MAX_DECAY)(_diag_matmul)
        pl.when(span >= GLA_FACTOR_MAX_DECAY)(_diag_exact)

    for h in heads:
        o[h] = o[h] + _dot(att_ref[h].astype(BF16), v[h], TN)

    for h in heads:
        ms = jnp.mean(o[h] * o[h], axis=-1, keepdims=True)
        on = o[h] * lax.rsqrt(ms + NORM_EPS) * nw_ref[...]
        for j in range(VS):
            ob_ref[VS * h + j] = on[:, j * LANE:(j + 1) * LANE] * _silu(zb_ref[VS * h + j])

    for h in heads:
        decay = jnp.exp(b_col[h * B_HK:(h + 1) * B_HK])
        decay = jnp.concatenate([decay] * VS, axis=1)
        s_ref[h] = s_ref[h] * decay + _dot((k[h] * hk(ed, h)).astype(BF16), v[h], TN)

    @pl.when(c == nchunk - 1)
    def _fin():
        for h in heads:
            sout_ref[0, 0, h] = s_ref[h]


def _gla(z3, gw, nw, *, state, prev_ob, prev_state, layer, depth, nseq, nchunk, C, SB, row0, valid_lo, valid_hi):
    rb = lambda n, c: row0 + c * nseq + n
    state_block = (1, 1, B_HEADS, B_HK, B_HV)
    state_map = lambda n, c: (layer, n, 0, 0, 0)
    qs = B_HEADS * B_HK // LANE
    vs = B_HEADS * B_HV // LANE
    in_specs = [
        pl.BlockSpec((qs, C, LANE), lambda n, c: (SL_QB // qs, rb(n, c), 0)),
        pl.BlockSpec((qs, C, LANE), lambda n, c: (SL_KB // qs, rb(n, c), 0)),
        pl.BlockSpec((vs, C, LANE), lambda n, c: (SL_VB // vs, rb(n, c), 0)),
        pl.BlockSpec((vs, C, LANE), lambda n, c: (SL_ZB // vs, rb(n, c), 0)),
        pl.BlockSpec((1, C, LANE), lambda n, c: (SL_GK, rb(n, c), 0)),
        pl.BlockSpec((LANE, B_HEADS * B_HK), lambda n, c: (0, 0)),
        pl.BlockSpec((LANE, B_HEADS * B_HK), lambda n, c: (0, 0)),
        pl.BlockSpec((1, B_HV), lambda n, c: (0, 0)),
    ]
    args = [z3, z3, z3, z3, z3, *gw, nw]
    if state is not None:
        in_specs.append(pl.BlockSpec(state_block, state_map))
        args.append(state)
    aliases = {}
    for out_idx, prev in enumerate((prev_ob, prev_state)):
        if prev is not None:
            aliases[len(args)] = out_idx
            in_specs.append(pl.BlockSpec(memory_space=pl.ANY))
            args.append(prev)
    kern = functools.partial(_gla_kernel, C=C, SB=SB, nchunk=nchunk, valid_lo=valid_lo, valid_hi=valid_hi,
                             zero_init=state is None, n_alias=len(aliases))
    return pl.pallas_call(
        kern,
        grid=(nseq, nchunk),
        in_specs=in_specs,
        out_specs=[
            pl.BlockSpec((vs, C, LANE), lambda n, c: (0, rb(n, c), 0)),
            pl.BlockSpec(state_block, state_map),
        ],
        out_shape=[
            jax.ShapeDtypeStruct((NPAIR, z3.shape[1], LANE), F32),
            jax.ShapeDtypeStruct((depth, nseq, B_HEADS, B_HK, B_HV), F32),
        ],
        input_output_aliases=aliases,
        scratch_shapes=[
            pltpu.VMEM((B_HEADS, B_HK, B_HV), F32),
            pltpu.VMEM((B_HEADS, C, C), F32),
        ],
        compiler_params=pltpu.CompilerParams(
            dimension_semantics=("arbitrary", "arbitrary"), vmem_limit_bytes=VMEM_LIMIT),
        name="gla_c%d" % C,
    )(*args)


def _outproj_kernel(oa_ref, ob_ref, ga_ref, gb_ref, x_ref, w_ref, g_ref, o_ref, m_ref):
    for q in range(NPAIR):
        m = _sigmoid(ga_ref[q]) * oa_ref[q] + _sigmoid(gb_ref[q]) * ob_ref[q]
        m_ref[:, q * LANE:(q + 1) * LANE] = m.astype(BF16)
    y = jnp.dot(m_ref[...], w_ref[...], preferred_element_type=F32)
    ms = jnp.mean(y * y, axis=-1, keepdims=True)
    o_ref[...] = x_ref[...] + y * lax.rsqrt(ms + NORM_EPS) * g_ref[...]


def _outproj(oa3, ob3, z3, x, w, g, tile_m):
    m = x.shape[0]
    return pl.pallas_call(
        _outproj_kernel,
        grid=(m // tile_m,),
        in_specs=[
            pl.BlockSpec((NPAIR, tile_m, LANE), lambda i: (0, i, 0)),
            pl.BlockSpec((NPAIR, tile_m, LANE), lambda i: (0, i, 0)),
            pl.BlockSpec((NPAIR, tile_m, LANE), lambda i: (SL_GA // NPAIR, i, 0)),
            pl.BlockSpec((NPAIR, tile_m, LANE), lambda i: (SL_GB // NPAIR, i, 0)),
            pl.BlockSpec((tile_m, D_MODEL), lambda i: (i, 0)),
            pl.BlockSpec((D_MODEL, D_MODEL), lambda i: (0, 0)),
            pl.BlockSpec((1, D_MODEL), lambda i: (0, 0)),
        ],
        out_specs=pl.BlockSpec((tile_m, D_MODEL), lambda i: (i, 0)),
        out_shape=jax.ShapeDtypeStruct((m, D_MODEL), F32),
        scratch_shapes=[pltpu.VMEM((tile_m, D_MODEL), BF16)],
        compiler_params=pltpu.CompilerParams(
            dimension_semantics=("arbitrary",), vmem_limit_bytes=VMEM_LIMIT),
        name="outproj",
    )(oa3, ob3, z3, z3, x, w, g)


def _pick_tile(m, cap, mult):
    best = mult
    for t in range(mult, cap + 1, mult):
        if m % t == 0:
            best = t
    assert m % best == 0
    return best


def _pad_lanes(x):
    return jnp.pad(x, [(0, 0)] * (x.ndim - 1) + [(0, LANE - x.shape[-1])])


def _shift_rows(buf):
    n = buf.shape[0]
    rkv = buf[:, :3 * NPAIR].reshape(n, 3 * D_MODEL)
    return jnp.concatenate([rkv, buf[:, 3 * NPAIR, :LORA], buf[:, 3 * NPAIR + 1, :LORA]], axis=1)


def kernel(x_prompt, x_sample, state_rwkv, state_shift, state_gla, meta_tokens, norm_pre, w_in, rwkv_mu, rwkv_w0, rwkv_w2, rwkv_a0, rwkv_a2, rwkv_k_k, rwkv_k_a, rwkv_r_k, rwkv_lnx_w, rwkv_lnx_b, gla_gk_w2, gla_gk_b, gla_norm_w, w_out, norm_post):
    depth = w_in.shape[0]
    bp, seq, _ = x_prompt.shape
    bs, dseq, _ = x_sample.shape
    front = CHUNK - N_META
    tp = front + N_META + seq
    assert tp % CHUNK == 0 and dseq <= SAMPLE_ROWS and bs % RWKV_SAMPLE_SEQS == 0
    nchunk = tp // CHUNK
    mp = bp * tp
    ms = bs * SAMPLE_ROWS
    mtot = mp + ms
    tile_in = _pick_tile(mtot, 1184, 16)
    tile_out = _pick_tile(mtot, 296, 8)

    meta = jnp.broadcast_to(meta_tokens[None], (bp, N_META, D_MODEL))
    xp = jnp.concatenate([jnp.zeros((bp, front, D_MODEL), F32), meta, x_prompt], axis=1)
    xp = xp.reshape(bp, nchunk, CHUNK, D_MODEL).transpose(1, 0, 2, 3).reshape(mp, D_MODEL)
    xs = jnp.pad(x_sample, ((0, 0), (0, SAMPLE_ROWS - dseq), (0, 0))).reshape(ms, D_MODEL)
    x = jnp.concatenate([xp, xs], axis=0)

    w_packed = _pack_w_in(w_in)
    w_out_bf = w_out.astype(BF16)
    zero_buf = jnp.zeros((bp, N_BUF_ROWS, LANE), F32)

    sr_p = sr_s = sg_p = sg_s = None
    shift_p, shift_s = [], []
    for l in range(depth):
        mu = rwkv_mu[l]
        par = jnp.stack([mu[:D_MODEL], mu[D_MODEL:2 * D_MODEL], mu[2 * D_MODEL:3 * D_MODEL],
                         rwkv_w0[l], rwkv_a0[l], rwkv_k_k[l], rwkv_k_a[l], rwkv_r_k[l].reshape(-1),
                         rwkv_lnx_w[l], rwkv_lnx_b[l]], axis=0)
        par = jnp.pad(par, ((0, N_PAR_ROWS - par.shape[0]), (0, 0)))
        mulo = jnp.pad(jnp.stack([_pad_lanes(mu[3 * D_MODEL:3 * D_MODEL + LORA]),
                                  _pad_lanes(mu[3 * D_MODEL + LORA:])], axis=0), ((0, 6), (0, 0)))
        w2 = jnp.pad(rwkv_w2[l], ((0, LANE - LORA), (0, 0))).astype(BF16)
        a2 = jnp.pad(rwkv_a2[l], ((0, LANE - LORA), (0, 0))).astype(BF16)
        gw = jnp.concatenate([gla_gk_w2[l], gla_gk_b[l][None]], axis=0)
        gw = _split2(jnp.pad(gw, ((0, LANE - gw.shape[0]), (0, 0))))
        nw = gla_norm_w[l][None]
        sh = state_shift[l]
        buf_s = jnp.concatenate([sh[:, :3 * D_MODEL].reshape(bs, 3 * NPAIR, LANE),
                                 _pad_lanes(sh[:, 3 * D_MODEL:3 * D_MODEL + LORA])[:, None],
                                 _pad_lanes(sh[:, 3 * D_MODEL + LORA:])[:, None]], axis=1)
        buf_s = jnp.pad(buf_s, ((0, 0), (0, N_BUF_ROWS - buf_s.shape[1]), (0, 0)))

        z3 = _inproj(x, norm_pre[l][None], w_packed, l, tile_in)

        prompt = dict(layer=l, depth=depth, nseq=bp, nchunk=nchunk, C=CHUNK, row0=0, valid_lo=front, valid_hi=tp)
        sample = dict(layer=l, depth=depth, nseq=bs, nchunk=1, C=SAMPLE_ROWS, valid_lo=0, valid_hi=dseq)
        oa, sr_p, sh_p = _rwkv(z3, par, mulo, w2, a2, zero_buf, state=None, prev_oa=None, prev_state=sr_p,
                               nsb=RWKV_PROMPT_SEQS, ngroup=RWKV_LANE_GROUPS, **prompt)
        oa, sr_s, sh_s = _rwkv(z3, par, mulo, w2, a2, buf_s, state=state_rwkv, prev_oa=oa, prev_state=sr_s,
                               nsb=RWKV_SAMPLE_SEQS, ngroup=1, row0=mp // (SAMPLE_ROWS * RWKV_SAMPLE_SEQS),
                               **sample)
        ob, sg_p = _gla(z3, gw, nw, state=None, prev_ob=None, prev_state=sg_p, SB=GLA_SUB, **prompt)
        ob, sg_s = _gla(z3, gw, nw, state=state_gla, prev_ob=ob, prev_state=sg_s, SB=SAMPLE_ROWS,
                        row0=mp // SAMPLE_ROWS, **sample)

        x = _outproj(oa, ob, z3, x, w_out_bf[l], norm_post[l][None], tile_out)
        shift_p.append(_shift_rows(sh_p))
        shift_s.append(_shift_rows(sh_s))

    y_prompt = x[:mp].reshape(nchunk, bp, CHUNK, D_MODEL).transpose(1, 0, 2, 3).reshape(bp, tp, D_MODEL)
    y_prompt = y_prompt[:, front + N_META:]
    y_sample = x[mp:].reshape(bs, SAMPLE_ROWS, D_MODEL)[:, :dseq]
    return (y_prompt, y_sample, sr_p, jnp.stack(shift_p), sg_p, sr_s, jnp.stack(shift_s), sg_s)
```

```python
import functools
import math

import jax
import jax.numpy as jnp
from jax import lax
from jax.experimental import pallas as pl
from jax.experimental.pallas import tpu as pltpu

F32 = jnp.float32
BF16 = jnp.bfloat16

D_MODEL = 2048
N_META = 16
NORM_EPS = 1e-6
LNX_EPS = 64e-5
A_HEAD = 64
A_HEADS = D_MODEL // A_HEAD
LORA = 96
B_HEADS = 4
B_HK = 256
B_HV = 512
GATE_LORA = 16
GATE_TAU = 16.0
CHUNK = 64
SAMPLE_ROWS = 8
GLA_SUB = 16
GLA_FACTOR_MAX_DECAY = 60.0
RWKV_SAMPLE_SEQS = 4
RWKV_PROMPT_SEQS = 2
RWKV_UNIT_HEADS = 4
RWKV_LANE_GROUPS = 2

LANE = 128
NPAIR = D_MODEL // LANE
SLABS_PER_TILE = 8
TILE_N = SLABS_PER_TILE * LANE
SL_R, SL_K, SL_V, SL_ZA = 0, 16, 32, 48
SL_QB, SL_KB, SL_VB, SL_ZB = 64, 72, 80, 96
SL_GA, SL_GB = 112, 128
SL_WLO, SL_ALO, SL_GK = 144, 145, 146
NSLAB = 152
NCOLP = NSLAB * LANE
N_BUF_ROWS = 56
N_PAR_ROWS = 16
VMEM_LIMIT = 56 * 1024 * 1024

NN = ((1,), (0,))
NT = ((1,), (1,))
TN = ((0,), (0,))


def _dot(a, b, dims=NN):
    return lax.dot_general(a, b, (dims, ((), ())), preferred_element_type=F32)


def _mm(a, b, dims=NN):
    return _dot(a.astype(BF16), b.astype(BF16), dims)


def _split2(x):
    hi = x.astype(BF16)
    lo = (x - hi.astype(F32)).astype(BF16)
    return hi, lo


def _mm3(a, b_hi, b_lo, dims=NN):
    ah, al = _split2(a)
    return _dot(ah, b_hi, dims) + _dot(al, b_hi, dims) + _dot(ah, b_lo, dims)


def _cumsum_rows(tri, x):
    hi, lo = _split2(x)
    return _dot(tri, hi) + _dot(tri, lo)


def _sigmoid(x):
    return 1.0 / (1.0 + jnp.exp(-x))


def _silu(x):
    return x * _sigmoid(x)


def _log_sigmoid(x):
    return jnp.minimum(x, 0.0) - jnp.log(1.0 + jnp.exp(-jnp.abs(x)))


def _iota(shape, dim):
    return lax.broadcasted_iota(jnp.int32, shape, dim)


W_RKV = 3 * D_MODEL
W_SHIFT = W_RKV + 2 * LORA
W_MID = 3 * D_MODEL
W_GK = W_SHIFT + W_MID
W_TAIL = 3 * D_MODEL


PACK_TILE = 512
W_MAIN = W_RKV + W_MID + W_TAIL
N_SMALL = NCOLP - W_MAIN


def _pack_kernel(w_ref, o_ref):
    o_ref[0] = w_ref[0].astype(BF16)


def _pack_w_in(w_in):
    depth, d, ncol = w_in.shape
    assert ncol == W_GK + GATE_LORA + W_TAIL and d == D_MODEL and N_SMALL == TILE_N
    wt = jnp.swapaxes(w_in, 1, 2)
    small = jnp.concatenate([jnp.pad(wt[:, a:a + n], ((0, 0), (0, LANE - n), (0, 0)))
                             for a, n in ((W_RKV, LORA), (W_RKV + LORA, LORA), (W_GK, GATE_LORA))], axis=1)
    small = jnp.pad(small, ((0, 0), (0, N_SMALL - small.shape[1]), (0, 0))).astype(BF16)
    g1 = W_RKV // PACK_TILE
    g2 = (W_RKV + W_MID) // PACK_TILE
    skip1 = W_SHIFT - W_RKV
    skip2 = W_GK + GATE_LORA - (W_RKV + W_MID) - skip1

    def src_row(i):
        row = i * PACK_TILE + (i >= g1).astype(jnp.int32) * skip1 + (i >= g2).astype(jnp.int32) * skip2
        return pl.multiple_of(row, math.gcd(PACK_TILE, skip1, skip2))

    main = pl.pallas_call(
        _pack_kernel,
        grid=(depth, W_MAIN // PACK_TILE),
        in_specs=[pl.BlockSpec((pl.Element(1), pl.Element(PACK_TILE), pl.Element(D_MODEL)),
                               lambda l, i: (l, src_row(i), 0))],
        out_specs=pl.BlockSpec((1, PACK_TILE, D_MODEL), lambda l, i: (l, i, 0)),
        out_shape=jax.ShapeDtypeStruct((depth, W_MAIN, D_MODEL), BF16),
        compiler_params=pltpu.CompilerParams(
            dimension_semantics=("arbitrary", "arbitrary"), vmem_limit_bytes=VMEM_LIMIT),
        name="pack_w_in",
    )(wt)
    return main, small


def _inproj_kernel(x_ref, g_ref, w_ref, ws_ref, o_ref, h_ref):
    j = pl.program_id(1)

    @pl.when(j == 0)
    def _():
        x = x_ref[...]
        ms = jnp.mean(x * x, axis=-1, keepdims=True)
        h_ref[...] = (x * lax.rsqrt(ms + NORM_EPS) * g_ref[...]).astype(BF16)

    def emit(w):
        acc = _dot(h_ref[...], w, NT)
        for q in range(SLABS_PER_TILE):
            o_ref[q] = acc[:, q * LANE:(q + 1) * LANE]

    @pl.when(j < W_MAIN // TILE_N)
    def _main():
        emit(w_ref[0])

    @pl.when(j >= W_MAIN // TILE_N)
    def _small():
        emit(ws_ref[0])


def _inproj(x, g, w, layer, tile_m):
    m = x.shape[0]
    w_main, w_small = w
    n_main = W_MAIN // TILE_N
    return pl.pallas_call(
        _inproj_kernel,
        grid=(m // tile_m, NCOLP // TILE_N),
        in_specs=[
            pl.BlockSpec((tile_m, D_MODEL), lambda i, j: (i, 0)),
            pl.BlockSpec((1, D_MODEL), lambda i, j: (0, 0)),
            pl.BlockSpec((1, TILE_N, D_MODEL), lambda i, j: (layer, jnp.minimum(j, n_main - 1), 0)),
            pl.BlockSpec((1, TILE_N, D_MODEL), lambda i, j: (layer, 0, 0)),
        ],
        out_specs=pl.BlockSpec((SLABS_PER_TILE, tile_m, LANE), lambda i, j: (j, i, 0)),
        out_shape=jax.ShapeDtypeStruct((NSLAB, m, LANE), F32),
        scratch_shapes=[pltpu.VMEM((tile_m, D_MODEL), BF16)],
        compiler_params=pltpu.CompilerParams(
            dimension_semantics=("arbitrary", "arbitrary"), vmem_limit_bytes=VMEM_LIMIT),
        name="inproj",
    )(x, g, w_main, w_small)


def _rwkv_kernel(*refs, C, nsb, nchunk, ngroup, valid_lo, valid_hi, zero_init, n_alias):
    it = iter(refs)
    zr_ref, za_ref, lo_ref, par_ref, mulo_ref, w2_ref, a2_ref, buf_ref = (next(it) for _ in range(8))
    sin_ref = None if zero_init else next(it)
    for _ in range(n_alias):
        next(it)
    oa_ref, sout_ref, shift_ref, bd_ref, prev_ref, at_ref, rt_ref, bt_ref, kt_ref, vm_ref, y_ref = it
    c = pl.program_id(1)
    H = A_HEAD
    R = nsb * C
    W = D_MODEL
    NH = RWKV_UNIT_HEADS
    NG = ngroup
    QL = NH * H
    NQ = W // QL
    units = [(s, q) for s in range(nsb) for q in range(NQ)]

    @pl.when(c == 0)
    def _init():
        prev_ref[...] = buf_ref[...]
        for s, q in units:
            if zero_init:
                bd_ref[s * NQ + q] = jnp.zeros((QL, QL), F32)
            else:
                blocks = []
                for h in range(NH):
                    pieces = [jnp.zeros((H, H), F32)] * NH
                    pieces[h] = sin_ref[0, s, NH * q + h]
                    blocks.append(jnp.concatenate(pieces, axis=1))
                bd_ref[s * NQ + q] = jnp.concatenate(blocks, axis=0)

    row = _iota((R, 1), 0)
    rin = row & (C - 1)
    pos = c * C + rin
    valid = (pos >= valid_lo) & (pos < valid_hi)

    def wide(ref, n0, n):
        return jnp.concatenate([ref[n0 + j] for j in range(n)], axis=1)

    def rowcat(ref, s, n0, n):
        return jnp.concatenate([ref[s, n0 + j:n0 + j + 1, :] for j in range(n)], axis=1)

    SUB = 8
    row8 = _iota((SUB, 1), 0)

    def prev_rows(x, n0, n):
        rolled = pltpu.roll(x, 1, axis=0)
        patches = {}
        for s in range(nsb):
            t0 = s * C
            patches[t0] = jnp.where(row8 == 0, rowcat(prev_ref, s, n0, n), rolled[t0:t0 + SUB])
            if valid_lo > 0:
                t1 = s * C + (valid_lo % C) // SUB * SUB
                hit = (c == valid_lo // C) & (row8 == valid_lo % SUB)
                patches[t1] = jnp.where(hit, rowcat(buf_ref, s, n0, n), patches.get(t1, rolled[t1:t1 + SUB]))
        pieces, t = [], 0
        for t0 in sorted(patches):
            if t0 > t:
                pieces.append(rolled[t:t0])
            pieces.append(patches[t0])
            t = t0 + SUB
        if t < R:
            pieces.append(rolled[t:])
        return jnp.concatenate(pieces, axis=0)

    par = par_ref[...]
    wl_raw = lo_ref[0]
    al_raw = lo_ref[1]
    tw = jnp.tanh(wl_raw + (prev_rows(wl_raw, 3 * NPAIR, 1) - wl_raw) * mulo_ref[0:1, :])
    xa = al_raw + (prev_rows(al_raw, 3 * NPAIR + 1, 1) - al_raw) * mulo_ref[1:2, :]

    seg01 = jnp.where((_iota((QL, QL), 0) & -H) == (_iota((QL, QL), 1) & -H), 1.0, 0.0).astype(BF16)
    ti = _iota((R, R), 0)
    tj = _iota((R, R), 1)
    tri = jnp.where((tj <= ti) & ((ti & -C) == (tj & -C)), 1.0, 0.0).astype(BF16)
    SW = NH * C
    ri = _iota((C, SW), 0)
    ci = _iota((C, SW), 1) & (C - 1)
    strict = ci < ri
    incl = ci <= ri
    eye = jnp.where(ri == ci, 1.0, 0.0)
    lane_i = _iota((1, SW), 1)
    lane_c = _iota((1, QL), 1)
    head_i = [(lane_i >= C * h) & (lane_i < C * (h + 1)) for h in range(NH)]
    head_c = [(lane_c >= H * h) & (lane_c < H * (h + 1)) for h in range(NH)]
    bd_mask = (_iota((QL, QL), 0) & -H) == (_iota((QL, QL), 1) & -H)
    n_dbl = C.bit_length() - 2

    GW = W // NG
    GP = GW // LANE
    GQ = GW // QL
    st = [dict() for _ in range(NG)]

    def glanes(x, g):
        return x[:, g * GW:(g + 1) * GW]

    def seg(x):
        pm = jnp.concatenate([x[:, q * QL:(q + 1) * QL] for q in range(GQ)], axis=0)
        sm = _dot(pm.astype(BF16), seg01)
        return jnp.concatenate([sm[q * R:(q + 1) * R] for q in range(GQ)], axis=1)

    def wide1(g):
        d = st[g]
        pg = glanes(par, g)
        d["par"] = pg
        rz = wide(zr_ref, g * GP, GP)
        kz = wide(zr_ref, NPAIR + g * GP, GP)
        vz = wide(zr_ref, 2 * NPAIR + g * GP, GP)
        d["r"] = rz + (prev_rows(rz, g * GP, GP) - rz) * pg[0:1]
        d["k"] = kz + (prev_rows(kz, NPAIR + g * GP, GP) - kz) * pg[1:2]
        d["v"] = vz + (prev_rows(vz, 2 * NPAIR + g * GP, GP) - vz) * pg[2:3]
        d["wl"] = pg[3:4] + _dot(tw.astype(BF16), w2_ref[:, g * GW:(g + 1) * GW])
        d["alp"] = pg[4:5] + _dot(xa.astype(BF16), a2_ref[:, g * GW:(g + 1) * GW])

    def wide2(g):
        d = st[g]
        pg = d["par"]
        ld = -jnp.exp(_log_sigmoid(d["wl"]) - 0.5)
        d["ld"] = jnp.where(valid, ld, 0.0)
        d["al"] = _sigmoid(d["alp"])
        d["kkv"] = d["k"] * pg[5:6]
        d["k2"] = d["k"] * (1.0 + (d["al"] - 1.0) * pg[6:7])
        d["ssq"] = seg(d["kkv"] * d["kkv"])
        d["L"] = _cumsum_rows(tri, d["ld"])
        d["bsum"] = seg(d["r"] * d["k2"] * pg[7:8])

    def wide3(g):
        d = st[g]
        kkn = d["kkv"] / jnp.maximum(jnp.sqrt(d["ssq"]), 1e-12)
        d["bonus"] = d["bsum"] * d["v"]
        L, ld = d["L"], d["ld"]
        if C >= 32:
            mid = [L[s * C + C // 2 - 1:s * C + C // 2] for s in range(nsb)]
            lref = mid[0]
            for s in range(1, nsb):
                lref = jnp.where(row >= s * C, mid[s], lref)
        else:
            mid = [jnp.zeros((1, GW), F32)] * nsb
            lref = mid[0]
        e_fwd = jnp.exp(L - lref)
        e_bwd = jnp.exp(lref - L)
        gs = slice(g * GW, (g + 1) * GW)
        at_ref[:, gs] = -kkn * jnp.exp(L - ld - lref)
        rt_ref[:, gs] = d["r"] * e_fwd
        bt_ref[:, gs] = jnp.where(valid, kkn * d["al"], 0.0) * e_bwd
        kt_ref[:, gs] = jnp.where(valid, d["k2"], 0.0) * e_bwd
        vm_ref[:, gs] = jnp.where(valid, d["v"], 0.0)
        d["e_mid"] = [jnp.exp(m) for m in mid]
        d["e_last"] = [e_fwd[s * C + C - 1:s * C + C] for s in range(nsb)]
        d["p_last"] = [jnp.exp(L[s * C + C - 1:s * C + C]) for s in range(nsb)]
        for key in ("wl", "alp", "ld", "al", "kkv", "k2", "ssq", "L", "bsum", "k", "r", "v"):
            del d[key]

    def sl(ref, s, q):
        return ref[s * C:(s + 1) * C, q * QL:(q + 1) * QL]

    def blockdiag(x, masks):
        return jnp.concatenate([jnp.where(m, x, jnp.zeros_like(x)) for m in masks], axis=0)

    def units_of(g):
        return [(s, g * GQ + j) for s in range(nsb) for j in range(GQ)]

    unit_keys = ("lhs_ar", "rhs_bk", "a_ab", "a_ak", "a_rb", "a_rk", "inv", "apow", "x2", "av2", "vm", "uu")

    def unit_g(g):
        d = st[g]
        for name in unit_keys:
            d[name] = {}
        for u in units_of(g):
            s, q = u
            d["lhs_ar"][u] = jnp.concatenate([sl(at_ref, s, q), sl(rt_ref, s, q)], axis=0).astype(BF16)
            bt = sl(bt_ref, s, q).astype(BF16)
            kt = sl(kt_ref, s, q).astype(BF16)
            d["rhs_bk"][u] = jnp.concatenate([bt, kt], axis=0)
            gm = _dot(d["lhs_ar"][u],
                      jnp.concatenate([blockdiag(bt, head_c), blockdiag(kt, head_c)], axis=0), NT)
            d["a_ab"][u] = jnp.where(strict, gm[:C, :SW], 0.0)
            d["a_ak"][u] = jnp.where(strict, gm[:C, SW:], 0.0)
            d["a_rb"][u] = jnp.where(incl, gm[C:, :SW], 0.0)
            d["a_rk"][u] = jnp.where(incl, gm[C:, SW:], 0.0)
            d["inv"][u] = eye + d["a_ab"][u]

    def unit_p1(g):
        d = st[g]
        for u in units_of(g):
            pw = d["a_ab"][u].astype(BF16)
            d["apow"][u] = _dot(pw, blockdiag(pw, head_i))

    def unit_level(g, lvl):
        d = st[g]
        for u in units_of(g):
            pw = d["apow"][u].astype(BF16)
            rhs = blockdiag(pw, head_i)
            if lvl < n_dbl - 1:
                both = _dot(jnp.concatenate([pw, d["inv"][u].astype(BF16)], axis=0), rhs)
                d["apow"][u] = both[:C]
                d["inv"][u] = d["inv"][u] + both[C:]
            else:
                d["inv"][u] = d["inv"][u] + _dot(d["inv"][u].astype(BF16), rhs)

    def unit_x2(g):
        d = st[g]
        for u in units_of(g):
            s, q = u
            lo = (q - g * GQ) * QL
            bdm = bd_ref[s * NQ + q] * d["e_mid"][s][:, lo:lo + QL]
            d["x2"][u] = _dot(d["lhs_ar"][u], bdm.astype(BF16), NT)

    def unit_av2(g):
        d = st[g]
        for u in units_of(g):
            s, q = u
            d["vm"][u] = sl(vm_ref, s, q).astype(BF16)
            d["av2"][u] = _dot(jnp.concatenate([d["a_ak"][u], d["a_rk"][u]], axis=0).astype(BF16),
                               blockdiag(d["vm"][u], head_c))

    def unit_u(g):
        d = st[g]
        for u in units_of(g):
            w1a = (d["x2"][u][:C] + d["av2"][u][:C]).astype(BF16)
            d["uu"][u] = _dot(d["inv"][u].astype(BF16), blockdiag(w1a, head_c)).astype(BF16)

    def unit_y(g):
        d = st[g]
        for u in units_of(g):
            s, q = u
            y_ref[s * C:(s + 1) * C, q * QL:(q + 1) * QL] = (
                d["x2"][u][C:] + d["av2"][u][C:] + _dot(d["a_rb"][u].astype(BF16), blockdiag(d["uu"][u], head_c)))

    def unit_upd(g):
        d = st[g]
        for u in units_of(g):
            s, q = u
            lo = (q - g * GQ) * QL
            upd = _dot(jnp.concatenate([d["uu"][u], d["vm"][u]], axis=0), d["rhs_bk"][u], TN)
            bd = bd_ref[s * NQ + q] * d["p_last"][s][:, lo:lo + QL] + upd * d["e_last"][s][:, lo:lo + QL]
            bd_ref[s * NQ + q] = jnp.where(bd_mask, bd, 0.0)
        for name in unit_keys:
            del d[name]

    def post1(g):
        d = st[g]
        d["y"] = y_ref[:, g * GW:(g + 1) * GW]
        d["mean"] = seg(d["y"]) * (1.0 / H)

    def post2(g):
        d = st[g]
        d["yc"] = d["y"] - d["mean"]
        d["var"] = seg(d["yc"] * d["yc"]) * (1.0 / H)

    def post3(g):
        d = st[g]
        pg = d["par"]
        out = ((d["yc"] * lax.rsqrt(d["var"] + LNX_EPS) * pg[8:9] + pg[9:10] + d["bonus"])
               * _silu(wide(za_ref, g * GP, GP)))
        for p in range(GP):
            oa_ref[g * GP + p] = out[:, p * LANE:(p + 1) * LANE]

    unit_stages = ([unit_g, unit_p1] + [functools.partial(unit_level, lvl=i) for i in range(n_dbl)]
                   + [unit_x2, unit_av2, unit_u, unit_y, unit_upd])
    for f in (wide1, wide2, wide3):
        f(0)
    for g in range(NG):
        side = []
        if g + 1 < NG:
            side += [(wide1, g + 1), (wide2, g + 1), (wide3, g + 1)]
        if g > 0:
            side += [(post1, g - 1), (post2, g - 1), (post3, g - 1)]
        every = max(1, len(unit_stages) // max(1, len(side)))
        for j, stage in enumerate(unit_stages):
            stage(g)
            if side and (j + 1) % every == 0:
                f, gg = side.pop(0)
                f(gg)
        for f, gg in side:
            f(gg)
    for f in (post1, post2, post3):
        f(NG - 1)

    if nchunk > 1:
        for s in range(nsb):
            last = s * C + C - 1
            for j in range(3 * NPAIR):
                prev_ref[s, j:j + 1, :] = zr_ref[j, last:last + 1, :]
            prev_ref[s, 3 * NPAIR:3 * NPAIR + 1, :] = wl_raw[last:last + 1]
            prev_ref[s, 3 * NPAIR + 1:3 * NPAIR + 2, :] = al_raw[last:last + 1]

    @pl.when(c == nchunk - 1)
    def _fin():
        for s, q in units:
            bd = bd_ref[s * NQ + q]
            for h in range(NH):
                sout_ref[0, s, NH * q + h] = bd[h * H:(h + 1) * H, h * H:(h + 1) * H]
        shift_ref[...] = jnp.zeros(shift_ref.shape, F32)
        for s in range(nsb):
            last = s * C + (valid_hi - 1) - (nchunk - 1) * C
            for j in range(3 * NPAIR):
                shift_ref[s, j:j + 1, :] = zr_ref[j, last:last + 1, :]
            shift_ref[s, 3 * NPAIR:3 * NPAIR + 1, :] = wl_raw[last:last + 1]
            shift_ref[s, 3 * NPAIR + 1:3 * NPAIR + 2, :] = al_raw[last:last + 1]


def _rwkv(z3, par, mulo, w2, a2, buf, *, state, prev_oa, prev_state, layer, depth, nseq, nsb, nchunk, ngroup, C,
          row0, valid_lo, valid_hi):
    R = nsb * C
    rb = lambda n, c: row0 + c * (nseq // nsb) + n
    state_block = (1, nsb, A_HEADS, A_HEAD, A_HEAD)
    state_map = lambda n, c: (layer, n, 0, 0, 0)
    in_specs = [
        pl.BlockSpec((3 * NPAIR, R, LANE), lambda n, c: (0, rb(n, c), 0)),
        pl.BlockSpec((NPAIR, R, LANE), lambda n, c: (SL_ZA // NPAIR, rb(n, c), 0)),
        pl.BlockSpec((2, R, LANE), lambda n, c: (SL_WLO // 2, rb(n, c), 0)),
        pl.BlockSpec((N_PAR_ROWS, D_MODEL), lambda n, c: (0, 0)),
        pl.BlockSpec((8, LANE), lambda n, c: (0, 0)),
        pl.BlockSpec((LANE, D_MODEL), lambda n, c: (0, 0)),
        pl.BlockSpec((LANE, D_MODEL), lambda n, c: (0, 0)),
        pl.BlockSpec((nsb, N_BUF_ROWS, LANE), lambda n, c: (n, 0, 0)),
    ]
    args = [z3, z3, z3, par, mulo, w2, a2, buf]
    if state is not None:
        in_specs.append(pl.BlockSpec(state_block, state_map))
        args.append(state)
    aliases = {}
    for out_idx, prev in enumerate((prev_oa, prev_state)):
        if prev is not None:
            aliases[len(args)] = out_idx
            in_specs.append(pl.BlockSpec(memory_space=pl.ANY))
            args.append(prev)
    kern = functools.partial(_rwkv_kernel, C=C, nsb=nsb, nchunk=nchunk, ngroup=ngroup, valid_lo=valid_lo,
                             valid_hi=valid_hi, zero_init=state is None, n_alias=len(aliases))
    wide_scratch = pltpu.VMEM((R, D_MODEL), F32)
    unit_lanes = RWKV_UNIT_HEADS * A_HEAD
    return pl.pallas_call(
        kern,
        grid=(nseq // nsb, nchunk),
        in_specs=in_specs,
        out_specs=[
            pl.BlockSpec((NPAIR, R, LANE), lambda n, c: (0, rb(n, c), 0)),
            pl.BlockSpec(state_block, state_map),
            pl.BlockSpec((nsb, N_BUF_ROWS, LANE), lambda n, c: (n, 0, 0)),
        ],
        out_shape=[
            jax.ShapeDtypeStruct((NPAIR, z3.shape[1], LANE), F32),
            jax.ShapeDtypeStruct((depth, nseq, A_HEADS, A_HEAD, A_HEAD), F32),
            jax.ShapeDtypeStruct((nseq, N_BUF_ROWS, LANE), F32),
        ],
        input_output_aliases=aliases,
        scratch_shapes=[
            pltpu.VMEM((nsb * D_MODEL // unit_lanes, unit_lanes, unit_lanes), F32),
            pltpu.VMEM((nsb, N_BUF_ROWS, LANE), F32),
        ] + [wide_scratch] * 6,
        compiler_params=pltpu.CompilerParams(
            dimension_semantics=("arbitrary", "arbitrary"), vmem_limit_bytes=VMEM_LIMIT),
        name="rwkv_c%d" % C,
    )(*args)


def _gla_kernel(*refs, C, SB, nchunk, valid_lo, valid_hi, zero_init, n_alias):
    it = iter(refs)
    q_ref, k_ref, v_ref, zb_ref, gk_ref, gwh_ref, gwl_ref, nw_ref = (next(it) for _ in range(8))
    sin_ref = None if zero_init else next(it)
    for _ in range(n_alias):
        next(it)
    ob_ref, sout_ref, s_ref, att_ref = it
    c = pl.program_id(1)
    heads = range(B_HEADS)
    QS = B_HK // LANE
    VS = B_HV // LANE

    @pl.when(c == 0)
    def _init():
        for h in heads:
            s_ref[h] = jnp.zeros((B_HK, B_HV), F32) if zero_init else sin_ref[0, 0, h]

    pos = c * C + _iota((C, 1), 0)
    valid = (pos >= valid_lo) & (pos < valid_hi)

    glo1 = jnp.where(_iota((C, LANE), 1) == GATE_LORA, 1.0, gk_ref[0])
    g = jnp.where(valid, _log_sigmoid(_mm3(glo1, gwh_ref[...], gwl_ref[...])) / GATE_TAU, 0.0)
    ti = _iota((C, C), 0)
    tj = _iota((C, C), 1)
    tri = jnp.where(tj <= ti, 1.0, 0.0).astype(BF16)
    ones = jnp.ones((C, LANE), BF16)
    parts = _split2(g)
    b = _dot(tri, parts[0]) + _dot(tri, parts[1])
    b_col = _dot(parts[0], ones, TN) + _dot(parts[1], ones, TN)
    eb = jnp.exp(b)
    ed = jnp.exp(b[C - 1:C] - b)

    def hk(x, h):
        return x[:, h * B_HK:(h + 1) * B_HK]

    q = [jnp.concatenate([q_ref[QS * h + j] for j in range(QS)], axis=1) * (B_HK ** -0.5) for h in heads]
    k = [jnp.where(valid, jnp.concatenate([k_ref[QS * h + j] for j in range(QS)], axis=1), 0.0) for h in heads]
    v = [jnp.where(valid, jnp.concatenate([v_ref[VS * h + j] for j in range(VS)], axis=1), 0.0).astype(BF16)
         for h in heads]

    o = [_mm(q[h] * hk(eb, h), s_ref[h]) for h in heads]

    att_ref[...] = jnp.zeros(att_ref.shape, F32)
    nb = C // SB
    for i in range(1, nb):
        lo, hi = SB * i, SB * (i + 1)
        for h in heads:
            bh = hk(b, h)
            beta = bh[lo - 1:lo]
            qi = q[h][lo:hi] * jnp.exp(bh[lo:hi] - beta)
            kj = k[h][:lo] * jnp.exp(beta - bh[:lo])
            att_ref[h, :lo, lo:hi] = _mm(kj, qi, NT)

    srow = _iota((SB, 1), 0)

    def _diag_matmul():
        causal = srow <= _iota((SB, SB), 1)
        for i in range(nb):
            lo = SB * i
            for h in heads:
                bb = hk(b, h)[lo:lo + SB]
                qh = q[h][lo:lo + SB] * jnp.exp(bb - bb[0:1])
                kh = k[h][lo:lo + SB] * jnp.exp(bb[0:1] - bb)
                att_ref[h, lo:lo + SB, lo:lo + SB] = jnp.where(causal, _mm(kh, qh, NT), 0.0)

    def _diag_exact():
        for i in range(nb):
            lo = SB * i
            for h in heads:
                bb = hk(b, h)[lo:lo + SB]
                kb = k[h][lo:lo + SB]
                for t in range(SB):
                    e = jnp.exp(jnp.where(srow <= t, bb[t:t + 1] - bb, -jnp.inf))
                    att_ref[h, lo:lo + SB, lo + t:lo + t + 1] = jnp.sum(
                        q[h][lo + t:lo + t + 1] * kb * e, axis=-1, keepdims=True)

    if nb == 1:
        _diag_exact()
    else:
        span = jnp.max(jnp.concatenate(
            [b[SB * i:SB * i + 1] - b[SB * i + SB - 1:SB * i + SB] for i in range(nb)], axis=0))
        pl.when(span < GLA_FACTOR_MAX_DECAY)(_diag_matmul)
        pl.when(span >= GLA_FACTOR_MAX_DECAY)(_diag_exact)

    for h in heads:
        o[h] = o[h] + _dot(att_ref[h].astype(BF16), v[h], TN)

    for h in heads:
        ms = jnp.mean(o[h] * o[h], axis=-1, keepdims=True)
        on = o[h] * lax.rsqrt(ms + NORM_EPS) * nw_ref[...]
        for j in range(VS):
            ob_ref[VS * h + j] = on[:, j * LANE:(j + 1) * LANE] * _silu(zb_ref[VS * h + j])

    for h in heads:
        decay = jnp.exp(b_col[h * B_HK:(h + 1) * B_HK])
        decay = jnp.concatenate([decay] * VS, axis=1)
        s_ref[h] = s_ref[h] * decay + _dot((k[h] * hk(ed, h)).astype(BF16), v[h], TN)

    @pl.when(c == nchunk - 1)
    def _fin():
        for h in heads:
            sout_ref[0, 0, h] = s_ref[h]


def _gla(z3, gw, nw, *, state, prev_ob, prev_state, layer, depth, nseq, nchunk, C, SB, row0, valid_lo, valid_hi):
    rb = lambda n, c: row0 + c * nseq + n
    state_block = (1, 1, B_HEADS, B_HK, B_HV)
    state_map = lambda n, c: (layer, n, 0, 0, 0)
    qs = B_HEADS * B_HK // LANE
    vs = B_HEADS * B_HV // LANE
    in_specs = [
        pl.BlockSpec((qs, C, LANE), lambda n, c: (SL_QB // qs, rb(n, c), 0)),
        pl.BlockSpec((qs, C, LANE), lambda n, c: (SL_KB // qs, rb(n, c), 0)),
        pl.BlockSpec((vs, C, LANE), lambda n, c: (SL_VB // vs, rb(n, c), 0)),
        pl.BlockSpec((vs, C, LANE), lambda n, c: (SL_ZB // vs, rb(n, c), 0)),
        pl.BlockSpec((1, C, LANE), lambda n, c: (SL_GK, rb(n, c), 0)),
        pl.BlockSpec((LANE, B_HEADS * B_HK), lambda n, c: (0, 0)),
        pl.BlockSpec((LANE, B_HEADS * B_HK), lambda n, c: (0, 0)),
        pl.BlockSpec((1, B_HV), lambda n, c: (0, 0)),
    ]
    args = [z3, z3, z3, z3, z3, *gw, nw]
    if state is not None:
        in_specs.append(pl.BlockSpec(state_block, state_map))
        args.append(state)
    aliases = {}
    for out_idx, prev in enumerate((prev_ob, prev_state)):
        if prev is not None:
            aliases[len(args)] = out_idx
            in_specs.append(pl.BlockSpec(memory_space=pl.ANY))
            args.append(prev)
    kern = functools.partial(_gla_kernel, C=C, SB=SB, nchunk=nchunk, valid_lo=valid_lo, valid_hi=valid_hi,
                             zero_init=state is None, n_alias=len(aliases))
    return pl.pallas_call(
        kern,
        grid=(nseq, nchunk),
        in_specs=in_specs,
        out_specs=[
            pl.BlockSpec((vs, C, LANE), lambda n, c: (0, rb(n, c), 0)),
            pl.BlockSpec(state_block, state_map),
        ],
        out_shape=[
            jax.ShapeDtypeStruct((NPAIR, z3.shape[1], LANE), F32),
            jax.ShapeDtypeStruct((depth, nseq, B_HEADS, B_HK, B_HV), F32),
        ],
        input_output_aliases=aliases,
        scratch_shapes=[
            pltpu.VMEM((B_HEADS, B_HK, B_HV), F32),
            pltpu.VMEM((B_HEADS, C, C), F32),
        ],
        compiler_params=pltpu.CompilerParams(
            dimension_semantics=("arbitrary", "arbitrary"), vmem_limit_bytes=VMEM_LIMIT),
        name="gla_c%d" % C,
    )(*args)


def _outproj_kernel(oa_ref, ob_ref, ga_ref, gb_ref, x_ref, w_ref, g_ref, o_ref, m_ref):
    for q in range(NPAIR):
        m = _sigmoid(ga_ref[q]) * oa_ref[q] + _sigmoid(gb_ref[q]) * ob_ref[q]
        m_ref[:, q * LANE:(q + 1) * LANE] = m.astype(BF16)
    y = jnp.dot(m_ref[...], w_ref[...], preferred_element_type=F32)
    ms = jnp.mean(y * y, axis=-1, keepdims=True)
    o_ref[...] = x_ref[...] + y * lax.rsqrt(ms + NORM_EPS) * g_ref[...]


def _outproj(oa3, ob3, z3, x, w, g, tile_m):
    m = x.shape[0]
    return pl.pallas_call(
        _outproj_kernel,
        grid=(m // tile_m,),
        in_specs=[
            pl.BlockSpec((NPAIR, tile_m, LANE), lambda i: (0, i, 0)),
            pl.BlockSpec((NPAIR, tile_m, LANE), lambda i: (0, i, 0)),
            pl.BlockSpec((NPAIR, tile_m, LANE), lambda i: (SL_GA // NPAIR, i, 0)),
            pl.BlockSpec((NPAIR, tile_m, LANE), lambda i: (SL_GB // NPAIR, i, 0)),
            pl.BlockSpec((tile_m, D_MODEL), lambda i: (i, 0)),
            pl.BlockSpec((D_MODEL, D_MODEL), lambda i: (0, 0)),
            pl.BlockSpec((1, D_MODEL), lambda i: (0, 0)),
        ],
        out_specs=pl.BlockSpec((tile_m, D_MODEL), lambda i: (i, 0)),
        out_shape=jax.ShapeDtypeStruct((m, D_MODEL), F32),
        scratch_shapes=[pltpu.VMEM((tile_m, D_MODEL), BF16)],
        compiler_params=pltpu.CompilerParams(
            dimension_semantics=("arbitrary",), vmem_limit_bytes=VMEM_LIMIT),
        name="outproj",
    )(oa3, ob3, z3, z3, x, w, g)


def _pick_tile(m, cap, mult):
    best = mult
    for t in range(mult, cap + 1, mult):
        if m % t == 0:
            best = t
    assert m % best == 0
    return best


def _pad_lanes(x):
    return jnp.pad(x, [(0, 0)] * (x.ndim - 1) + [(0, LANE - x.shape[-1])])


def _shift_rows(buf):
    n = buf.shape[0]
    rkv = buf[:, :3 * NPAIR].reshape(n, 3 * D_MODEL)
    return jnp.concatenate([rkv, buf[:, 3 * NPAIR, :LORA], buf[:, 3 * NPAIR + 1, :LORA]], axis=1)


def kernel(x_prompt, x_sample, state_rwkv, state_shift, state_gla, meta_tokens, norm_pre, w_in, rwkv_mu, rwkv_w0, rwkv_w2, rwkv_a0, rwkv_a2, rwkv_k_k, rwkv_k_a, rwkv_r_k, rwkv_lnx_w, rwkv_lnx_b, gla_gk_w2, gla_gk_b, gla_norm_w, w_out, norm_post):
    depth = w_in.shape[0]
    bp, seq, _ = x_prompt.shape
    bs, dseq, _ = x_sample.shape
    front = CHUNK - N_META
    tp = front + N_META + seq
    assert tp % CHUNK == 0 and dseq <= SAMPLE_ROWS and bs % RWKV_SAMPLE_SEQS == 0
    nchunk = tp // CHUNK
    mp = bp * tp
    ms = bs * SAMPLE_ROWS
    mtot = mp + ms
    tile_in = _pick_tile(mtot, 1184, 16)
    tile_out = _pick_tile(mtot, 296, 8)

    meta = jnp.broadcast_to(meta_tokens[None], (bp, N_META, D_MODEL))
    xp = jnp.concatenate([jnp.zeros((bp, front, D_MODEL), F32), meta, x_prompt], axis=1)
    xp = xp.reshape(bp, nchunk, CHUNK, D_MODEL).transpose(1, 0, 2, 3).reshape(mp, D_MODEL)
    xs = jnp.pad(x_sample, ((0, 0), (0, SAMPLE_ROWS - dseq), (0, 0))).reshape(ms, D_MODEL)
    x = jnp.concatenate([xp, xs], axis=0)

    w_packed = _pack_w_in(w_in)
    w_out_bf = w_out.astype(BF16)
    zero_buf = jnp.zeros((bp, N_BUF_ROWS, LANE), F32)

    sr_p = sr_s = sg_p = sg_s = None
    shift_p, shift_s = [], []
    for l in range(depth):
        mu = rwkv_mu[l]
        par = jnp.stack([mu[:D_MODEL], mu[D_MODEL:2 * D_MODEL], mu[2 * D_MODEL:3 * D_MODEL],
                         rwkv_w0[l], rwkv_a0[l], rwkv_k_k[l], rwkv_k_a[l], rwkv_r_k[l].reshape(-1),
                         rwkv_lnx_w[l], rwkv_lnx_b[l]], axis=0)
        par = jnp.pad(par, ((0, N_PAR_ROWS - par.shape[0]), (0, 0)))
        mulo = jnp.pad(jnp.stack([_pad_lanes(mu[3 * D_MODEL:3 * D_MODEL + LORA]),
                                  _pad_lanes(mu[3 * D_MODEL + LORA:])], axis=0), ((0, 6), (0, 0)))
        w2 = jnp.pad(rwkv_w2[l], ((0, LANE - LORA), (0, 0))).astype(BF16)
        a2 = jnp.pad(rwkv_a2[l], ((0, LANE - LORA), (0, 0))).astype(BF16)
        gw = jnp.concatenate([gla_gk_w2[l], gla_gk_b[l][None]], axis=0)
        gw = _split2(jnp.pad(gw, ((0, LANE - gw.shape[0]), (0, 0))))
        nw = gla_norm_w[l][None]
        sh = state_shift[l]
        buf_s = jnp.concatenate([sh[:, :3 * D_MODEL].reshape(bs, 3 * NPAIR, LANE),
                                 _pad_lanes(sh[:, 3 * D_MODEL:3 * D_MODEL + LORA])[:, None],
                                 _pad_lanes(sh[:, 3 * D_MODEL + LORA:])[:, None]], axis=1)
        buf_s = jnp.pad(buf_s, ((0, 0), (0, N_BUF_ROWS - buf_s.shape[1]), (0, 0)))

        z3 = _inproj(x, norm_pre[l][None], w_packed, l, tile_in)

        prompt = dict(layer=l, depth=depth, nseq=bp, nchunk=nchunk, C=CHUNK, row0=0, valid_lo=front, valid_hi=tp)
        sample = dict(layer=l, depth=depth, nseq=bs, nchunk=1, C=SAMPLE_ROWS, valid_lo=0, valid_hi=dseq)
        oa, sr_p, sh_p = _rwkv(z3, par, mulo, w2, a2, zero_buf, state=None, prev_oa=None, prev_state=sr_p,
                               nsb=RWKV_PROMPT_SEQS, ngroup=RWKV_LANE_GROUPS, **prompt)
        oa, sr_s, sh_s = _rwkv(z3, par, mulo, w2, a2, buf_s, state=state_rwkv, prev_oa=oa, prev_state=sr_s,
                               nsb=RWKV_SAMPLE_SEQS, ngroup=1, row0=mp // (SAMPLE_ROWS * RWKV_SAMPLE_SEQS),
                               **sample)
        ob, sg_p = _gla(z3, gw, nw, state=None, prev_ob=None, prev_state=sg_p, SB=GLA_SUB, **prompt)
        ob, sg_s = _gla(z3, gw, nw, state=state_gla, prev_ob=ob, prev_state=sg_s, SB=SAMPLE_ROWS,
                        row0=mp // SAMPLE_ROWS, **sample)

        x = _outproj(oa, ob, z3, x, w_out_bf[l], norm_post[l][None], tile_out)
        shift_p.append(_shift_rows(sh_p))
        shift_s.append(_shift_rows(sh_s))

    y_prompt = x[:mp].reshape(nchunk, bp, CHUNK, D_MODEL).transpose(1, 0, 2, 3).reshape(bp, tp, D_MODEL)
    y_prompt = y_prompt[:, front + N_META:]
    y_sample = x[mp:].reshape(bs, SAMPLE_ROWS, D_MODEL)[:, :dseq]
    return (y_prompt, y_sample, sr_p, jnp.stack(shift_p), sg_p, sr_s, jnp.stack(shift_s), sg_s)
```

```python
import functools
import math

import jax
import jax.numpy as jnp
from jax import lax
from jax.experimental import pallas as pl
from jax.experimental.pallas import tpu as pltpu

F32 = jnp.float32
BF16 = jnp.bfloat16

D_MODEL = 2048
N_META = 16
NORM_EPS = 1e-6
LNX_EPS = 64e-5
A_HEAD = 64
A_HEADS = D_MODEL // A_HEAD
LORA = 96
B_HEADS = 4
B_HK = 256
B_HV = 512
GATE_LORA = 16
GATE_TAU = 16.0
CHUNK = 64
SAMPLE_ROWS = 8
GLA_SUB = 16
GLA_FACTOR_MAX_DECAY = 60.0
RWKV_SAMPLE_SEQS = 4
GLA_SAMPLE_SEQS = 2
RWKV_PROMPT_SEQS = 2
RWKV_UNIT_HEADS = 4
RWKV_LANE_GROUPS = 2

LANE = 128
NPAIR = D_MODEL // LANE
SLABS_PER_TILE = 8
TILE_N = SLABS_PER_TILE * LANE
SL_R, SL_K, SL_V, SL_ZA = 0, 16, 32, 48
SL_QB, SL_KB, SL_VB, SL_ZB = 64, 72, 80, 96
SL_GA, SL_GB = 112, 128
SL_WLO, SL_ALO, SL_GK = 144, 145, 146
NSLAB = 152
NCOLP = NSLAB * LANE
N_BUF_ROWS = 56
N_PAR_ROWS = 16
VMEM_LIMIT = 56 * 1024 * 1024

NN = ((1,), (0,))
NT = ((1,), (1,))
TN = ((0,), (0,))


def _dot(a, b, dims=NN):
    return lax.dot_general(a, b, (dims, ((), ())), preferred_element_type=F32)


def _mm(a, b, dims=NN):
    return _dot(a.astype(BF16), b.astype(BF16), dims)


def _split2(x):
    hi = x.astype(BF16)
    lo = (x - hi.astype(F32)).astype(BF16)
    return hi, lo


def _mm3(a, b_hi, b_lo, dims=NN):
    ah, al = _split2(a)
    return _dot(ah, b_hi, dims) + _dot(al, b_hi, dims) + _dot(ah, b_lo, dims)


def _cumsum_rows(tri, x):
    hi, lo = _split2(x)
    return _dot(tri, hi) + _dot(tri, lo)


def _sigmoid(x):
    return 1.0 / (1.0 + jnp.exp(-x))


def _silu(x):
    return x * _sigmoid(x)


def _log_sigmoid(x):
    return jnp.minimum(x, 0.0) - jnp.log(1.0 + jnp.exp(-jnp.abs(x)))


def _iota(shape, dim):
    return lax.broadcasted_iota(jnp.int32, shape, dim)


W_RKV = 3 * D_MODEL
W_SHIFT = W_RKV + 2 * LORA
W_MID = 3 * D_MODEL
W_GK = W_SHIFT + W_MID
W_TAIL = 3 * D_MODEL


PACK_TILE = 512
W_MAIN = W_RKV + W_MID + W_TAIL
N_SMALL = NCOLP - W_MAIN


def _pack_kernel(w_ref, o_ref):
    o_ref[0] = w_ref[0].astype(BF16)


def _pack_w_in(w_in):
    depth, d, ncol = w_in.shape
    assert ncol == W_GK + GATE_LORA + W_TAIL and d == D_MODEL and N_SMALL == TILE_N
    wt = jnp.swapaxes(w_in, 1, 2)
    small = jnp.concatenate([jnp.pad(wt[:, a:a + n], ((0, 0), (0, LANE - n), (0, 0)))
                             for a, n in ((W_RKV, LORA), (W_RKV + LORA, LORA), (W_GK, GATE_LORA))], axis=1)
    small = lax.optimization_barrier(jnp.pad(small, ((0, 0), (0, N_SMALL - small.shape[1]), (0, 0)))).astype(BF16)
    g1 = W_RKV // PACK_TILE
    g2 = (W_RKV + W_MID) // PACK_TILE
    skip1 = W_SHIFT - W_RKV
    skip2 = W_GK + GATE_LORA - (W_RKV + W_MID) - skip1

    def src_row(i):
        row = i * PACK_TILE + (i >= g1).astype(jnp.int32) * skip1 + (i >= g2).astype(jnp.int32) * skip2
        return pl.multiple_of(row, math.gcd(PACK_TILE, skip1, skip2))

    main = pl.pallas_call(
        _pack_kernel,
        grid=(depth, W_MAIN // PACK_TILE),
        in_specs=[pl.BlockSpec((pl.Element(1), pl.Element(PACK_TILE), pl.Element(D_MODEL)),
                               lambda l, i: (l, src_row(i), 0))],
        out_specs=pl.BlockSpec((1, PACK_TILE, D_MODEL), lambda l, i: (l, i, 0)),
        out_shape=jax.ShapeDtypeStruct((depth, W_MAIN, D_MODEL), BF16),
        compiler_params=pltpu.CompilerParams(
            dimension_semantics=("arbitrary", "arbitrary"), vmem_limit_bytes=VMEM_LIMIT),
        name="pack_w_in",
    )(wt)
    return main, small


def _inproj_kernel(x_ref, g_ref, w_ref, ws_ref, o_ref, h_ref):
    j = pl.program_id(1)

    @pl.when(j == 0)
    def _():
        x = x_ref[...]
        ms = jnp.mean(x * x, axis=-1, keepdims=True)
        h_ref[...] = (x * lax.rsqrt(ms + NORM_EPS) * g_ref[...]).astype(BF16)

    def emit(w):
        acc = _dot(h_ref[...], w, NT)
        for q in range(SLABS_PER_TILE):
            o_ref[q] = acc[:, q * LANE:(q + 1) * LANE]

    @pl.when(j < W_MAIN // TILE_N)
    def _main():
        emit(w_ref[0])

    @pl.when(j >= W_MAIN // TILE_N)
    def _small():
        emit(ws_ref[0])


def _inproj(x, g, w, layer, tile_m):
    m = x.shape[0]
    w_main, w_small = w
    n_main = W_MAIN // TILE_N
    return pl.pallas_call(
        _inproj_kernel,
        grid=(m // tile_m, NCOLP // TILE_N),
        in_specs=[
            pl.BlockSpec((tile_m, D_MODEL), lambda i, j: (i, 0)),
            pl.BlockSpec((1, D_MODEL), lambda i, j: (0, 0)),
            pl.BlockSpec((1, TILE_N, D_MODEL), lambda i, j: (layer, jnp.minimum(j, n_main - 1), 0)),
            pl.BlockSpec((1, TILE_N, D_MODEL), lambda i, j: (layer, 0, 0)),
        ],
        out_specs=pl.BlockSpec((SLABS_PER_TILE, tile_m, LANE), lambda i, j: (j, i, 0)),
        out_shape=jax.ShapeDtypeStruct((NSLAB, m, LANE), F32),
        scratch_shapes=[pltpu.VMEM((tile_m, D_MODEL), BF16)],
        compiler_params=pltpu.CompilerParams(
            dimension_semantics=("arbitrary", "arbitrary"), vmem_limit_bytes=VMEM_LIMIT),
        name="inproj",
    )(x, g, w_main, w_small)


def _rwkv_kernel(*refs, C, nsb, nchunk, ngroup, valid_lo, valid_hi, zero_init, n_alias):
    it = iter(refs)
    zr_ref, za_ref, lo_ref, par_ref, mulo_ref, w2_ref, a2_ref, buf_ref = (next(it) for _ in range(8))
    sin_ref = None if zero_init else next(it)
    for _ in range(n_alias):
        next(it)
    oa_ref, sout_ref, shift_ref, bd_ref, prev_ref, at_ref, rt_ref, bt_ref, kt_ref, vm_ref, y_ref = it
    c = pl.program_id(1)
    H = A_HEAD
    R = nsb * C
    W = D_MODEL
    NH = RWKV_UNIT_HEADS
    NG = ngroup
    QL = NH * H
    NQ = W // QL
    units = [(s, q) for s in range(nsb) for q in range(NQ)]

    @pl.when(c == 0)
    def _init():
        prev_ref[...] = buf_ref[...]
        for s, q in units:
            if zero_init:
                bd_ref[s * NQ + q] = jnp.zeros((QL, QL), F32)
            else:
                blocks = []
                for h in range(NH):
                    pieces = [jnp.zeros((H, H), F32)] * NH
                    pieces[h] = sin_ref[0, s, NH * q + h]
                    blocks.append(jnp.concatenate(pieces, axis=1))
                bd_ref[s * NQ + q] = jnp.concatenate(blocks, axis=0)

    row = _iota((R, 1), 0)
    rin = row & (C - 1)
    pos = c * C + rin
    valid = (pos >= valid_lo) & (pos < valid_hi)

    def wide(ref, n0, n):
        return jnp.concatenate([ref[n0 + j] for j in range(n)], axis=1)

    def rowcat(ref, s, n0, n):
        return jnp.concatenate([ref[s, n0 + j:n0 + j + 1, :] for j in range(n)], axis=1)

    SUB = 8
    row8 = _iota((SUB, 1), 0)

    def prev_rows(x, n0, n):
        rolled = pltpu.roll(x, 1, axis=0)
        patches = {}
        for s in range(nsb):
            t0 = s * C
            patches[t0] = jnp.where(row8 == 0, rowcat(prev_ref, s, n0, n), rolled[t0:t0 + SUB])
            if valid_lo > 0:
                t1 = s * C + (valid_lo % C) // SUB * SUB
                hit = (c == valid_lo // C) & (row8 == valid_lo % SUB)
                patches[t1] = jnp.where(hit, rowcat(buf_ref, s, n0, n), patches.get(t1, rolled[t1:t1 + SUB]))
        pieces, t = [], 0
        for t0 in sorted(patches):
            if t0 > t:
                pieces.append(rolled[t:t0])
            pieces.append(patches[t0])
            t = t0 + SUB
        if t < R:
            pieces.append(rolled[t:])
        return jnp.concatenate(pieces, axis=0)

    par = par_ref[...]
    wl_raw = lo_ref[0]
    al_raw = lo_ref[1]
    tw = jnp.tanh(wl_raw + (prev_rows(wl_raw, 3 * NPAIR, 1) - wl_raw) * mulo_ref[0:1, :])
    xa = al_raw + (prev_rows(al_raw, 3 * NPAIR + 1, 1) - al_raw) * mulo_ref[1:2, :]

    seg01 = jnp.where((_iota((QL, QL), 0) & -H) == (_iota((QL, QL), 1) & -H), 1.0, 0.0).astype(BF16)
    ti = _iota((R, R), 0)
    tj = _iota((R, R), 1)
    tri = jnp.where((tj <= ti) & ((ti & -C) == (tj & -C)), 1.0, 0.0).astype(BF16)
    SW = NH * C
    ri = _iota((C, SW), 0)
    ci = _iota((C, SW), 1) & (C - 1)
    strict = ci < ri
    incl = ci <= ri
    eye = jnp.where(ri == ci, 1.0, 0.0)
    lane_i = _iota((1, SW), 1)
    lane_c = _iota((1, QL), 1)
    head_i = [(lane_i >= C * h) & (lane_i < C * (h + 1)) for h in range(NH)]
    head_c = [(lane_c >= H * h) & (lane_c < H * (h + 1)) for h in range(NH)]
    bd_mask = (_iota((QL, QL), 0) & -H) == (_iota((QL, QL), 1) & -H)
    n_dbl = C.bit_length() - 2

    GW = W // NG
    GP = GW // LANE
    GQ = GW // QL
    st = [dict() for _ in range(NG)]

    def glanes(x, g):
        return x[:, g * GW:(g + 1) * GW]

    def seg(x):
        pm = jnp.concatenate([x[:, q * QL:(q + 1) * QL] for q in range(GQ)], axis=0)
        sm = _dot(pm.astype(BF16), seg01)
        return jnp.concatenate([sm[q * R:(q + 1) * R] for q in range(GQ)], axis=1)

    def wide1(g):
        d = st[g]
        pg = glanes(par, g)
        d["par"] = pg
        rz = wide(zr_ref, g * GP, GP)
        kz = wide(zr_ref, NPAIR + g * GP, GP)
        vz = wide(zr_ref, 2 * NPAIR + g * GP, GP)
        d["r"] = rz + (prev_rows(rz, g * GP, GP) - rz) * pg[0:1]
        d["k"] = kz + (prev_rows(kz, NPAIR + g * GP, GP) - kz) * pg[1:2]
        d["v"] = vz + (prev_rows(vz, 2 * NPAIR + g * GP, GP) - vz) * pg[2:3]
        d["wl"] = pg[3:4] + _dot(tw.astype(BF16), w2_ref[:, g * GW:(g + 1) * GW])
        d["alp"] = pg[4:5] + _dot(xa.astype(BF16), a2_ref[:, g * GW:(g + 1) * GW])

    def wide2(g):
        d = st[g]
        pg = d["par"]
        ld = -jnp.exp(_log_sigmoid(d["wl"]) - 0.5)
        d["ld"] = jnp.where(valid, ld, 0.0)
        d["al"] = _sigmoid(d["alp"])
        d["kkv"] = d["k"] * pg[5:6]
        d["k2"] = d["k"] * (1.0 + (d["al"] - 1.0) * pg[6:7])
        d["ssq"] = seg(d["kkv"] * d["kkv"])
        d["L"] = _cumsum_rows(tri, d["ld"])
        d["bsum"] = seg(d["r"] * d["k2"] * pg[7:8])

    def wide3(g):
        d = st[g]
        kkn = d["kkv"] / jnp.maximum(jnp.sqrt(d["ssq"]), 1e-12)
        d["bonus"] = d["bsum"] * d["v"]
        L, ld = d["L"], d["ld"]
        if C >= 32:
            mid = [L[s * C + C // 2 - 1:s * C + C // 2] for s in range(nsb)]
            lref = mid[0]
            for s in range(1, nsb):
                lref = jnp.where(row >= s * C, mid[s], lref)
        else:
            mid = [jnp.zeros((1, GW), F32)] * nsb
            lref = mid[0]
        e_fwd = jnp.exp(L - lref)
        e_bwd = jnp.exp(lref - L)
        gs = slice(g * GW, (g + 1) * GW)
        at_ref[:, gs] = -kkn * jnp.exp(L - ld - lref)
        rt_ref[:, gs] = d["r"] * e_fwd
        bt_ref[:, gs] = jnp.where(valid, kkn * d["al"], 0.0) * e_bwd
        kt_ref[:, gs] = jnp.where(valid, d["k2"], 0.0) * e_bwd
        vm_ref[:, gs] = jnp.where(valid, d["v"], 0.0)
        d["e_mid"] = [jnp.exp(m) for m in mid]
        d["e_last"] = [e_fwd[s * C + C - 1:s * C + C] for s in range(nsb)]
        d["p_last"] = [jnp.exp(L[s * C + C - 1:s * C + C]) for s in range(nsb)]
        for key in ("wl", "alp", "ld", "al", "kkv", "k2", "ssq", "L", "bsum", "k", "r", "v"):
            del d[key]

    def sl(ref, s, q):
        return ref[s * C:(s + 1) * C, q * QL:(q + 1) * QL]

    def blockdiag(x, masks):
        return jnp.concatenate([jnp.where(m, x, jnp.zeros_like(x)) for m in masks], axis=0)

    def units_of(g):
        return [(s, g * GQ + j) for s in range(nsb) for j in range(GQ)]

    unit_keys = ("lhs_ar", "rhs_bk", "a_ab", "a_ak", "a_rb", "a_rk", "inv", "apow", "x2", "av2", "vm", "uu")

    def unit_g(g):
        d = st[g]
        for name in unit_keys:
            d[name] = {}
        for u in units_of(g):
            s, q = u
            d["lhs_ar"][u] = jnp.concatenate([sl(at_ref, s, q), sl(rt_ref, s, q)], axis=0).astype(BF16)
            bt = sl(bt_ref, s, q).astype(BF16)
            kt = sl(kt_ref, s, q).astype(BF16)
            d["rhs_bk"][u] = jnp.concatenate([bt, kt], axis=0)
            gm = _dot(d["lhs_ar"][u],
                      jnp.concatenate([blockdiag(bt, head_c), blockdiag(kt, head_c)], axis=0), NT)
            d["a_ab"][u] = jnp.where(strict, gm[:C, :SW], 0.0)
            d["a_ak"][u] = jnp.where(strict, gm[:C, SW:], 0.0)
            d["a_rb"][u] = jnp.where(incl, gm[C:, :SW], 0.0)
            d["a_rk"][u] = jnp.where(incl, gm[C:, SW:], 0.0)
            d["inv"][u] = eye + d["a_ab"][u]

    def unit_p1(g):
        d = st[g]
        for u in units_of(g):
            pw = d["a_ab"][u].astype(BF16)
            d["apow"][u] = _dot(pw, blockdiag(pw, head_i))

    def unit_level(g, lvl):
        d = st[g]
        for u in units_of(g):
            pw = d["apow"][u].astype(BF16)
            rhs = blockdiag(pw, head_i)
            if lvl < n_dbl - 1:
                both = _dot(jnp.concatenate([pw, d["inv"][u].astype(BF16)], axis=0), rhs)
                d["apow"][u] = both[:C]
                d["inv"][u] = d["inv"][u] + both[C:]
            else:
                d["inv"][u] = d["inv"][u] + _dot(d["inv"][u].astype(BF16), rhs)

    def unit_x2(g):
        d = st[g]
        for u in units_of(g):
            s, q = u
            lo = (q - g * GQ) * QL
            bdm = bd_ref[s * NQ + q] * d["e_mid"][s][:, lo:lo + QL]
            d["x2"][u] = _dot(d["lhs_ar"][u], bdm.astype(BF16), NT)

    def unit_av2(g):
        d = st[g]
        for u in units_of(g):
            s, q = u
            d["vm"][u] = sl(vm_ref, s, q).astype(BF16)
            d["av2"][u] = _dot(jnp.concatenate([d["a_ak"][u], d["a_rk"][u]], axis=0).astype(BF16),
                               blockdiag(d["vm"][u], head_c))

    def unit_u(g):
        d = st[g]
        for u in units_of(g):
            w1a = (d["x2"][u][:C] + d["av2"][u][:C]).astype(BF16)
            d["uu"][u] = _dot(d["inv"][u].astype(BF16), blockdiag(w1a, head_c)).astype(BF16)

    def unit_y(g):
        d = st[g]
        for u in units_of(g):
            s, q = u
            y_ref[s * C:(s + 1) * C, q * QL:(q + 1) * QL] = (
                d["x2"][u][C:] + d["av2"][u][C:] + _dot(d["a_rb"][u].astype(BF16), blockdiag(d["uu"][u], head_c)))

    def unit_upd(g):
        d = st[g]
        for u in units_of(g):
            s, q = u
            lo = (q - g * GQ) * QL
            upd = _dot(jnp.concatenate([d["uu"][u], d["vm"][u]], axis=0), d["rhs_bk"][u], TN)
            bd = bd_ref[s * NQ + q] * d["p_last"][s][:, lo:lo + QL] + upd * d["e_last"][s][:, lo:lo + QL]
            bd_ref[s * NQ + q] = jnp.where(bd_mask, bd, 0.0)
        for name in unit_keys:
            del d[name]

    def post1(g):
        d = st[g]
        d["y"] = y_ref[:, g * GW:(g + 1) * GW]
        d["mean"] = seg(d["y"]) * (1.0 / H)

    def post2(g):
        d = st[g]
        d["yc"] = d["y"] - d["mean"]
        d["var"] = seg(d["yc"] * d["yc"]) * (1.0 / H)

    def post3(g):
        d = st[g]
        pg = d["par"]
        out = ((d["yc"] * lax.rsqrt(d["var"] + LNX_EPS) * pg[8:9] + pg[9:10] + d["bonus"])
               * _silu(wide(za_ref, g * GP, GP)))
        for p in range(GP):
            oa_ref[g * GP + p] = out[:, p * LANE:(p + 1) * LANE]

    unit_stages = ([unit_g, unit_p1] + [functools.partial(unit_level, lvl=i) for i in range(n_dbl)]
                   + [unit_x2, unit_av2, unit_u, unit_y, unit_upd])
    for f in (wide1, wide2, wide3):
        f(0)
    for g in range(NG):
        side = []
        if g + 1 < NG:
            side += [(wide1, g + 1), (wide2, g + 1), (wide3, g + 1)]
        if g > 0:
            side += [(post1, g - 1), (post2, g - 1), (post3, g - 1)]
        every = max(1, len(unit_stages) // max(1, len(side)))
        for j, stage in enumerate(unit_stages):
            stage(g)
            if side and (j + 1) % every == 0:
                f, gg = side.pop(0)
                f(gg)
        for f, gg in side:
            f(gg)
    for f in (post1, post2, post3):
        f(NG - 1)

    if nchunk > 1:
        for s in range(nsb):
            last = s * C + C - 1
            for j in range(3 * NPAIR):
                prev_ref[s, j:j + 1, :] = zr_ref[j, last:last + 1, :]
            prev_ref[s, 3 * NPAIR:3 * NPAIR + 1, :] = wl_raw[last:last + 1]
            prev_ref[s, 3 * NPAIR + 1:3 * NPAIR + 2, :] = al_raw[last:last + 1]

    @pl.when(c == nchunk - 1)
    def _fin():
        for s, q in units:
            bd = bd_ref[s * NQ + q]
            for h in range(NH):
                sout_ref[0, s, NH * q + h] = bd[h * H:(h + 1) * H, h * H:(h + 1) * H]
        shift_ref[...] = jnp.zeros(shift_ref.shape, F32)
        for s in range(nsb):
            last = s * C + (valid_hi - 1) - (nchunk - 1) * C
            for j in range(3 * NPAIR):
                shift_ref[s, j:j + 1, :] = zr_ref[j, last:last + 1, :]
            shift_ref[s, 3 * NPAIR:3 * NPAIR + 1, :] = wl_raw[last:last + 1]
            shift_ref[s, 3 * NPAIR + 1:3 * NPAIR + 2, :] = al_raw[last:last + 1]


def _rwkv(z3, par, mulo, w2, a2, buf, *, state, prev_oa, prev_state, layer, depth, nseq, nsb, nchunk, ngroup, C,
          row0, valid_lo, valid_hi):
    R = nsb * C
    rb = lambda n, c: row0 + c * (nseq // nsb) + n
    state_block = (1, nsb, A_HEADS, A_HEAD, A_HEAD)
    state_map = lambda n, c: (layer, n, 0, 0, 0)
    in_specs = [
        pl.BlockSpec((3 * NPAIR, R, LANE), lambda n, c: (0, rb(n, c), 0)),
        pl.BlockSpec((NPAIR, R, LANE), lambda n, c: (SL_ZA // NPAIR, rb(n, c), 0)),
        pl.BlockSpec((2, R, LANE), lambda n, c: (SL_WLO // 2, rb(n, c), 0)),
        pl.BlockSpec((N_PAR_ROWS, D_MODEL), lambda n, c: (0, 0)),
        pl.BlockSpec((8, LANE), lambda n, c: (0, 0)),
        pl.BlockSpec((LANE, D_MODEL), lambda n, c: (0, 0)),
        pl.BlockSpec((LANE, D_MODEL), lambda n, c: (0, 0)),
        pl.BlockSpec((nsb, N_BUF_ROWS, LANE), lambda n, c: (n, 0, 0)),
    ]
    args = [z3, z3, z3, par, mulo, w2, a2, buf]
    if state is not None:
        in_specs.append(pl.BlockSpec(state_block, state_map))
        args.append(state)
    aliases = {}
    for out_idx, prev in enumerate((prev_oa, prev_state)):
        if prev is not None:
            aliases[len(args)] = out_idx
            in_specs.append(pl.BlockSpec(memory_space=pl.ANY))
            args.append(prev)
    kern = functools.partial(_rwkv_kernel, C=C, nsb=nsb, nchunk=nchunk, ngroup=ngroup, valid_lo=valid_lo,
                             valid_hi=valid_hi, zero_init=state is None, n_alias=len(aliases))
    wide_scratch = pltpu.VMEM((R, D_MODEL), F32)
    unit_lanes = RWKV_UNIT_HEADS * A_HEAD
    return pl.pallas_call(
        kern,
        grid=(nseq // nsb, nchunk),
        in_specs=in_specs,
        out_specs=[
            pl.BlockSpec((NPAIR, R, LANE), lambda n, c: (0, rb(n, c), 0)),
            pl.BlockSpec(state_block, state_map),
            pl.BlockSpec((nsb, N_BUF_ROWS, LANE), lambda n, c: (n, 0, 0)),
        ],
        out_shape=[
            jax.ShapeDtypeStruct((NPAIR, z3.shape[1], LANE), F32),
            jax.ShapeDtypeStruct((depth, nseq, A_HEADS, A_HEAD, A_HEAD), F32),
            jax.ShapeDtypeStruct((nseq, N_BUF_ROWS, LANE), F32),
        ],
        input_output_aliases=aliases,
        scratch_shapes=[
            pltpu.VMEM((nsb * D_MODEL // unit_lanes, unit_lanes, unit_lanes), F32),
            pltpu.VMEM((nsb, N_BUF_ROWS, LANE), F32),
        ] + [wide_scratch] * 6,
        compiler_params=pltpu.CompilerParams(
            dimension_semantics=("arbitrary", "arbitrary"), vmem_limit_bytes=VMEM_LIMIT),
        name="rwkv_c%d" % C,
    )(*args)


def _gla_kernel(*refs, C, SB, nsb, nchunk, valid_lo, valid_hi, zero_init, n_alias):
    it = iter(refs)
    q_ref, k_ref, v_ref, zb_ref, gk_ref, gwh_ref, gwl_ref, nw_ref = (next(it) for _ in range(8))
    sin_ref = None if zero_init else next(it)
    for _ in range(n_alias):
        next(it)
    ob_ref, sout_ref, s_ref, att_ref = it
    c = pl.program_id(1)
    R = nsb * C
    units = [(s, h) for s in range(nsb) for h in range(B_HEADS)]
    QS = B_HK // LANE
    VS = B_HV // LANE

    def slot(u):
        return u[0] * B_HEADS + u[1]

    @pl.when(c == 0)
    def _init():
        for u in units:
            s_ref[slot(u)] = jnp.zeros((B_HK, B_HV), F32) if zero_init else sin_ref[0, u[0], u[1]]

    row = _iota((R, 1), 0)
    pos = c * C + (row & (C - 1))
    valid = (pos >= valid_lo) & (pos < valid_hi)

    glo1 = jnp.where(_iota((R, LANE), 1) == GATE_LORA, 1.0, gk_ref[0])
    g = jnp.where(valid, _log_sigmoid(_mm3(glo1, gwh_ref[...], gwl_ref[...])) / GATE_TAU, 0.0)
    ti = _iota((R, R), 0)
    tj = _iota((R, R), 1)
    tri = jnp.where((tj <= ti) & ((ti & -C) == (tj & -C)), 1.0, 0.0).astype(BF16)
    parts = _split2(g)
    b = _dot(tri, parts[0]) + _dot(tri, parts[1])
    eb = jnp.exp(b)
    b_col = []
    for s in range(nsb):
        ones_s = jnp.where((row >= s * C) & (row < (s + 1) * C), 1.0, 0.0).astype(BF16) * jnp.ones((1, LANE), BF16)
        b_col.append(_dot(parts[0], ones_s, TN) + _dot(parts[1], ones_s, TN))

    def part(x, u):
        s, h = u
        return x[s * C:(s + 1) * C, h * B_HK:(h + 1) * B_HK]

    def gather(ref, u, n):
        s, h = u
        return jnp.concatenate([ref[n * h + j, s * C:(s + 1) * C, :] for j in range(n)], axis=1)

    vrow = {u: valid[u[0] * C:(u[0] + 1) * C] for u in units}
    q = {u: gather(q_ref, u, QS) * (B_HK ** -0.5) for u in units}
    k = {u: jnp.where(vrow[u], gather(k_ref, u, QS), 0.0) for u in units}
    v = {u: jnp.where(vrow[u], gather(v_ref, u, VS), 0.0).astype(BF16) for u in units}
    bu = {u: part(b, u) for u in units}

    o = {u: _mm(q[u] * part(eb, u), s_ref[slot(u)]) for u in units}

    att_ref[...] = jnp.zeros(att_ref.shape, F32)
    nb = C // SB
    for i in range(1, nb):
        lo, hi = SB * i, SB * (i + 1)
        for u in units:
            beta = bu[u][lo - 1:lo]
            qi = q[u][lo:hi] * jnp.exp(bu[u][lo:hi] - beta)
            kj = k[u][:lo] * jnp.exp(beta - bu[u][:lo])
            att_ref[slot(u), :lo, lo:hi] = _mm(kj, qi, NT)

    srow = _iota((SB, 1), 0)

    def _diag_matmul():
        causal = srow <= _iota((SB, SB), 1)
        for i in range(nb):
            lo = SB * i
            for u in units:
                bb = bu[u][lo:lo + SB]
                qh = q[u][lo:lo + SB] * jnp.exp(bb - bb[0:1])
                kh = k[u][lo:lo + SB] * jnp.exp(bb[0:1] - bb)
                att_ref[slot(u), lo:lo + SB, lo:lo + SB] = jnp.where(causal, _mm(kh, qh, NT), 0.0)

    def _diag_exact():
        for i in range(nb):
            lo = SB * i
            for u in units:
                bb = bu[u][lo:lo + SB]
                kb = k[u][lo:lo + SB]
                for t in range(SB):
                    e = jnp.exp(jnp.where(srow <= t, bb[t:t + 1] - bb, -jnp.inf))
                    att_ref[slot(u), lo:lo + SB, lo + t:lo + t + 1] = jnp.sum(
                        q[u][lo + t:lo + t + 1] * kb * e, axis=-1, keepdims=True)

    if nb == 1:
        _diag_exact()
    else:
        span = jnp.max(jnp.concatenate(
            [b[s * C + SB * i:s * C + SB * i + 1] - b[s * C + SB * i + SB - 1:s * C + SB * i + SB]
             for s in range(nsb) for i in range(nb)], axis=0))
        pl.when(span < GLA_FACTOR_MAX_DECAY)(_diag_matmul)
        pl.when(span >= GLA_FACTOR_MAX_DECAY)(_diag_exact)

    for u in units:
        o[u] = o[u] + _dot(att_ref[slot(u)].astype(BF16), v[u], TN)

    for u in units:
        s, h = u
        ms = jnp.mean(o[u] * o[u], axis=-1, keepdims=True)
        on = o[u] * lax.rsqrt(ms + NORM_EPS) * nw_ref[...]
        for j in range(VS):
            ob_ref[VS * h + j, s * C:(s + 1) * C, :] = (
                on[:, j * LANE:(j + 1) * LANE] * _silu(zb_ref[VS * h + j, s * C:(s + 1) * C, :]))

    for u in units:
        s, h = u
        decay = jnp.exp(b_col[s][h * B_HK:(h + 1) * B_HK])
        decay = jnp.concatenate([decay] * VS, axis=1)
        kd = k[u] * jnp.exp(bu[u][C - 1:C] - bu[u])
        s_ref[slot(u)] = s_ref[slot(u)] * decay + _dot(kd.astype(BF16), v[u], TN)

    @pl.when(c == nchunk - 1)
    def _fin():
        for u in units:
            sout_ref[0, u[0], u[1]] = s_ref[slot(u)]


def _gla(z3, gw, nw, *, state, prev_ob, prev_state, layer, depth, nseq, nsb, nchunk, C, SB, row0, valid_lo,
         valid_hi):
    R = nsb * C
    rb = lambda n, c: row0 + c * (nseq // nsb) + n
    state_block = (1, nsb, B_HEADS, B_HK, B_HV)
    state_map = lambda n, c: (layer, n, 0, 0, 0)
    qs = B_HEADS * B_HK // LANE
    vs = B_HEADS * B_HV // LANE
    in_specs = [
        pl.BlockSpec((qs, R, LANE), lambda n, c: (SL_QB // qs, rb(n, c), 0)),
        pl.BlockSpec((qs, R, LANE), lambda n, c: (SL_KB // qs, rb(n, c), 0)),
        pl.BlockSpec((vs, R, LANE), lambda n, c: (SL_VB // vs, rb(n, c), 0)),
        pl.BlockSpec((vs, R, LANE), lambda n, c: (SL_ZB // vs, rb(n, c), 0)),
        pl.BlockSpec((1, R, LANE), lambda n, c: (SL_GK, rb(n, c), 0)),
        pl.BlockSpec((LANE, B_HEADS * B_HK), lambda n, c: (0, 0)),
        pl.BlockSpec((LANE, B_HEADS * B_HK), lambda n, c: (0, 0)),
        pl.BlockSpec((1, B_HV), lambda n, c: (0, 0)),
    ]
    args = [z3, z3, z3, z3, z3, *gw, nw]
    if state is not None:
        in_specs.append(pl.BlockSpec(state_block, state_map))
        args.append(state)
    aliases = {}
    for out_idx, prev in enumerate((prev_ob, prev_state)):
        if prev is not None:
            aliases[len(args)] = out_idx
            in_specs.append(pl.BlockSpec(memory_space=pl.ANY))
            args.append(prev)
    kern = functools.partial(_gla_kernel, C=C, SB=SB, nsb=nsb, nchunk=nchunk, valid_lo=valid_lo, valid_hi=valid_hi,
                             zero_init=state is None, n_alias=len(aliases))
    return pl.pallas_call(
        kern,
        grid=(nseq // nsb, nchunk),
        in_specs=in_specs,
        out_specs=[
            pl.BlockSpec((vs, R, LANE), lambda n, c: (0, rb(n, c), 0)),
            pl.BlockSpec(state_block, state_map),
        ],
        out_shape=[
            jax.ShapeDtypeStruct((NPAIR, z3.shape[1], LANE), F32),
            jax.ShapeDtypeStruct((depth, nseq, B_HEADS, B_HK, B_HV), F32),
        ],
        input_output_aliases=aliases,
        scratch_shapes=[
            pltpu.VMEM((nsb * B_HEADS, B_HK, B_HV), F32),
            pltpu.VMEM((nsb * B_HEADS, C, C), F32),
        ],
        compiler_params=pltpu.CompilerParams(
            dimension_semantics=("arbitrary", "arbitrary"), vmem_limit_bytes=VMEM_LIMIT),
        name="gla_c%d" % C,
    )(*args)


def _outproj_kernel(oa_ref, ob_ref, ga_ref, gb_ref, x_ref, w_ref, g_ref, o_ref, m_ref):
    for q in range(NPAIR):
        m = _sigmoid(ga_ref[q]) * oa_ref[q] + _sigmoid(gb_ref[q]) * ob_ref[q]
        m_ref[:, q * LANE:(q + 1) * LANE] = m.astype(BF16)
    y = jnp.dot(m_ref[...], w_ref[...], preferred_element_type=F32)
    ms = jnp.mean(y * y, axis=-1, keepdims=True)
    o_ref[...] = x_ref[...] + y * lax.rsqrt(ms + NORM_EPS) * g_ref[...]


def _outproj(oa3, ob3, z3, x, w, g, tile_m):
    m = x.shape[0]
    return pl.pallas_call(
        _outproj_kernel,
        grid=(m // tile_m,),
        in_specs=[
            pl.BlockSpec((NPAIR, tile_m, LANE), lambda i: (0, i, 0)),
            pl.BlockSpec((NPAIR, tile_m, LANE), lambda i: (0, i, 0)),
            pl.BlockSpec((NPAIR, tile_m, LANE), lambda i: (SL_GA // NPAIR, i, 0)),
            pl.BlockSpec((NPAIR, tile_m, LANE), lambda i: (SL_GB // NPAIR, i, 0)),
            pl.BlockSpec((tile_m, D_MODEL), lambda i: (i, 0)),
            pl.BlockSpec((D_MODEL, D_MODEL), lambda i: (0, 0)),
            pl.BlockSpec((1, D_MODEL), lambda i: (0, 0)),
        ],
        out_specs=pl.BlockSpec((tile_m, D_MODEL), lambda i: (i, 0)),
        out_shape=jax.ShapeDtypeStruct((m, D_MODEL), F32),
        scratch_shapes=[pltpu.VMEM((tile_m, D_MODEL), BF16)],
        compiler_params=pltpu.CompilerParams(
            dimension_semantics=("arbitrary",), vmem_limit_bytes=VMEM_LIMIT),
        name="outproj",
    )(oa3, ob3, z3, z3, x, w, g)


def _pick_tile(m, cap, mult):
    best = mult
    for t in range(mult, cap + 1, mult):
        if m % t == 0:
            best = t
    assert m % best == 0
    return best


def _pad_lanes(x):
    return jnp.pad(x, [(0, 0)] * (x.ndim - 1) + [(0, LANE - x.shape[-1])])


def _shift_rows(buf):
    n = buf.shape[0]
    rkv = buf[:, :3 * NPAIR].reshape(n, 3 * D_MODEL)
    return jnp.concatenate([rkv, buf[:, 3 * NPAIR, :LORA], buf[:, 3 * NPAIR + 1, :LORA]], axis=1)


def kernel(x_prompt, x_sample, state_rwkv, state_shift, state_gla, meta_tokens, norm_pre, w_in, rwkv_mu, rwkv_w0, rwkv_w2, rwkv_a0, rwkv_a2, rwkv_k_k, rwkv_k_a, rwkv_r_k, rwkv_lnx_w, rwkv_lnx_b, gla_gk_w2, gla_gk_b, gla_norm_w, w_out, norm_post):
    depth = w_in.shape[0]
    bp, seq, _ = x_prompt.shape
    bs, dseq, _ = x_sample.shape
    front = CHUNK - N_META
    tp = front + N_META + seq
    assert tp % CHUNK == 0 and dseq <= SAMPLE_ROWS and bs % RWKV_SAMPLE_SEQS == 0
    nchunk = tp // CHUNK
    mp = bp * tp
    ms = bs * SAMPLE_ROWS
    mtot = mp + ms
    tile_in = _pick_tile(mtot, 1184, 16)
    tile_out = _pick_tile(mtot, 296, 8)

    meta = jnp.broadcast_to(meta_tokens[None], (bp, N_META, D_MODEL))
    xp = jnp.concatenate([jnp.zeros((bp, front, D_MODEL), F32), meta, x_prompt], axis=1)
    xp = xp.reshape(bp, nchunk, CHUNK, D_MODEL).transpose(1, 0, 2, 3).reshape(mp, D_MODEL)
    xs = jnp.pad(x_sample, ((0, 0), (0, SAMPLE_ROWS - dseq), (0, 0))).reshape(ms, D_MODEL)
    x = jnp.concatenate([xp, xs], axis=0)

    w_packed = _pack_w_in(w_in)
    w_out_bf = w_out.astype(BF16)
    zero_buf = jnp.zeros((bp, N_BUF_ROWS, LANE), F32)

    sr_p = sr_s = sg_p = sg_s = None
    shift_p, shift_s = [], []
    for l in range(depth):
        mu = rwkv_mu[l]
        par = jnp.stack([mu[:D_MODEL], mu[D_MODEL:2 * D_MODEL], mu[2 * D_MODEL:3 * D_MODEL],
                         rwkv_w0[l], rwkv_a0[l], rwkv_k_k[l], rwkv_k_a[l], rwkv_r_k[l].reshape(-1),
                         rwkv_lnx_w[l], rwkv_lnx_b[l]], axis=0)
        par = jnp.pad(par, ((0, N_PAR_ROWS - par.shape[0]), (0, 0)))
        mulo = jnp.pad(jnp.stack([_pad_lanes(mu[3 * D_MODEL:3 * D_MODEL + LORA]),
                                  _pad_lanes(mu[3 * D_MODEL + LORA:])], axis=0), ((0, 6), (0, 0)))
        w2 = jnp.pad(rwkv_w2[l], ((0, LANE - LORA), (0, 0))).astype(BF16)
        a2 = jnp.pad(rwkv_a2[l], ((0, LANE - LORA), (0, 0))).astype(BF16)
        gw = jnp.concatenate([gla_gk_w2[l], gla_gk_b[l][None]], axis=0)
        gw = _split2(jnp.pad(gw, ((0, LANE - gw.shape[0]), (0, 0))))
        nw = gla_norm_w[l][None]
        sh = state_shift[l]
        buf_s = jnp.concatenate([sh[:, :3 * D_MODEL].reshape(bs, 3 * NPAIR, LANE),
                                 _pad_lanes(sh[:, 3 * D_MODEL:3 * D_MODEL + LORA])[:, None],
                                 _pad_lanes(sh[:, 3 * D_MODEL + LORA:])[:, None]], axis=1)
        buf_s = jnp.pad(buf_s, ((0, 0), (0, N_BUF_ROWS - buf_s.shape[1]), (0, 0)))

        z3 = _inproj(x, norm_pre[l][None], w_packed, l, tile_in)

        prompt = dict(layer=l, depth=depth, nseq=bp, nchunk=nchunk, C=CHUNK, row0=0, valid_lo=front, valid_hi=tp)
        sample = dict(layer=l, depth=depth, nseq=bs, nchunk=1, C=SAMPLE_ROWS, valid_lo=0, valid_hi=dseq)
        oa, sr_p, sh_p = _rwkv(z3, par, mulo, w2, a2, zero_buf, state=None, prev_oa=None, prev_state=sr_p,
                               nsb=RWKV_PROMPT_SEQS, ngroup=RWKV_LANE_GROUPS, **prompt)
        oa, sr_s, sh_s = _rwkv(z3, par, mulo, w2, a2, buf_s, state=state_rwkv, prev_oa=oa, prev_state=sr_s,
                               nsb=RWKV_SAMPLE_SEQS, ngroup=1, row0=mp // (SAMPLE_ROWS * RWKV_SAMPLE_SEQS),
                               **sample)
        ob, sg_p = _gla(z3, gw, nw, state=None, prev_ob=None, prev_state=sg_p, SB=GLA_SUB, nsb=1, **prompt)
        ob, sg_s = _gla(z3, gw, nw, state=state_gla, prev_ob=ob, prev_state=sg_s, SB=SAMPLE_ROWS,
                        nsb=GLA_SAMPLE_SEQS, row0=mp // (SAMPLE_ROWS * GLA_SAMPLE_SEQS), **sample)

        x = _outproj(oa, ob, z3, x, w_out_bf[l], norm_post[l][None], tile_out)
        shift_p.append(_shift_rows(sh_p))
        shift_s.append(_shift_rows(sh_s))

    y_prompt = x[:mp].reshape(nchunk, bp, CHUNK, D_MODEL).transpose(1, 0, 2, 3).reshape(bp, tp, D_MODEL)
    y_prompt = y_prompt[:, front + N_META:]
    y_sample = x[mp:].reshape(bs, SAMPLE_ROWS, D_MODEL)[:, :dseq]
    return (y_prompt, y_sample, sr_p, jnp.stack(shift_p), sg_p, sr_s, jnp.stack(shift_s), sg_s)
```

```python
import functools
import math

import jax
import jax.numpy as jnp
from jax import lax
from jax.experimental import pallas as pl
from jax.experimental.pallas import tpu as pltpu

F32 = jnp.float32
BF16 = jnp.bfloat16

D_MODEL = 2048
N_META = 16
NORM_EPS = 1e-6
LNX_EPS = 64e-5
A_HEAD = 64
A_HEADS = D_MODEL // A_HEAD
LORA = 96
B_HEADS = 4
B_HK = 256
B_HV = 512
GATE_LORA = 16
GATE_TAU = 16.0
CHUNK = 64
SAMPLE_ROWS = 8
GLA_SUB = 16
GLA_FACTOR_MAX_DECAY = 60.0
RWKV_SAMPLE_SEQS = 4
GLA_SAMPLE_SEQS = 2
GLA_PROMPT_SEQS = 2
RWKV_PROMPT_SEQS = 2
RWKV_UNIT_HEADS = 4
RWKV_LANE_GROUPS = 2

LANE = 128
NPAIR = D_MODEL // LANE
SLABS_PER_TILE = 8
TILE_N = SLABS_PER_TILE * LANE
SL_R, SL_K, SL_V, SL_ZA = 0, 16, 32, 48
SL_QB, SL_KB, SL_VB, SL_ZB = 64, 72, 80, 96
SL_GA, SL_GB = 112, 128
SL_WLO, SL_ALO, SL_GK = 144, 145, 146
NSLAB = 147
NCOLP = NSLAB * LANE
N_BUF_ROWS = 56
N_PAR_ROWS = 16
VMEM_LIMIT = 56 * 1024 * 1024

NN = ((1,), (0,))
NT = ((1,), (1,))
TN = ((0,), (0,))


def _dot(a, b, dims=NN):
    return lax.dot_general(a, b, (dims, ((), ())), preferred_element_type=F32)


def _mm(a, b, dims=NN):
    return _dot(a.astype(BF16), b.astype(BF16), dims)


def _split2(x):
    hi = x.astype(BF16)
    lo = (x - hi.astype(F32)).astype(BF16)
    return hi, lo


def _mm3(a, b_hi, b_lo, dims=NN):
    ah, al = _split2(a)
    return _dot(ah, b_hi, dims) + _dot(al, b_hi, dims) + _dot(ah, b_lo, dims)


def _cumsum_rows(tri, x):
    hi, lo = _split2(x)
    return _dot(tri, hi) + _dot(tri, lo)


def _sigmoid(x):
    return 1.0 / (1.0 + jnp.exp(-x))


def _silu(x):
    return x * _sigmoid(x)


def _log_sigmoid(x):
    return jnp.minimum(x, 0.0) - jnp.log(1.0 + jnp.exp(-jnp.abs(x)))


def _iota(shape, dim):
    return lax.broadcasted_iota(jnp.int32, shape, dim)


W_RKV = 3 * D_MODEL
W_SHIFT = W_RKV + 2 * LORA
W_MID = 3 * D_MODEL
W_GK = W_SHIFT + W_MID
W_TAIL = 3 * D_MODEL


PACK_TILE = 512
W_MAIN = W_RKV + W_MID + W_TAIL
N_SMALL = NCOLP - W_MAIN


def _pack_kernel(w_ref, o_ref):
    o_ref[0] = w_ref[0].astype(BF16)


def _pack_w_in(w_in):
    depth, d, ncol = w_in.shape
    assert ncol == W_GK + GATE_LORA + W_TAIL and d == D_MODEL and N_SMALL <= TILE_N
    wt = jnp.swapaxes(w_in, 1, 2)
    small = jnp.concatenate([jnp.pad(wt[:, a:a + n], ((0, 0), (0, LANE - n), (0, 0)))
                             for a, n in ((W_RKV, LORA), (W_RKV + LORA, LORA), (W_GK, GATE_LORA))], axis=1)
    small = lax.optimization_barrier(jnp.pad(small, ((0, 0), (0, N_SMALL - small.shape[1]), (0, 0)))).astype(BF16)
    g1 = W_RKV // PACK_TILE
    g2 = (W_RKV + W_MID) // PACK_TILE
    skip1 = W_SHIFT - W_RKV
    skip2 = W_GK + GATE_LORA - (W_RKV + W_MID) - skip1

    def src_row(i):
        row = i * PACK_TILE + (i >= g1).astype(jnp.int32) * skip1 + (i >= g2).astype(jnp.int32) * skip2
        return pl.multiple_of(row, math.gcd(PACK_TILE, skip1, skip2))

    main = pl.pallas_call(
        _pack_kernel,
        grid=(depth, W_MAIN // PACK_TILE),
        in_specs=[pl.BlockSpec((pl.Element(1), pl.Element(PACK_TILE), pl.Element(D_MODEL)),
                               lambda l, i: (l, src_row(i), 0))],
        out_specs=pl.BlockSpec((1, PACK_TILE, D_MODEL), lambda l, i: (l, i, 0)),
        out_shape=jax.ShapeDtypeStruct((depth, W_MAIN, D_MODEL), BF16),
        compiler_params=pltpu.CompilerParams(
            dimension_semantics=("arbitrary", "arbitrary"), vmem_limit_bytes=VMEM_LIMIT),
        name="pack_w_in",
    )(wt)
    return main, small


def _inproj_kernel(x_ref, g_ref, w_ref, ws_ref, o_ref, h_ref):
    j = pl.program_id(1)

    @pl.when(j == 0)
    def _():
        x = x_ref[...]
        ms = jnp.mean(x * x, axis=-1, keepdims=True)
        h_ref[...] = (x * lax.rsqrt(ms + NORM_EPS) * g_ref[...]).astype(BF16)

    def emit(w):
        acc = _dot(h_ref[...], w, NT)
        for q in range(w.shape[0] // LANE):
            o_ref[q] = acc[:, q * LANE:(q + 1) * LANE]

    @pl.when(j < W_MAIN // TILE_N)
    def _main():
        emit(w_ref[0])

    @pl.when(j >= W_MAIN // TILE_N)
    def _small():
        emit(ws_ref[0])


def _inproj(x, g, w, layer, tile_m):
    m = x.shape[0]
    w_main, w_small = w
    n_main = W_MAIN // TILE_N
    return pl.pallas_call(
        _inproj_kernel,
        grid=(m // tile_m, n_main + 1),
        in_specs=[
            pl.BlockSpec((tile_m, D_MODEL), lambda i, j: (i, 0)),
            pl.BlockSpec((1, D_MODEL), lambda i, j: (0, 0)),
            pl.BlockSpec((1, TILE_N, D_MODEL), lambda i, j: (layer, jnp.minimum(j, n_main - 1), 0)),
            pl.BlockSpec((1, N_SMALL, D_MODEL), lambda i, j: (layer, 0, 0)),
        ],
        out_specs=pl.BlockSpec((SLABS_PER_TILE, tile_m, LANE), lambda i, j: (j, i, 0)),
        out_shape=jax.ShapeDtypeStruct((NSLAB, m, LANE), F32),
        scratch_shapes=[pltpu.VMEM((tile_m, D_MODEL), BF16)],
        compiler_params=pltpu.CompilerParams(
            dimension_semantics=("arbitrary", "arbitrary"), vmem_limit_bytes=VMEM_LIMIT),
        name="inproj",
    )(x, g, w_main, w_small)


def _rwkv_kernel(*refs, C, nsb, nchunk, ngroup, valid_lo, valid_hi, zero_init, n_alias):
    it = iter(refs)
    zr_ref, za_ref, lo_ref, par_ref, mulo_ref, w2_ref, a2_ref, buf_ref = (next(it) for _ in range(8))
    sin_ref = None if zero_init else next(it)
    for _ in range(n_alias):
        next(it)
    oa_ref, sout_ref, shift_ref, bd_ref, prev_ref, at_ref, rt_ref, bt_ref, kt_ref, vm_ref, y_ref = it
    c = pl.program_id(1)
    H = A_HEAD
    R = nsb * C
    W = D_MODEL
    NH = RWKV_UNIT_HEADS
    NG = ngroup
    QL = NH * H
    NQ = W // QL
    units = [(s, q) for s in range(nsb) for q in range(NQ)]

    @pl.when(c == 0)
    def _init():
        prev_ref[...] = buf_ref[...]
        for s, q in units:
            if zero_init:
                bd_ref[s * NQ + q] = jnp.zeros((QL, QL), F32)
            else:
                blocks = []
                for h in range(NH):
                    pieces = [jnp.zeros((H, H), F32)] * NH
                    pieces[h] = sin_ref[0, s, NH * q + h]
                    blocks.append(jnp.concatenate(pieces, axis=1))
                bd_ref[s * NQ + q] = jnp.concatenate(blocks, axis=0)

    row = _iota((R, 1), 0)
    rin = row & (C - 1)
    pos = c * C + rin
    valid = (pos >= valid_lo) & (pos < valid_hi)

    def wide(ref, n0, n):
        return jnp.concatenate([ref[n0 + j] for j in range(n)], axis=1)

    def rowcat(ref, s, n0, n):
        return jnp.concatenate([ref[s, n0 + j:n0 + j + 1, :] for j in range(n)], axis=1)

    SUB = 8
    row8 = _iota((SUB, 1), 0)

    def prev_rows(x, n0, n):
        rolled = pltpu.roll(x, 1, axis=0)
        patches = {}
        for s in range(nsb):
            t0 = s * C
            patches[t0] = jnp.where(row8 == 0, rowcat(prev_ref, s, n0, n), rolled[t0:t0 + SUB])
            if valid_lo > 0:
                t1 = s * C + (valid_lo % C) // SUB * SUB
                hit = (c == valid_lo // C) & (row8 == valid_lo % SUB)
                patches[t1] = jnp.where(hit, rowcat(buf_ref, s, n0, n), patches.get(t1, rolled[t1:t1 + SUB]))
        pieces, t = [], 0
        for t0 in sorted(patches):
            if t0 > t:
                pieces.append(rolled[t:t0])
            pieces.append(patches[t0])
            t = t0 + SUB
        if t < R:
            pieces.append(rolled[t:])
        return jnp.concatenate(pieces, axis=0)

    par = par_ref[...]
    wl_raw = lo_ref[0]
    al_raw = lo_ref[1]
    tw = jnp.tanh(wl_raw + (prev_rows(wl_raw, 3 * NPAIR, 1) - wl_raw) * mulo_ref[0:1, :])
    xa = al_raw + (prev_rows(al_raw, 3 * NPAIR + 1, 1) - al_raw) * mulo_ref[1:2, :]

    seg01 = jnp.where((_iota((QL, QL), 0) & -H) == (_iota((QL, QL), 1) & -H), 1.0, 0.0).astype(BF16)
    ti = _iota((R, R), 0)
    tj = _iota((R, R), 1)
    tri = jnp.where((tj <= ti) & ((ti & -C) == (tj & -C)), 1.0, 0.0).astype(BF16)
    SW = NH * C
    ri = _iota((C, SW), 0)
    ci = _iota((C, SW), 1) & (C - 1)
    strict = ci < ri
    incl = ci <= ri
    eye = jnp.where(ri == ci, 1.0, 0.0)
    lane_i = _iota((1, SW), 1)
    lane_c = _iota((1, QL), 1)
    head_i = [(lane_i >= C * h) & (lane_i < C * (h + 1)) for h in range(NH)]
    head_c = [(lane_c >= H * h) & (lane_c < H * (h + 1)) for h in range(NH)]
    bd_mask = (_iota((QL, QL), 0) & -H) == (_iota((QL, QL), 1) & -H)
    n_dbl = C.bit_length() - 2

    GW = W // NG
    GP = GW // LANE
    GQ = GW // QL
    st = [dict() for _ in range(NG)]

    def glanes(x, g):
        return x[:, g * GW:(g + 1) * GW]

    def seg(x):
        pm = jnp.concatenate([x[:, q * QL:(q + 1) * QL] for q in range(GQ)], axis=0)
        sm = _dot(pm.astype(BF16), seg01)
        return jnp.concatenate([sm[q * R:(q + 1) * R] for q in range(GQ)], axis=1)

    def wide1(g):
        d = st[g]
        pg = glanes(par, g)
        d["par"] = pg
        rz = wide(zr_ref, g * GP, GP)
        kz = wide(zr_ref, NPAIR + g * GP, GP)
        vz = wide(zr_ref, 2 * NPAIR + g * GP, GP)
        d["r"] = rz + (prev_rows(rz, g * GP, GP) - rz) * pg[0:1]
        d["k"] = kz + (prev_rows(kz, NPAIR + g * GP, GP) - kz) * pg[1:2]
        d["v"] = vz + (prev_rows(vz, 2 * NPAIR + g * GP, GP) - vz) * pg[2:3]
        d["wl"] = pg[3:4] + _dot(tw.astype(BF16), w2_ref[:, g * GW:(g + 1) * GW])
        d["alp"] = pg[4:5] + _dot(xa.astype(BF16), a2_ref[:, g * GW:(g + 1) * GW])

    def wide2(g):
        d = st[g]
        pg = d["par"]
        ld = -jnp.exp(_log_sigmoid(d["wl"]) - 0.5)
        d["ld"] = jnp.where(valid, ld, 0.0)
        d["al"] = _sigmoid(d["alp"])
        d["kkv"] = d["k"] * pg[5:6]
        d["k2"] = d["k"] * (1.0 + (d["al"] - 1.0) * pg[6:7])
        d["ssq"] = seg(d["kkv"] * d["kkv"])
        d["L"] = _cumsum_rows(tri, d["ld"])
        d["bsum"] = seg(d["r"] * d["k2"] * pg[7:8])

    def wide3(g):
        d = st[g]
        kkn = d["kkv"] / jnp.maximum(jnp.sqrt(d["ssq"]), 1e-12)
        d["bonus"] = d["bsum"] * d["v"]
        L, ld = d["L"], d["ld"]
        if C >= 32:
            mid = [L[s * C + C // 2 - 1:s * C + C // 2] for s in range(nsb)]
            lref = mid[0]
            for s in range(1, nsb):
                lref = jnp.where(row >= s * C, mid[s], lref)
        else:
            mid = [jnp.zeros((1, GW), F32)] * nsb
            lref = mid[0]
        e_fwd = jnp.exp(L - lref)
        e_bwd = jnp.exp(lref - L)
        gs = slice(g * GW, (g + 1) * GW)
        at_ref[:, gs] = -kkn * jnp.exp(L - ld - lref)
        rt_ref[:, gs] = d["r"] * e_fwd
        bt_ref[:, gs] = jnp.where(valid, kkn * d["al"], 0.0) * e_bwd
        kt_ref[:, gs] = jnp.where(valid, d["k2"], 0.0) * e_bwd
        vm_ref[:, gs] = jnp.where(valid, d["v"], 0.0)
        d["e_mid"] = [jnp.exp(m) for m in mid]
        d["e_last"] = [e_fwd[s * C + C - 1:s * C + C] for s in range(nsb)]
        d["p_last"] = [jnp.exp(L[s * C + C - 1:s * C + C]) for s in range(nsb)]
        for key in ("wl", "alp", "ld", "al", "kkv", "k2", "ssq", "L", "bsum", "k", "r", "v"):
            del d[key]

    def sl(ref, s, q):
        return ref[s * C:(s + 1) * C, q * QL:(q + 1) * QL]

    def blockdiag(x, masks):
        return jnp.concatenate([jnp.where(m, x, jnp.zeros_like(x)) for m in masks], axis=0)

    def units_of(g):
        return [(s, g * GQ + j) for s in range(nsb) for j in range(GQ)]

    unit_keys = ("lhs_ar", "rhs_bk", "a_ab", "a_ak", "a_rb", "a_rk", "inv", "apow", "x2", "av2", "vm", "uu")

    def unit_g(g):
        d = st[g]
        for name in unit_keys:
            d[name] = {}
        for u in units_of(g):
            s, q = u
            d["lhs_ar"][u] = jnp.concatenate([sl(at_ref, s, q), sl(rt_ref, s, q)], axis=0).astype(BF16)
            bt = sl(bt_ref, s, q).astype(BF16)
            kt = sl(kt_ref, s, q).astype(BF16)
            d["rhs_bk"][u] = jnp.concatenate([bt, kt], axis=0)
            gm = _dot(d["lhs_ar"][u],
                      jnp.concatenate([blockdiag(bt, head_c), blockdiag(kt, head_c)], axis=0), NT)
            d["a_ab"][u] = jnp.where(strict, gm[:C, :SW], 0.0)
            d["a_ak"][u] = jnp.where(strict, gm[:C, SW:], 0.0)
            d["a_rb"][u] = jnp.where(incl, gm[C:, :SW], 0.0)
            d["a_rk"][u] = jnp.where(incl, gm[C:, SW:], 0.0)
            d["inv"][u] = eye + d["a_ab"][u]

    def unit_p1(g):
        d = st[g]
        for u in units_of(g):
            pw = d["a_ab"][u].astype(BF16)
            d["apow"][u] = _dot(pw, blockdiag(pw, head_i))

    def unit_level(g, lvl):
        d = st[g]
        for u in units_of(g):
            pw = d["apow"][u].astype(BF16)
            rhs = blockdiag(pw, head_i)
            if lvl < n_dbl - 1:
                both = _dot(jnp.concatenate([pw, d["inv"][u].astype(BF16)], axis=0), rhs)
                d["apow"][u] = both[:C]
                d["inv"][u] = d["inv"][u] + both[C:]
            else:
                d["inv"][u] = d["inv"][u] + _dot(d["inv"][u].astype(BF16), rhs)

    def unit_x2(g):
        d = st[g]
        for u in units_of(g):
            s, q = u
            lo = (q - g * GQ) * QL
            bdm = bd_ref[s * NQ + q] * d["e_mid"][s][:, lo:lo + QL]
            d["x2"][u] = _dot(d["lhs_ar"][u], bdm.astype(BF16), NT)

    def unit_av2(g):
        d = st[g]
        for u in units_of(g):
            s, q = u
            d["vm"][u] = sl(vm_ref, s, q).astype(BF16)
            d["av2"][u] = _dot(jnp.concatenate([d["a_ak"][u], d["a_rk"][u]], axis=0).astype(BF16),
                               blockdiag(d["vm"][u], head_c))

    def unit_u(g):
        d = st[g]
        for u in units_of(g):
            w1a = (d["x2"][u][:C] + d["av2"][u][:C]).astype(BF16)
            d["uu"][u] = _dot(d["inv"][u].astype(BF16), blockdiag(w1a, head_c)).astype(BF16)

    def unit_y(g):
        d = st[g]
        for u in units_of(g):
            s, q = u
            y_ref[s * C:(s + 1) * C, q * QL:(q + 1) * QL] = (
                d["x2"][u][C:] + d["av2"][u][C:] + _dot(d["a_rb"][u].astype(BF16), blockdiag(d["uu"][u], head_c)))

    def unit_upd(g):
        d = st[g]
        for u in units_of(g):
            s, q = u
            lo = (q - g * GQ) * QL
            upd = _dot(jnp.concatenate([d["uu"][u], d["vm"][u]], axis=0), d["rhs_bk"][u], TN)
            bd = bd_ref[s * NQ + q] * d["p_last"][s][:, lo:lo + QL] + upd * d["e_last"][s][:, lo:lo + QL]
            bd_ref[s * NQ + q] = jnp.where(bd_mask, bd, 0.0)
        for name in unit_keys:
            del d[name]

    def post1(g):
        d = st[g]
        d["y"] = y_ref[:, g * GW:(g + 1) * GW]
        d["mean"] = seg(d["y"]) * (1.0 / H)

    def post2(g):
        d = st[g]
        d["yc"] = d["y"] - d["mean"]
        d["var"] = seg(d["yc"] * d["yc"]) * (1.0 / H)

    def post3(g):
        d = st[g]
        pg = d["par"]
        out = ((d["yc"] * lax.rsqrt(d["var"] + LNX_EPS) * pg[8:9] + pg[9:10] + d["bonus"])
               * _silu(wide(za_ref, g * GP, GP)))
        for p in range(GP):
            oa_ref[g * GP + p] = out[:, p * LANE:(p + 1) * LANE]

    unit_stages = ([unit_g, unit_p1] + [functools.partial(unit_level, lvl=i) for i in range(n_dbl)]
                   + [unit_x2, unit_av2, unit_u, unit_y, unit_upd])
    for f in (wide1, wide2, wide3):
        f(0)
    for g in range(NG):
        side = []
        if g + 1 < NG:
            side += [(wide1, g + 1), (wide2, g + 1), (wide3, g + 1)]
        if g > 0:
            side += [(post1, g - 1), (post2, g - 1), (post3, g - 1)]
        every = max(1, len(unit_stages) // max(1, len(side)))
        for j, stage in enumerate(unit_stages):
            stage(g)
            if side and (j + 1) % every == 0:
                f, gg = side.pop(0)
                f(gg)
        for f, gg in side:
            f(gg)
    for f in (post1, post2, post3):
        f(NG - 1)

    if nchunk > 1:
        for s in range(nsb):
            last = s * C + C - 1
            for j in range(3 * NPAIR):
                prev_ref[s, j:j + 1, :] = zr_ref[j, last:last + 1, :]
            prev_ref[s, 3 * NPAIR:3 * NPAIR + 1, :] = wl_raw[last:last + 1]
            prev_ref[s, 3 * NPAIR + 1:3 * NPAIR + 2, :] = al_raw[last:last + 1]

    @pl.when(c == nchunk - 1)
    def _fin():
        for s, q in units:
            bd = bd_ref[s * NQ + q]
            for h in range(NH):
                sout_ref[0, s, NH * q + h] = bd[h * H:(h + 1) * H, h * H:(h + 1) * H]
        shift_ref[...] = jnp.zeros(shift_ref.shape, F32)
        for s in range(nsb):
            last = s * C + (valid_hi - 1) - (nchunk - 1) * C
            for j in range(3 * NPAIR):
                shift_ref[s, j:j + 1, :] = zr_ref[j, last:last + 1, :]
            shift_ref[s, 3 * NPAIR:3 * NPAIR + 1, :] = wl_raw[last:last + 1]
            shift_ref[s, 3 * NPAIR + 1:3 * NPAIR + 2, :] = al_raw[last:last + 1]


def _rwkv(z3, par, mulo, w2, a2, buf, *, state, prev_oa, prev_state, layer, depth, nseq, nsb, nchunk, ngroup, C,
          row0, valid_lo, valid_hi):
    R = nsb * C
    rb = lambda n, c: row0 + c * (nseq // nsb) + n
    state_block = (1, nsb, A_HEADS, A_HEAD, A_HEAD)
    state_map = lambda n, c: (layer, n, 0, 0, 0)
    in_specs = [
        pl.BlockSpec((3 * NPAIR, R, LANE), lambda n, c: (0, rb(n, c), 0)),
        pl.BlockSpec((NPAIR, R, LANE), lambda n, c: (SL_ZA // NPAIR, rb(n, c), 0)),
        pl.BlockSpec((2, R, LANE), lambda n, c: (SL_WLO // 2, rb(n, c), 0)),
        pl.BlockSpec((N_PAR_ROWS, D_MODEL), lambda n, c: (0, 0)),
        pl.BlockSpec((8, LANE), lambda n, c: (0, 0)),
        pl.BlockSpec((LANE, D_MODEL), lambda n, c: (0, 0)),
        pl.BlockSpec((LANE, D_MODEL), lambda n, c: (0, 0)),
        pl.BlockSpec((nsb, N_BUF_ROWS, LANE), lambda n, c: (n, 0, 0)),
    ]
    args = [z3, z3, z3, par, mulo, w2, a2, buf]
    if state is not None:
        in_specs.append(pl.BlockSpec(state_block, state_map))
        args.append(state)
    aliases = {}
    for out_idx, prev in enumerate((prev_oa, prev_state)):
        if prev is not None:
            aliases[len(args)] = out_idx
            in_specs.append(pl.BlockSpec(memory_space=pl.ANY))
            args.append(prev)
    kern = functools.partial(_rwkv_kernel, C=C, nsb=nsb, nchunk=nchunk, ngroup=ngroup, valid_lo=valid_lo,
                             valid_hi=valid_hi, zero_init=state is None, n_alias=len(aliases))
    wide_scratch = pltpu.VMEM((R, D_MODEL), F32)
    unit_lanes = RWKV_UNIT_HEADS * A_HEAD
    return pl.pallas_call(
        kern,
        grid=(nseq // nsb, nchunk),
        in_specs=in_specs,
        out_specs=[
            pl.BlockSpec((NPAIR, R, LANE), lambda n, c: (0, rb(n, c), 0)),
            pl.BlockSpec(state_block, state_map),
            pl.BlockSpec((nsb, N_BUF_ROWS, LANE), lambda n, c: (n, 0, 0)),
        ],
        out_shape=[
            jax.ShapeDtypeStruct((NPAIR, z3.shape[1], LANE), F32),
            jax.ShapeDtypeStruct((depth, nseq, A_HEADS, A_HEAD, A_HEAD), F32),
            jax.ShapeDtypeStruct((nseq, N_BUF_ROWS, LANE), F32),
        ],
        input_output_aliases=aliases,
        scratch_shapes=[
            pltpu.VMEM((nsb * D_MODEL // unit_lanes, unit_lanes, unit_lanes), F32),
            pltpu.VMEM((nsb, N_BUF_ROWS, LANE), F32),
        ] + [wide_scratch] * 6,
        compiler_params=pltpu.CompilerParams(
            dimension_semantics=("arbitrary", "arbitrary"), vmem_limit_bytes=VMEM_LIMIT),
        name="rwkv_c%d" % C,
    )(*args)


def _gla_kernel(*refs, C, SB, nsb, nchunk, valid_lo, valid_hi, zero_init, n_alias):
    it = iter(refs)
    q_ref, k_ref, v_ref, zb_ref, gk_ref, gwh_ref, gwl_ref, nw_ref = (next(it) for _ in range(8))
    sin_ref = None if zero_init else next(it)
    for _ in range(n_alias):
        next(it)
    ob_ref, sout_ref, s_ref, att_ref = it
    c = pl.program_id(1)
    R = nsb * C
    units = [(s, h) for s in range(nsb) for h in range(B_HEADS)]
    QS = B_HK // LANE
    VS = B_HV // LANE

    def slot(u):
        return u[0] * B_HEADS + u[1]

    @pl.when(c == 0)
    def _init():
        for u in units:
            s_ref[slot(u)] = jnp.zeros((B_HK, B_HV), F32) if zero_init else sin_ref[0, u[0], u[1]]

    row = _iota((R, 1), 0)
    pos = c * C + (row & (C - 1))
    valid = (pos >= valid_lo) & (pos < valid_hi)

    glo1 = jnp.where(_iota((R, LANE), 1) == GATE_LORA, 1.0, gk_ref[0])
    g = jnp.where(valid, _log_sigmoid(_mm3(glo1, gwh_ref[...], gwl_ref[...])) / GATE_TAU, 0.0)
    ti = _iota((R, R), 0)
    tj = _iota((R, R), 1)
    tri = jnp.where((tj <= ti) & ((ti & -C) == (tj & -C)), 1.0, 0.0).astype(BF16)
    parts = _split2(g)
    b = _dot(tri, parts[0]) + _dot(tri, parts[1])
    eb = jnp.exp(b)
    b_col = []
    for s in range(nsb):
        ones_s = jnp.where((row >= s * C) & (row < (s + 1) * C), 1.0, 0.0).astype(BF16) * jnp.ones((1, LANE), BF16)
        b_col.append(_dot(parts[0], ones_s, TN) + _dot(parts[1], ones_s, TN))

    def part(x, u):
        s, h = u
        return x[s * C:(s + 1) * C, h * B_HK:(h + 1) * B_HK]

    def gather(ref, u, n):
        s, h = u
        return jnp.concatenate([ref[n * h + j, s * C:(s + 1) * C, :] for j in range(n)], axis=1)

    vrow = {u: valid[u[0] * C:(u[0] + 1) * C] for u in units}
    q = {u: gather(q_ref, u, QS) * (B_HK ** -0.5) for u in units}
    k = {u: jnp.where(vrow[u], gather(k_ref, u, QS), 0.0) for u in units}
    v = {u: jnp.where(vrow[u], gather(v_ref, u, VS), 0.0).astype(BF16) for u in units}
    bu = {u: part(b, u) for u in units}

    o = {u: _mm(q[u] * part(eb, u), s_ref[slot(u)]) for u in units}

    att_ref[...] = jnp.zeros(att_ref.shape, F32)
    nb = C // SB
    for i in range(1, nb):
        lo, hi = SB * i, SB * (i + 1)
        for u in units:
            beta = bu[u][lo - 1:lo]
            qi = q[u][lo:hi] * jnp.exp(bu[u][lo:hi] - beta)
            kj = k[u][:lo] * jnp.exp(beta - bu[u][:lo])
            att_ref[slot(u), :lo, lo:hi] = _mm(kj, qi, NT)

    srow = _iota((SB, 1), 0)

    def _diag_matmul():
        causal = srow <= _iota((SB, SB), 1)
        for i in range(nb):
            lo = SB * i
            for u in units:
                bb = bu[u][lo:lo + SB]
                qh = q[u][lo:lo + SB] * jnp.exp(bb - bb[0:1])
                kh = k[u][lo:lo + SB] * jnp.exp(bb[0:1] - bb)
                att_ref[slot(u), lo:lo + SB, lo:lo + SB] = jnp.where(causal, _mm(kh, qh, NT), 0.0)

    def _diag_exact():
        for i in range(nb):
            lo = SB * i
            for u in units:
                bb = bu[u][lo:lo + SB]
                kb = k[u][lo:lo + SB]
                for t in range(SB):
                    e = jnp.exp(jnp.where(srow <= t, bb[t:t + 1] - bb, -jnp.inf))
                    att_ref[slot(u), lo:lo + SB, lo + t:lo + t + 1] = jnp.sum(
                        q[u][lo + t:lo + t + 1] * kb * e, axis=-1, keepdims=True)

    if nb == 1:
        _diag_exact()
    else:
        span = jnp.max(jnp.concatenate(
            [b[s * C + SB * i:s * C + SB * i + 1] - b[s * C + SB * i + SB - 1:s * C + SB * i + SB]
             for s in range(nsb) for i in range(nb)], axis=0))
        pl.when(span < GLA_FACTOR_MAX_DECAY)(_diag_matmul)
        pl.when(span >= GLA_FACTOR_MAX_DECAY)(_diag_exact)

    for u in units:
        o[u] = o[u] + _dot(att_ref[slot(u)].astype(BF16), v[u], TN)

    for u in units:
        s, h = u
        ms = jnp.mean(o[u] * o[u], axis=-1, keepdims=True)
        on = o[u] * lax.rsqrt(ms + NORM_EPS) * nw_ref[...]
        for j in range(VS):
            ob_ref[VS * h + j, s * C:(s + 1) * C, :] = (
                on[:, j * LANE:(j + 1) * LANE] * _silu(zb_ref[VS * h + j, s * C:(s + 1) * C, :]))

    for u in units:
        s, h = u
        decay = jnp.exp(b_col[s][h * B_HK:(h + 1) * B_HK])
        decay = jnp.concatenate([decay] * VS, axis=1)
        kd = k[u] * jnp.exp(bu[u][C - 1:C] - bu[u])
        s_ref[slot(u)] = s_ref[slot(u)] * decay + _dot(kd.astype(BF16), v[u], TN)

    @pl.when(c == nchunk - 1)
    def _fin():
        for u in units:
            sout_ref[0, u[0], u[1]] = s_ref[slot(u)]


def _gla(z3, gw, nw, *, state, prev_ob, prev_state, layer, depth, nseq, nsb, nchunk, C, SB, row0, valid_lo,
         valid_hi):
    R = nsb * C
    rb = lambda n, c: row0 + c * (nseq // nsb) + n
    state_block = (1, nsb, B_HEADS, B_HK, B_HV)
    state_map = lambda n, c: (layer, n, 0, 0, 0)
    qs = B_HEADS * B_HK // LANE
    vs = B_HEADS * B_HV // LANE
    in_specs = [
        pl.BlockSpec((qs, R, LANE), lambda n, c: (SL_QB // qs, rb(n, c), 0)),
        pl.BlockSpec((qs, R, LANE), lambda n, c: (SL_KB // qs, rb(n, c), 0)),
        pl.BlockSpec((vs, R, LANE), lambda n, c: (SL_VB // vs, rb(n, c), 0)),
        pl.BlockSpec((vs, R, LANE), lambda n, c: (SL_ZB // vs, rb(n, c), 0)),
        pl.BlockSpec((1, R, LANE), lambda n, c: (SL_GK, rb(n, c), 0)),
        pl.BlockSpec((LANE, B_HEADS * B_HK), lambda n, c: (0, 0)),
        pl.BlockSpec((LANE, B_HEADS * B_HK), lambda n, c: (0, 0)),
        pl.BlockSpec((1, B_HV), lambda n, c: (0, 0)),
    ]
    args = [z3, z3, z3, z3, z3, *gw, nw]
    if state is not None:
        in_specs.append(pl.BlockSpec(state_block, state_map))
        args.append(state)
    aliases = {}
    for out_idx, prev in enumerate((prev_ob, prev_state)):
        if prev is not None:
            aliases[len(args)] = out_idx
            in_specs.append(pl.BlockSpec(memory_space=pl.ANY))
            args.append(prev)
    kern = functools.partial(_gla_kernel, C=C, SB=SB, nsb=nsb, nchunk=nchunk, valid_lo=valid_lo, valid_hi=valid_hi,
                             zero_init=state is None, n_alias=len(aliases))
    return pl.pallas_call(
        kern,
        grid=(nseq // nsb, nchunk),
        in_specs=in_specs,
        out_specs=[
            pl.BlockSpec((vs, R, LANE), lambda n, c: (0, rb(n, c), 0)),
            pl.BlockSpec(state_block, state_map),
        ],
        out_shape=[
            jax.ShapeDtypeStruct((NPAIR, z3.shape[1], LANE), F32),
            jax.ShapeDtypeStruct((depth, nseq, B_HEADS, B_HK, B_HV), F32),
        ],
        input_output_aliases=aliases,
        scratch_shapes=[
            pltpu.VMEM((nsb * B_HEADS, B_HK, B_HV), F32),
            pltpu.VMEM((nsb * B_HEADS, C, C), F32),
        ],
        compiler_params=pltpu.CompilerParams(
            dimension_semantics=("arbitrary", "arbitrary"), vmem_limit_bytes=VMEM_LIMIT),
        name="gla_c%d" % C,
    )(*args)


def _outproj_kernel(oa_ref, ob_ref, ga_ref, gb_ref, x_ref, w_ref, g_ref, o_ref, m_ref):
    for q in range(NPAIR):
        m = _sigmoid(ga_ref[q]) * oa_ref[q] + _sigmoid(gb_ref[q]) * ob_ref[q]
        m_ref[:, q * LANE:(q + 1) * LANE] = m.astype(BF16)
    y = jnp.dot(m_ref[...], w_ref[...], preferred_element_type=F32)
    ms = jnp.mean(y * y, axis=-1, keepdims=True)
    o_ref[...] = x_ref[...] + y * lax.rsqrt(ms + NORM_EPS) * g_ref[...]


def _outproj(oa3, ob3, z3, x, w, g, tile_m):
    m = x.shape[0]
    return pl.pallas_call(
        _outproj_kernel,
        grid=(m // tile_m,),
        in_specs=[
            pl.BlockSpec((NPAIR, tile_m, LANE), lambda i: (0, i, 0)),
            pl.BlockSpec((NPAIR, tile_m, LANE), lambda i: (0, i, 0)),
            pl.BlockSpec((NPAIR, tile_m, LANE), lambda i: (SL_GA // NPAIR, i, 0)),
            pl.BlockSpec((NPAIR, tile_m, LANE), lambda i: (SL_GB // NPAIR, i, 0)),
            pl.BlockSpec((tile_m, D_MODEL), lambda i: (i, 0)),
            pl.BlockSpec((D_MODEL, D_MODEL), lambda i: (0, 0)),
            pl.BlockSpec((1, D_MODEL), lambda i: (0, 0)),
        ],
        out_specs=pl.BlockSpec((tile_m, D_MODEL), lambda i: (i, 0)),
        out_shape=jax.ShapeDtypeStruct((m, D_MODEL), F32),
        scratch_shapes=[pltpu.VMEM((tile_m, D_MODEL), BF16)],
        compiler_params=pltpu.CompilerParams(
            dimension_semantics=("arbitrary",), vmem_limit_bytes=VMEM_LIMIT),
        name="outproj",
    )(oa3, ob3, z3, z3, x, w, g)


def _pick_tile(m, cap, mult):
    best = mult
    for t in range(mult, cap + 1, mult):
        if m % t == 0:
            best = t
    assert m % best == 0
    return best


def _pad_lanes(x):
    return jnp.pad(x, [(0, 0)] * (x.ndim - 1) + [(0, LANE - x.shape[-1])])


def _shift_rows(buf):
    n = buf.shape[0]
    rkv = buf[:, :3 * NPAIR].reshape(n, 3 * D_MODEL)
    return jnp.concatenate([rkv, buf[:, 3 * NPAIR, :LORA], buf[:, 3 * NPAIR + 1, :LORA]], axis=1)


def kernel(x_prompt, x_sample, state_rwkv, state_shift, state_gla, meta_tokens, norm_pre, w_in, rwkv_mu, rwkv_w0, rwkv_w2, rwkv_a0, rwkv_a2, rwkv_k_k, rwkv_k_a, rwkv_r_k, rwkv_lnx_w, rwkv_lnx_b, gla_gk_w2, gla_gk_b, gla_norm_w, w_out, norm_post):
    depth = w_in.shape[0]
    bp, seq, _ = x_prompt.shape
    bs, dseq, _ = x_sample.shape
    front = CHUNK - N_META
    tp = front + N_META + seq
    assert tp % CHUNK == 0 and dseq <= SAMPLE_ROWS and bs % RWKV_SAMPLE_SEQS == 0
    nchunk = tp // CHUNK
    mp = bp * tp
    ms = bs * SAMPLE_ROWS
    mtot = mp + ms
    tile_in = _pick_tile(mtot, 1184, 16)
    tile_out = _pick_tile(mtot, 296, 8)

    meta = jnp.broadcast_to(meta_tokens[None], (bp, N_META, D_MODEL))
    xp = jnp.concatenate([jnp.zeros((bp, front, D_MODEL), F32), meta, x_prompt], axis=1)
    xp = xp.reshape(bp, nchunk, CHUNK, D_MODEL).transpose(1, 0, 2, 3).reshape(mp, D_MODEL)
    xs = jnp.pad(x_sample, ((0, 0), (0, SAMPLE_ROWS - dseq), (0, 0))).reshape(ms, D_MODEL)
    x = jnp.concatenate([xp, xs], axis=0)

    w_packed = _pack_w_in(w_in)
    w_out_bf = w_out.astype(BF16)
    zero_buf = jnp.zeros((bp, N_BUF_ROWS, LANE), F32)

    sr_p = sr_s = sg_p = sg_s = None
    shift_p, shift_s = [], []
    for l in range(depth):
        mu = rwkv_mu[l]
        par = jnp.stack([mu[:D_MODEL], mu[D_MODEL:2 * D_MODEL], mu[2 * D_MODEL:3 * D_MODEL],
                         rwkv_w0[l], rwkv_a0[l], rwkv_k_k[l], rwkv_k_a[l], rwkv_r_k[l].reshape(-1),
                         rwkv_lnx_w[l], rwkv_lnx_b[l]], axis=0)
        par = jnp.pad(par, ((0, N_PAR_ROWS - par.shape[0]), (0, 0)))
        mulo = jnp.pad(jnp.stack([_pad_lanes(mu[3 * D_MODEL:3 * D_MODEL + LORA]),
                                  _pad_lanes(mu[3 * D_MODEL + LORA:])], axis=0), ((0, 6), (0, 0)))
        w2 = jnp.pad(rwkv_w2[l], ((0, LANE - LORA), (0, 0))).astype(BF16)
        a2 = jnp.pad(rwkv_a2[l], ((0, LANE - LORA), (0, 0))).astype(BF16)
        gw = jnp.concatenate([gla_gk_w2[l], gla_gk_b[l][None]], axis=0)
        gw = _split2(jnp.pad(gw, ((0, LANE - gw.shape[0]), (0, 0))))
        nw = gla_norm_w[l][None]
        sh = state_shift[l]
        buf_s = jnp.concatenate([sh[:, :3 * D_MODEL].reshape(bs, 3 * NPAIR, LANE),
                                 _pad_lanes(sh[:, 3 * D_MODEL:3 * D_MODEL + LORA])[:, None],
                                 _pad_lanes(sh[:, 3 * D_MODEL + LORA:])[:, None]], axis=1)
        buf_s = jnp.pad(buf_s, ((0, 0), (0, N_BUF_ROWS - buf_s.shape[1]), (0, 0)))

        z3 = _inproj(x, norm_pre[l][None], w_packed, l, tile_in)

        prompt = dict(layer=l, depth=depth, nseq=bp, nchunk=nchunk, C=CHUNK, row0=0, valid_lo=front, valid_hi=tp)
        sample = dict(layer=l, depth=depth, nseq=bs, nchunk=1, C=SAMPLE_ROWS, valid_lo=0, valid_hi=dseq)
        oa, sr_p, sh_p = _rwkv(z3, par, mulo, w2, a2, zero_buf, state=None, prev_oa=None, prev_state=sr_p,
                               nsb=RWKV_PROMPT_SEQS, ngroup=RWKV_LANE_GROUPS, **prompt)
        oa, sr_s, sh_s = _rwkv(z3, par, mulo, w2, a2, buf_s, state=state_rwkv, prev_oa=oa, prev_state=sr_s,
                               nsb=RWKV_SAMPLE_SEQS, ngroup=1, row0=mp // (SAMPLE_ROWS * RWKV_SAMPLE_SEQS),
                               **sample)
        ob, sg_p = _gla(z3, gw, nw, state=None, prev_ob=None, prev_state=sg_p, SB=GLA_SUB, nsb=GLA_PROMPT_SEQS,
                        **prompt)
        ob, sg_s = _gla(z3, gw, nw, state=state_gla, prev_ob=ob, prev_state=sg_s, SB=SAMPLE_ROWS,
                        nsb=GLA_SAMPLE_SEQS, row0=mp // (SAMPLE_ROWS * GLA_SAMPLE_SEQS), **sample)

        x = _outproj(oa, ob, z3, x, w_out_bf[l], norm_post[l][None], tile_out)
        shift_p.append(_shift_rows(sh_p))
        shift_s.append(_shift_rows(sh_s))

    y_prompt = x[:mp].reshape(nchunk, bp, CHUNK, D_MODEL).transpose(1, 0, 2, 3).reshape(bp, tp, D_MODEL)
    y_prompt = y_prompt[:, front + N_META:]
    y_sample = x[mp:].reshape(bs, SAMPLE_ROWS, D_MODEL)[:, :dseq]
    return (y_prompt, y_sample, sr_p, jnp.stack(shift_p), sg_p, sr_s, jnp.stack(shift_s), sg_s)
```

```python
import functools
import math

import jax
import jax.numpy as jnp
from jax import lax
from jax.experimental import pallas as pl
from jax.experimental.pallas import tpu as pltpu

F32 = jnp.float32
BF16 = jnp.bfloat16

D_MODEL = 2048
N_META = 16
NORM_EPS = 1e-6
LNX_EPS = 64e-5
A_HEAD = 64
A_HEADS = D_MODEL // A_HEAD
LORA = 96
B_HEADS = 4
B_HK = 256
B_HV = 512
GATE_LORA = 16
GATE_TAU = 16.0
CHUNK = 64
SAMPLE_ROWS = 8
GLA_SUB = 16
GLA_FACTOR_MAX_DECAY = 60.0
RWKV_SAMPLE_SEQS = 4
GLA_SAMPLE_SEQS = 2
GLA_PROMPT_SEQS = 4
RWKV_PROMPT_SEQS = 2
RWKV_UNIT_HEADS = 4
RWKV_LANE_GROUPS = 2

LANE = 128
NPAIR = D_MODEL // LANE
SLABS_PER_TILE = 8
TILE_N = SLABS_PER_TILE * LANE
SL_R, SL_K, SL_V, SL_ZA = 0, 16, 32, 48
SL_QB, SL_KB, SL_VB, SL_ZB = 64, 72, 80, 96
SL_GA, SL_GB = 112, 128
SL_WLO, SL_ALO, SL_GK = 144, 145, 146
NSLAB = 147
NCOLP = NSLAB * LANE
N_BUF_ROWS = 56
N_PAR_ROWS = 16
VMEM_LIMIT = 56 * 1024 * 1024

NN = ((1,), (0,))
NT = ((1,), (1,))
TN = ((0,), (0,))


def _dot(a, b, dims=NN):
    return lax.dot_general(a, b, (dims, ((), ())), preferred_element_type=F32)


def _mm(a, b, dims=NN):
    return _dot(a.astype(BF16), b.astype(BF16), dims)


def _split2(x):
    hi = x.astype(BF16)
    lo = (x - hi.astype(F32)).astype(BF16)
    return hi, lo


def _mm3(a, b_hi, b_lo, dims=NN):
    ah, al = _split2(a)
    return _dot(ah, b_hi, dims) + _dot(al, b_hi, dims) + _dot(ah, b_lo, dims)


def _cumsum_rows(tri, x):
    hi, lo = _split2(x)
    return _dot(tri, hi) + _dot(tri, lo)


def _sigmoid(x):
    return 1.0 / (1.0 + jnp.exp(-x))


def _silu(x):
    return x * _sigmoid(x)


def _log_sigmoid(x):
    return jnp.minimum(x, 0.0) - jnp.log(1.0 + jnp.exp(-jnp.abs(x)))


def _iota(shape, dim):
    return lax.broadcasted_iota(jnp.int32, shape, dim)


W_RKV = 3 * D_MODEL
W_SHIFT = W_RKV + 2 * LORA
W_MID = 3 * D_MODEL
W_GK = W_SHIFT + W_MID
W_TAIL = 3 * D_MODEL


PACK_TILE = 512
W_MAIN = W_RKV + W_MID + W_TAIL
N_SMALL = NCOLP - W_MAIN


def _pack_kernel(w_ref, o_ref):
    o_ref[0] = w_ref[0].astype(BF16)


def _pack_w_in(w_in):
    depth, d, ncol = w_in.shape
    assert ncol == W_GK + GATE_LORA + W_TAIL and d == D_MODEL and N_SMALL <= TILE_N
    wt = jnp.swapaxes(w_in, 1, 2)
    small = jnp.concatenate([jnp.pad(wt[:, a:a + n], ((0, 0), (0, LANE - n), (0, 0)))
                             for a, n in ((W_RKV, LORA), (W_RKV + LORA, LORA), (W_GK, GATE_LORA))], axis=1)
    small = lax.optimization_barrier(jnp.pad(small, ((0, 0), (0, N_SMALL - small.shape[1]), (0, 0)))).astype(BF16)
    g1 = W_RKV // PACK_TILE
    g2 = (W_RKV + W_MID) // PACK_TILE
    skip1 = W_SHIFT - W_RKV
    skip2 = W_GK + GATE_LORA - (W_RKV + W_MID) - skip1

    def src_row(i):
        row = i * PACK_TILE + (i >= g1).astype(jnp.int32) * skip1 + (i >= g2).astype(jnp.int32) * skip2
        return pl.multiple_of(row, math.gcd(PACK_TILE, skip1, skip2))

    main = pl.pallas_call(
        _pack_kernel,
        grid=(depth, W_MAIN // PACK_TILE),
        in_specs=[pl.BlockSpec((pl.Element(1), pl.Element(PACK_TILE), pl.Element(D_MODEL)),
                               lambda l, i: (l, src_row(i), 0))],
        out_specs=pl.BlockSpec((1, PACK_TILE, D_MODEL), lambda l, i: (l, i, 0)),
        out_shape=jax.ShapeDtypeStruct((depth, W_MAIN, D_MODEL), BF16),
        compiler_params=pltpu.CompilerParams(
            dimension_semantics=("arbitrary", "arbitrary"), vmem_limit_bytes=VMEM_LIMIT),
        name="pack_w_in",
    )(wt)
    return main, small


def _inproj_kernel(x_ref, g_ref, w_ref, ws_ref, o_ref, h_ref):
    j = pl.program_id(1)

    @pl.when(j == 0)
    def _():
        x = x_ref[...]
        ms = jnp.mean(x * x, axis=-1, keepdims=True)
        h_ref[...] = (x * lax.rsqrt(ms + NORM_EPS) * g_ref[...]).astype(BF16)

    def emit(w):
        acc = _dot(h_ref[...], w, NT)
        for q in range(w.shape[0] // LANE):
            o_ref[q] = acc[:, q * LANE:(q + 1) * LANE]

    @pl.when(j < W_MAIN // TILE_N)
    def _main():
        emit(w_ref[0])

    @pl.when(j >= W_MAIN // TILE_N)
    def _small():
        emit(ws_ref[0])


def _inproj(x, g, w, layer, tile_m):
    m = x.shape[0]
    w_main, w_small = w
    n_main = W_MAIN // TILE_N
    return pl.pallas_call(
        _inproj_kernel,
        grid=(m // tile_m, n_main + 1),
        in_specs=[
            pl.BlockSpec((tile_m, D_MODEL), lambda i, j: (i, 0)),
            pl.BlockSpec((1, D_MODEL), lambda i, j: (0, 0)),
            pl.BlockSpec((1, TILE_N, D_MODEL), lambda i, j: (layer, jnp.minimum(j, n_main - 1), 0)),
            pl.BlockSpec((1, N_SMALL, D_MODEL), lambda i, j: (layer, 0, 0)),
        ],
        out_specs=pl.BlockSpec((SLABS_PER_TILE, tile_m, LANE), lambda i, j: (j, i, 0)),
        out_shape=jax.ShapeDtypeStruct((NSLAB, m, LANE), F32),
        scratch_shapes=[pltpu.VMEM((tile_m, D_MODEL), BF16)],
        compiler_params=pltpu.CompilerParams(
            dimension_semantics=("arbitrary", "arbitrary"), vmem_limit_bytes=VMEM_LIMIT),
        name="inproj",
    )(x, g, w_main, w_small)


def _rwkv_kernel(*refs, C, nsb, nchunk, ngroup, valid_lo, valid_hi, zero_init, n_alias):
    it = iter(refs)
    zr_ref, za_ref, lo_ref, par_ref, mulo_ref, w2_ref, a2_ref, buf_ref = (next(it) for _ in range(8))
    sin_ref = None if zero_init else next(it)
    for _ in range(n_alias):
        next(it)
    oa_ref, sout_ref, shift_ref, bd_ref, prev_ref, at_ref, rt_ref, bt_ref, kt_ref, vm_ref, y_ref = it
    c = pl.program_id(1)
    H = A_HEAD
    R = nsb * C
    W = D_MODEL
    NH = RWKV_UNIT_HEADS
    NG = ngroup
    QL = NH * H
    NQ = W // QL
    units = [(s, q) for s in range(nsb) for q in range(NQ)]

    @pl.when(c == 0)
    def _init():
        prev_ref[...] = buf_ref[...]
        for s, q in units:
            if zero_init:
                bd_ref[s * NQ + q] = jnp.zeros((QL, QL), F32)
            else:
                blocks = []
                for h in range(NH):
                    pieces = [jnp.zeros((H, H), F32)] * NH
                    pieces[h] = sin_ref[0, s, NH * q + h]
                    blocks.append(jnp.concatenate(pieces, axis=1))
                bd_ref[s * NQ + q] = jnp.concatenate(blocks, axis=0)

    row = _iota((R, 1), 0)
    rin = row & (C - 1)
    pos = c * C + rin
    valid = (pos >= valid_lo) & (pos < valid_hi)

    def wide(ref, n0, n):
        return jnp.concatenate([ref[n0 + j] for j in range(n)], axis=1)

    def rowcat(ref, s, n0, n):
        return jnp.concatenate([ref[s, n0 + j:n0 + j + 1, :] for j in range(n)], axis=1)

    SUB = 8
    row8 = _iota((SUB, 1), 0)

    def prev_rows(x, n0, n):
        rolled = pltpu.roll(x, 1, axis=0)
        patches = {}
        for s in range(nsb):
            t0 = s * C
            patches[t0] = jnp.where(row8 == 0, rowcat(prev_ref, s, n0, n), rolled[t0:t0 + SUB])
            if valid_lo > 0:
                t1 = s * C + (valid_lo % C) // SUB * SUB
                hit = (c == valid_lo // C) & (row8 == valid_lo % SUB)
                patches[t1] = jnp.where(hit, rowcat(buf_ref, s, n0, n), patches.get(t1, rolled[t1:t1 + SUB]))
        pieces, t = [], 0
        for t0 in sorted(patches):
            if t0 > t:
                pieces.append(rolled[t:t0])
            pieces.append(patches[t0])
            t = t0 + SUB
        if t < R:
            pieces.append(rolled[t:])
        return jnp.concatenate(pieces, axis=0)

    par = par_ref[...]
    wl_raw = lo_ref[0]
    al_raw = lo_ref[1]
    tw = jnp.tanh(wl_raw + (prev_rows(wl_raw, 3 * NPAIR, 1) - wl_raw) * mulo_ref[0:1, :])
    xa = al_raw + (prev_rows(al_raw, 3 * NPAIR + 1, 1) - al_raw) * mulo_ref[1:2, :]

    seg01 = jnp.where((_iota((QL, QL), 0) & -H) == (_iota((QL, QL), 1) & -H), 1.0, 0.0).astype(BF16)
    ti = _iota((R, R), 0)
    tj = _iota((R, R), 1)
    tri = jnp.where((tj <= ti) & ((ti & -C) == (tj & -C)), 1.0, 0.0).astype(BF16)
    SW = NH * C
    ri = _iota((C, SW), 0)
    ci = _iota((C, SW), 1) & (C - 1)
    strict = ci < ri
    incl = ci <= ri
    eye = jnp.where(ri == ci, 1.0, 0.0)
    lane_i = _iota((1, SW), 1)
    lane_c = _iota((1, QL), 1)
    head_i = [(lane_i >= C * h) & (lane_i < C * (h + 1)) for h in range(NH)]
    head_c = [(lane_c >= H * h) & (lane_c < H * (h + 1)) for h in range(NH)]
    bd_mask = (_iota((QL, QL), 0) & -H) == (_iota((QL, QL), 1) & -H)
    n_dbl = C.bit_length() - 2

    GW = W // NG
    GP = GW // LANE
    GQ = GW // QL
    st = [dict() for _ in range(NG)]

    def glanes(x, g):
        return x[:, g * GW:(g + 1) * GW]

    def seg(x):
        pm = jnp.concatenate([x[:, q * QL:(q + 1) * QL] for q in range(GQ)], axis=0)
        sm = _dot(pm.astype(BF16), seg01)
        return jnp.concatenate([sm[q * R:(q + 1) * R] for q in range(GQ)], axis=1)

    def wide1(g):
        d = st[g]
        pg = glanes(par, g)
        d["par"] = pg
        rz = wide(zr_ref, g * GP, GP)
        kz = wide(zr_ref, NPAIR + g * GP, GP)
        vz = wide(zr_ref, 2 * NPAIR + g * GP, GP)
        d["r"] = rz + (prev_rows(rz, g * GP, GP) - rz) * pg[0:1]
        d["k"] = kz + (prev_rows(kz, NPAIR + g * GP, GP) - kz) * pg[1:2]
        d["v"] = vz + (prev_rows(vz, 2 * NPAIR + g * GP, GP) - vz) * pg[2:3]
        d["wl"] = pg[3:4] + _dot(tw.astype(BF16), w2_ref[:, g * GW:(g + 1) * GW])
        d["alp"] = pg[4:5] + _dot(xa.astype(BF16), a2_ref[:, g * GW:(g + 1) * GW])

    def wide2(g):
        d = st[g]
        pg = d["par"]
        ld = -jnp.exp(_log_sigmoid(d["wl"]) - 0.5)
        d["ld"] = jnp.where(valid, ld, 0.0)
        d["al"] = _sigmoid(d["alp"])
        d["kkv"] = d["k"] * pg[5:6]
        d["k2"] = d["k"] * (1.0 + (d["al"] - 1.0) * pg[6:7])
        d["ssq"] = seg(d["kkv"] * d["kkv"])
        d["L"] = _cumsum_rows(tri, d["ld"])
        d["bsum"] = seg(d["r"] * d["k2"] * pg[7:8])

    def wide3(g):
        d = st[g]
        kkn = d["kkv"] / jnp.maximum(jnp.sqrt(d["ssq"]), 1e-12)
        d["bonus"] = d["bsum"] * d["v"]
        L, ld = d["L"], d["ld"]
        if C >= 32:
            mid = [L[s * C + C // 2 - 1:s * C + C // 2] for s in range(nsb)]
            lref = mid[0]
            for s in range(1, nsb):
                lref = jnp.where(row >= s * C, mid[s], lref)
        else:
            mid = [jnp.zeros((1, GW), F32)] * nsb
            lref = mid[0]
        e_fwd = jnp.exp(L - lref)
        e_bwd = jnp.exp(lref - L)
        gs = slice(g * GW, (g + 1) * GW)
        at_ref[:, gs] = -kkn * jnp.exp(L - ld - lref)
        rt_ref[:, gs] = d["r"] * e_fwd
        bt_ref[:, gs] = jnp.where(valid, kkn * d["al"], 0.0) * e_bwd
        kt_ref[:, gs] = jnp.where(valid, d["k2"], 0.0) * e_bwd
        vm_ref[:, gs] = jnp.where(valid, d["v"], 0.0)
        d["e_mid"] = [jnp.exp(m) for m in mid]
        d["e_last"] = [e_fwd[s * C + C - 1:s * C + C] for s in range(nsb)]
        d["p_last"] = [jnp.exp(L[s * C + C - 1:s * C + C]) for s in range(nsb)]
        for key in ("wl", "alp", "ld", "al", "kkv", "k2", "ssq", "L", "bsum", "k", "r", "v"):
            del d[key]

    def sl(ref, s, q):
        return ref[s * C:(s + 1) * C, q * QL:(q + 1) * QL]

    def blockdiag(x, masks):
        return jnp.concatenate([jnp.where(m, x, jnp.zeros_like(x)) for m in masks], axis=0)

    def units_of(g):
        return [(s, g * GQ + j) for s in range(nsb) for j in range(GQ)]

    unit_keys = ("lhs_ar", "rhs_bk", "a_ab", "a_ak", "a_rb", "a_rk", "inv", "apow", "x2", "av2", "vm", "uu")

    def unit_g(g):
        d = st[g]
        for name in unit_keys:
            d[name] = {}
        for u in units_of(g):
            s, q = u
            d["lhs_ar"][u] = jnp.concatenate([sl(at_ref, s, q), sl(rt_ref, s, q)], axis=0).astype(BF16)
            bt = sl(bt_ref, s, q).astype(BF16)
            kt = sl(kt_ref, s, q).astype(BF16)
            d["rhs_bk"][u] = jnp.concatenate([bt, kt], axis=0)
            gm = _dot(d["lhs_ar"][u],
                      jnp.concatenate([blockdiag(bt, head_c), blockdiag(kt, head_c)], axis=0), NT)
            d["a_ab"][u] = jnp.where(strict, gm[:C, :SW], 0.0)
            d["a_ak"][u] = jnp.where(strict, gm[:C, SW:], 0.0)
            d["a_rb"][u] = jnp.where(incl, gm[C:, :SW], 0.0)
            d["a_rk"][u] = jnp.where(incl, gm[C:, SW:], 0.0)
            d["inv"][u] = eye + d["a_ab"][u]

    def unit_p1(g):
        d = st[g]
        for u in units_of(g):
            pw = d["a_ab"][u].astype(BF16)
            d["apow"][u] = _dot(pw, blockdiag(pw, head_i))

    def unit_level(g, lvl):
        d = st[g]
        for u in units_of(g):
            pw = d["apow"][u].astype(BF16)
            rhs = blockdiag(pw, head_i)
            if lvl < n_dbl - 1:
                both = _dot(jnp.concatenate([pw, d["inv"][u].astype(BF16)], axis=0), rhs)
                d["apow"][u] = both[:C]
                d["inv"][u] = d["inv"][u] + both[C:]
            else:
                d["inv"][u] = d["inv"][u] + _dot(d["inv"][u].astype(BF16), rhs)

    def unit_x2(g):
        d = st[g]
        for u in units_of(g):
            s, q = u
            lo = (q - g * GQ) * QL
            bdm = bd_ref[s * NQ + q] * d["e_mid"][s][:, lo:lo + QL]
            d["x2"][u] = _dot(d["lhs_ar"][u], bdm.astype(BF16), NT)

    def unit_av2(g):
        d = st[g]
        for u in units_of(g):
            s, q = u
            d["vm"][u] = sl(vm_ref, s, q).astype(BF16)
            d["av2"][u] = _dot(jnp.concatenate([d["a_ak"][u], d["a_rk"][u]], axis=0).astype(BF16),
                               blockdiag(d["vm"][u], head_c))

    def unit_u(g):
        d = st[g]
        for u in units_of(g):
            w1a = (d["x2"][u][:C] + d["av2"][u][:C]).astype(BF16)
            d["uu"][u] = _dot(d["inv"][u].astype(BF16), blockdiag(w1a, head_c)).astype(BF16)

    def unit_y(g):
        d = st[g]
        for u in units_of(g):
            s, q = u
            y_ref[s * C:(s + 1) * C, q * QL:(q + 1) * QL] = (
                d["x2"][u][C:] + d["av2"][u][C:] + _dot(d["a_rb"][u].astype(BF16), blockdiag(d["uu"][u], head_c)))

    def unit_upd(g):
        d = st[g]
        for u in units_of(g):
            s, q = u
            lo = (q - g * GQ) * QL
            upd = _dot(jnp.concatenate([d["uu"][u], d["vm"][u]], axis=0), d["rhs_bk"][u], TN)
            bd = bd_ref[s * NQ + q] * d["p_last"][s][:, lo:lo + QL] + upd * d["e_last"][s][:, lo:lo + QL]
            bd_ref[s * NQ + q] = jnp.where(bd_mask, bd, 0.0)
        for name in unit_keys:
            del d[name]

    def post1(g):
        d = st[g]
        d["y"] = y_ref[:, g * GW:(g + 1) * GW]
        d["mean"] = seg(d["y"]) * (1.0 / H)

    def post2(g):
        d = st[g]
        d["yc"] = d["y"] - d["mean"]
        d["var"] = seg(d["yc"] * d["yc"]) * (1.0 / H)

    def post3(g):
        d = st[g]
        pg = d["par"]
        out = ((d["yc"] * lax.rsqrt(d["var"] + LNX_EPS) * pg[8:9] + pg[9:10] + d["bonus"])
               * _silu(wide(za_ref, g * GP, GP)))
        for p in range(GP):
            oa_ref[g * GP + p] = out[:, p * LANE:(p + 1) * LANE]

    unit_stages = ([unit_g, unit_p1] + [functools.partial(unit_level, lvl=i) for i in range(n_dbl)]
                   + [unit_x2, unit_av2, unit_u, unit_y, unit_upd])
    for f in (wide1, wide2, wide3):
        f(0)
    for g in range(NG):
        side = []
        if g + 1 < NG:
            side += [(wide1, g + 1), (wide2, g + 1), (wide3, g + 1)]
        if g > 0:
            side += [(post1, g - 1), (post2, g - 1), (post3, g - 1)]
        every = max(1, len(unit_stages) // max(1, len(side)))
        for j, stage in enumerate(unit_stages):
            stage(g)
            if side and (j + 1) % every == 0:
                f, gg = side.pop(0)
                f(gg)
        for f, gg in side:
            f(gg)
    for f in (post1, post2, post3):
        f(NG - 1)

    if nchunk > 1:
        for s in range(nsb):
            last = s * C + C - 1
            for j in range(3 * NPAIR):
                prev_ref[s, j:j + 1, :] = zr_ref[j, last:last + 1, :]
            prev_ref[s, 3 * NPAIR:3 * NPAIR + 1, :] = wl_raw[last:last + 1]
            prev_ref[s, 3 * NPAIR + 1:3 * NPAIR + 2, :] = al_raw[last:last + 1]

    @pl.when(c == nchunk - 1)
    def _fin():
        for s, q in units:
            bd = bd_ref[s * NQ + q]
            for h in range(NH):
                sout_ref[0, s, NH * q + h] = bd[h * H:(h + 1) * H, h * H:(h + 1) * H]
        shift_ref[...] = jnp.zeros(shift_ref.shape, F32)
        for s in range(nsb):
            last = s * C + (valid_hi - 1) - (nchunk - 1) * C
            for j in range(3 * NPAIR):
                shift_ref[s, j:j + 1, :] = zr_ref[j, last:last + 1, :]
            shift_ref[s, 3 * NPAIR:3 * NPAIR + 1, :] = wl_raw[last:last + 1]
            shift_ref[s, 3 * NPAIR + 1:3 * NPAIR + 2, :] = al_raw[last:last + 1]


def _rwkv(z3, par, mulo, w2, a2, buf, *, state, prev_oa, prev_state, layer, depth, nseq, nsb, nchunk, ngroup, C,
          row0, valid_lo, valid_hi):
    R = nsb * C
    rb = lambda n, c: row0 + c * (nseq // nsb) + n
    state_block = (1, nsb, A_HEADS, A_HEAD, A_HEAD)
    state_map = lambda n, c: (layer, n, 0, 0, 0)
    in_specs = [
        pl.BlockSpec((3 * NPAIR, R, LANE), lambda n, c: (0, rb(n, c), 0)),
        pl.BlockSpec((NPAIR, R, LANE), lambda n, c: (SL_ZA // NPAIR, rb(n, c), 0)),
        pl.BlockSpec((2, R, LANE), lambda n, c: (SL_WLO // 2, rb(n, c), 0)),
        pl.BlockSpec((N_PAR_ROWS, D_MODEL), lambda n, c: (0, 0)),
        pl.BlockSpec((8, LANE), lambda n, c: (0, 0)),
        pl.BlockSpec((LANE, D_MODEL), lambda n, c: (0, 0)),
        pl.BlockSpec((LANE, D_MODEL), lambda n, c: (0, 0)),
        pl.BlockSpec((nsb, N_BUF_ROWS, LANE), lambda n, c: (n, 0, 0)),
    ]
    args = [z3, z3, z3, par, mulo, w2, a2, buf]
    if state is not None:
        in_specs.append(pl.BlockSpec(state_block, state_map))
        args.append(state)
    aliases = {}
    for out_idx, prev in enumerate((prev_oa, prev_state)):
        if prev is not None:
            aliases[len(args)] = out_idx
            in_specs.append(pl.BlockSpec(memory_space=pl.ANY))
            args.append(prev)
    kern = functools.partial(_rwkv_kernel, C=C, nsb=nsb, nchunk=nchunk, ngroup=ngroup, valid_lo=valid_lo,
                             valid_hi=valid_hi, zero_init=state is None, n_alias=len(aliases))
    wide_scratch = pltpu.VMEM((R, D_MODEL), F32)
    unit_lanes = RWKV_UNIT_HEADS * A_HEAD
    return pl.pallas_call(
        kern,
        grid=(nseq // nsb, nchunk),
        in_specs=in_specs,
        out_specs=[
            pl.BlockSpec((NPAIR, R, LANE), lambda n, c: (0, rb(n, c), 0)),
            pl.BlockSpec(state_block, state_map),
            pl.BlockSpec((nsb, N_BUF_ROWS, LANE), lambda n, c: (n, 0, 0)),
        ],
        out_shape=[
            jax.ShapeDtypeStruct((NPAIR, z3.shape[1], LANE), F32),
            jax.ShapeDtypeStruct((depth, nseq, A_HEADS, A_HEAD, A_HEAD), F32),
            jax.ShapeDtypeStruct((nseq, N_BUF_ROWS, LANE), F32),
        ],
        input_output_aliases=aliases,
        scratch_shapes=[
            pltpu.VMEM((nsb * D_MODEL // unit_lanes, unit_lanes, unit_lanes), F32),
            pltpu.VMEM((nsb, N_BUF_ROWS, LANE), F32),
        ] + [wide_scratch] * 6,
        compiler_params=pltpu.CompilerParams(
            dimension_semantics=("arbitrary", "arbitrary"), vmem_limit_bytes=VMEM_LIMIT),
        name="rwkv_c%d" % C,
    )(*args)


def _gla_kernel(*refs, C, SB, nsb, nchunk, valid_lo, valid_hi, zero_init, n_alias):
    it = iter(refs)
    q_ref, k_ref, v_ref, zb_ref, gk_ref, gwh_ref, gwl_ref, nw_ref = (next(it) for _ in range(8))
    sin_ref = None if zero_init else next(it)
    for _ in range(n_alias):
        next(it)
    ob_ref, sout_ref, s_ref, att_ref = it
    c = pl.program_id(1)
    R = nsb * C
    units = [(s, h) for s in range(nsb) for h in range(B_HEADS)]
    QS = B_HK // LANE
    VS = B_HV // LANE

    def slot(u):
        return u[0] * B_HEADS + u[1]

    @pl.when(c == 0)
    def _init():
        for u in units:
            s_ref[slot(u)] = jnp.zeros((B_HK, B_HV), F32) if zero_init else sin_ref[0, u[0], u[1]]

    row = _iota((R, 1), 0)
    pos = c * C + (row & (C - 1))
    valid = (pos >= valid_lo) & (pos < valid_hi)

    glo1 = jnp.where(_iota((R, LANE), 1) == GATE_LORA, 1.0, gk_ref[0])
    g = jnp.where(valid, _log_sigmoid(_mm3(glo1, gwh_ref[...], gwl_ref[...])) / GATE_TAU, 0.0)
    ti = _iota((R, R), 0)
    tj = _iota((R, R), 1)
    tri = jnp.where((tj <= ti) & ((ti & -C) == (tj & -C)), 1.0, 0.0).astype(BF16)
    parts = _split2(g)
    b = _dot(tri, parts[0]) + _dot(tri, parts[1])
    eb = jnp.exp(b)
    b_col = []
    for s in range(nsb):
        ones_s = jnp.where((row >= s * C) & (row < (s + 1) * C), 1.0, 0.0).astype(BF16) * jnp.ones((1, LANE), BF16)
        b_col.append(_dot(parts[0], ones_s, TN) + _dot(parts[1], ones_s, TN))

    def part(x, u):
        s, h = u
        return x[s * C:(s + 1) * C, h * B_HK:(h + 1) * B_HK]

    def gather(ref, u, n):
        s, h = u
        return jnp.concatenate([ref[n * h + j, s * C:(s + 1) * C, :] for j in range(n)], axis=1)

    vrow = {u: valid[u[0] * C:(u[0] + 1) * C] for u in units}
    q = {u: gather(q_ref, u, QS) * (B_HK ** -0.5) for u in units}
    k = {u: jnp.where(vrow[u], gather(k_ref, u, QS), 0.0) for u in units}
    v = {u: jnp.where(vrow[u], gather(v_ref, u, VS), 0.0).astype(BF16) for u in units}
    bu = {u: part(b, u) for u in units}

    o = {u: _mm(q[u] * part(eb, u), s_ref[slot(u)]) for u in units}

    att_ref[...] = jnp.zeros(att_ref.shape, F32)
    nb = C // SB
    for i in range(1, nb):
        lo, hi = SB * i, SB * (i + 1)
        for u in units:
            beta = bu[u][lo - 1:lo]
            qi = q[u][lo:hi] * jnp.exp(bu[u][lo:hi] - beta)
            kj = k[u][:lo] * jnp.exp(beta - bu[u][:lo])
            att_ref[slot(u), :lo, lo:hi] = _mm(kj, qi, NT)

    srow = _iota((SB, 1), 0)

    def _diag_matmul():
        causal = srow <= _iota((SB, SB), 1)
        for i in range(nb):
            lo = SB * i
            for u in units:
                bb = bu[u][lo:lo + SB]
                qh = q[u][lo:lo + SB] * jnp.exp(bb - bb[0:1])
                kh = k[u][lo:lo + SB] * jnp.exp(bb[0:1] - bb)
                att_ref[slot(u), lo:lo + SB, lo:lo + SB] = jnp.where(causal, _mm(kh, qh, NT), 0.0)

    def _diag_exact():
        for i in range(nb):
            lo = SB * i
            for u in units:
                bb = bu[u][lo:lo + SB]
                kb = k[u][lo:lo + SB]
                for t in range(SB):
                    e = jnp.exp(jnp.where(srow <= t, bb[t:t + 1] - bb, -jnp.inf))
                    att_ref[slot(u), lo:lo + SB, lo + t:lo + t + 1] = jnp.sum(
                        q[u][lo + t:lo + t + 1] * kb * e, axis=-1, keepdims=True)

    if nb == 1:
        _diag_exact()
    else:
        span = jnp.max(jnp.concatenate(
            [b[s * C + SB * i:s * C + SB * i + 1] - b[s * C + SB * i + SB - 1:s * C + SB * i + SB]
             for s in range(nsb) for i in range(nb)], axis=0))
        pl.when(span < GLA_FACTOR_MAX_DECAY)(_diag_matmul)
        pl.when(span >= GLA_FACTOR_MAX_DECAY)(_diag_exact)

    for u in units:
        o[u] = o[u] + _dot(att_ref[slot(u)].astype(BF16), v[u], TN)

    for u in units:
        s, h = u
        ms = jnp.mean(o[u] * o[u], axis=-1, keepdims=True)
        on = o[u] * lax.rsqrt(ms + NORM_EPS) * nw_ref[...]
        for j in range(VS):
            ob_ref[VS * h + j, s * C:(s + 1) * C, :] = (
                on[:, j * LANE:(j + 1) * LANE] * _silu(zb_ref[VS * h + j, s * C:(s + 1) * C, :]))

    for u in units:
        s, h = u
        decay = jnp.exp(b_col[s][h * B_HK:(h + 1) * B_HK])
        decay = jnp.concatenate([decay] * VS, axis=1)
        kd = k[u] * jnp.exp(bu[u][C - 1:C] - bu[u])
        s_ref[slot(u)] = s_ref[slot(u)] * decay + _dot(kd.astype(BF16), v[u], TN)

    @pl.when(c == nchunk - 1)
    def _fin():
        for u in units:
            sout_ref[0, u[0], u[1]] = s_ref[slot(u)]


def _gla(z3, gw, nw, *, state, prev_ob, prev_state, layer, depth, nseq, nsb, nchunk, C, SB, row0, valid_lo,
         valid_hi):
    R = nsb * C
    rb = lambda n, c: row0 + c * (nseq // nsb) + n
    state_block = (1, nsb, B_HEADS, B_HK, B_HV)
    state_map = lambda n, c: (layer, n, 0, 0, 0)
    qs = B_HEADS * B_HK // LANE
    vs = B_HEADS * B_HV // LANE
    in_specs = [
        pl.BlockSpec((qs, R, LANE), lambda n, c: (SL_QB // qs, rb(n, c), 0)),
        pl.BlockSpec((qs, R, LANE), lambda n, c: (SL_KB // qs, rb(n, c), 0)),
        pl.BlockSpec((vs, R, LANE), lambda n, c: (SL_VB // vs, rb(n, c), 0)),
        pl.BlockSpec((vs, R, LANE), lambda n, c: (SL_ZB // vs, rb(n, c), 0)),
        pl.BlockSpec((1, R, LANE), lambda n, c: (SL_GK, rb(n, c), 0)),
        pl.BlockSpec((LANE, B_HEADS * B_HK), lambda n, c: (0, 0)),
        pl.BlockSpec((LANE, B_HEADS * B_HK), lambda n, c: (0, 0)),
        pl.BlockSpec((1, B_HV), lambda n, c: (0, 0)),
    ]
    args = [z3, z3, z3, z3, z3, *gw, nw]
    if state is not None:
        in_specs.append(pl.BlockSpec(state_block, state_map))
        args.append(state)
    aliases = {}
    for out_idx, prev in enumerate((prev_ob, prev_state)):
        if prev is not None:
            aliases[len(args)] = out_idx
            in_specs.append(pl.BlockSpec(memory_space=pl.ANY))
            args.append(prev)
    kern = functools.partial(_gla_kernel, C=C, SB=SB, nsb=nsb, nchunk=nchunk, valid_lo=valid_lo, valid_hi=valid_hi,
                             zero_init=state is None, n_alias=len(aliases))
    return pl.pallas_call(
        kern,
        grid=(nseq // nsb, nchunk),
        in_specs=in_specs,
        out_specs=[
            pl.BlockSpec((vs, R, LANE), lambda n, c: (0, rb(n, c), 0)),
            pl.BlockSpec(state_block, state_map),
        ],
        out_shape=[
            jax.ShapeDtypeStruct((NPAIR, z3.shape[1], LANE), F32),
            jax.ShapeDtypeStruct((depth, nseq, B_HEADS, B_HK, B_HV), F32),
        ],
        input_output_aliases=aliases,
        scratch_shapes=[
            pltpu.VMEM((nsb * B_HEADS, B_HK, B_HV), F32),
            pltpu.VMEM((nsb * B_HEADS, C, C), F32),
        ],
        compiler_params=pltpu.CompilerParams(
            dimension_semantics=("arbitrary", "arbitrary"), vmem_limit_bytes=VMEM_LIMIT),
        name="gla_c%d" % C,
    )(*args)


def _outproj_kernel(oa_ref, ob_ref, ga_ref, gb_ref, x_ref, w_ref, g_ref, o_ref, m_ref):
    for q in range(NPAIR):
        m = _sigmoid(ga_ref[q]) * oa_ref[q] + _sigmoid(gb_ref[q]) * ob_ref[q]
        m_ref[:, q * LANE:(q + 1) * LANE] = m.astype(BF16)
    y = jnp.dot(m_ref[...], w_ref[...], preferred_element_type=F32)
    ms = jnp.mean(y * y, axis=-1, keepdims=True)
    o_ref[...] = x_ref[...] + y * lax.rsqrt(ms + NORM_EPS) * g_ref[...]


def _outproj(oa3, ob3, z3, x, w, g, tile_m):
    m = x.shape[0]
    return pl.pallas_call(
        _outproj_kernel,
        grid=(m // tile_m,),
        in_specs=[
            pl.BlockSpec((NPAIR, tile_m, LANE), lambda i: (0, i, 0)),
            pl.BlockSpec((NPAIR, tile_m, LANE), lambda i: (0, i, 0)),
            pl.BlockSpec((NPAIR, tile_m, LANE), lambda i: (SL_GA // NPAIR, i, 0)),
            pl.BlockSpec((NPAIR, tile_m, LANE), lambda i: (SL_GB // NPAIR, i, 0)),
            pl.BlockSpec((tile_m, D_MODEL), lambda i: (i, 0)),
            pl.BlockSpec((D_MODEL, D_MODEL), lambda i: (0, 0)),
            pl.BlockSpec((1, D_MODEL), lambda i: (0, 0)),
        ],
        out_specs=pl.BlockSpec((tile_m, D_MODEL), lambda i: (i, 0)),
        out_shape=jax.ShapeDtypeStruct((m, D_MODEL), F32),
        scratch_shapes=[pltpu.VMEM((tile_m, D_MODEL), BF16)],
        compiler_params=pltpu.CompilerParams(
            dimension_semantics=("arbitrary",), vmem_limit_bytes=VMEM_LIMIT),
        name="outproj",
    )(oa3, ob3, z3, z3, x, w, g)


def _pick_tile(m, cap, mult):
    best = mult
    for t in range(mult, cap + 1, mult):
        if m % t == 0:
            best = t
    assert m % best == 0
    return best


def _pad_lanes(x):
    return jnp.pad(x, [(0, 0)] * (x.ndim - 1) + [(0, LANE - x.shape[-1])])


def _shift_rows(buf):
    n = buf.shape[0]
    rkv = buf[:, :3 * NPAIR].reshape(n, 3 * D_MODEL)
    return jnp.concatenate([rkv, buf[:, 3 * NPAIR, :LORA], buf[:, 3 * NPAIR + 1, :LORA]], axis=1)


def kernel(x_prompt, x_sample, state_rwkv, state_shift, state_gla, meta_tokens, norm_pre, w_in, rwkv_mu, rwkv_w0, rwkv_w2, rwkv_a0, rwkv_a2, rwkv_k_k, rwkv_k_a, rwkv_r_k, rwkv_lnx_w, rwkv_lnx_b, gla_gk_w2, gla_gk_b, gla_norm_w, w_out, norm_post):
    depth = w_in.shape[0]
    bp, seq, _ = x_prompt.shape
    bs, dseq, _ = x_sample.shape
    front = CHUNK - N_META
    tp = front + N_META + seq
    assert tp % CHUNK == 0 and dseq <= SAMPLE_ROWS and bs % RWKV_SAMPLE_SEQS == 0
    nchunk = tp // CHUNK
    mp = bp * tp
    ms = bs * SAMPLE_ROWS
    mtot = mp + ms
    tile_in = _pick_tile(mtot, 1184, 16)
    tile_out = _pick_tile(mtot, 296, 8)

    meta = jnp.broadcast_to(meta_tokens[None], (bp, N_META, D_MODEL))
    xp = jnp.concatenate([jnp.zeros((bp, front, D_MODEL), F32), meta, x_prompt], axis=1)
    xp = xp.reshape(bp, nchunk, CHUNK, D_MODEL).transpose(1, 0, 2, 3).reshape(mp, D_MODEL)
    xs = jnp.pad(x_sample, ((0, 0), (0, SAMPLE_ROWS - dseq), (0, 0))).reshape(ms, D_MODEL)
    x = jnp.concatenate([xp, xs], axis=0)

    w_packed = _pack_w_in(w_in)
    w_out_bf = w_out.astype(BF16)
    zero_buf = jnp.zeros((bp, N_BUF_ROWS, LANE), F32)

    sr_p = sr_s = sg_p = sg_s = None
    shift_p, shift_s = [], []
    for l in range(depth):
        mu = rwkv_mu[l]
        par = jnp.stack([mu[:D_MODEL], mu[D_MODEL:2 * D_MODEL], mu[2 * D_MODEL:3 * D_MODEL],
                         rwkv_w0[l], rwkv_a0[l], rwkv_k_k[l], rwkv_k_a[l], rwkv_r_k[l].reshape(-1),
                         rwkv_lnx_w[l], rwkv_lnx_b[l]], axis=0)
        par = jnp.pad(par, ((0, N_PAR_ROWS - par.shape[0]), (0, 0)))
        mulo = jnp.pad(jnp.stack([_pad_lanes(mu[3 * D_MODEL:3 * D_MODEL + LORA]),
                                  _pad_lanes(mu[3 * D_MODEL + LORA:])], axis=0), ((0, 6), (0, 0)))
        w2 = jnp.pad(rwkv_w2[l], ((0, LANE - LORA), (0, 0))).astype(BF16)
        a2 = jnp.pad(rwkv_a2[l], ((0, LANE - LORA), (0, 0))).astype(BF16)
        gw = jnp.concatenate([gla_gk_w2[l], gla_gk_b[l][None]], axis=0)
        gw = _split2(jnp.pad(gw, ((0, LANE - gw.shape[0]), (0, 0))))
        nw = gla_norm_w[l][None]
        sh = state_shift[l]
        buf_s = jnp.concatenate([sh[:, :3 * D_MODEL].reshape(bs, 3 * NPAIR, LANE),
                                 _pad_lanes(sh[:, 3 * D_MODEL:3 * D_MODEL + LORA])[:, None],
                                 _pad_lanes(sh[:, 3 * D_MODEL + LORA:])[:, None]], axis=1)
        buf_s = jnp.pad(buf_s, ((0, 0), (0, N_BUF_ROWS - buf_s.shape[1]), (0, 0)))

        z3 = _inproj(x, norm_pre[l][None], w_packed, l, tile_in)

        prompt = dict(layer=l, depth=depth, nseq=bp, nchunk=nchunk, C=CHUNK, row0=0, valid_lo=front, valid_hi=tp)
        sample = dict(layer=l, depth=depth, nseq=bs, nchunk=1, C=SAMPLE_ROWS, valid_lo=0, valid_hi=dseq)
        oa, sr_p, sh_p = _rwkv(z3, par, mulo, w2, a2, zero_buf, state=None, prev_oa=None, prev_state=sr_p,
                               nsb=RWKV_PROMPT_SEQS, ngroup=RWKV_LANE_GROUPS, **prompt)
        oa, sr_s, sh_s = _rwkv(z3, par, mulo, w2, a2, buf_s, state=state_rwkv, prev_oa=oa, prev_state=sr_s,
                               nsb=RWKV_SAMPLE_SEQS, ngroup=1, row0=mp // (SAMPLE_ROWS * RWKV_SAMPLE_SEQS),
                               **sample)
        ob, sg_p = _gla(z3, gw, nw, state=None, prev_ob=None, prev_state=sg_p, SB=GLA_SUB, nsb=GLA_PROMPT_SEQS,
                        **prompt)
        ob, sg_s = _gla(z3, gw, nw, state=state_gla, prev_ob=ob, prev_state=sg_s, SB=SAMPLE_ROWS,
                        nsb=GLA_SAMPLE_SEQS, row0=mp // (SAMPLE_ROWS * GLA_SAMPLE_SEQS), **sample)

        x = _outproj(oa, ob, z3, x, w_out_bf[l], norm_post[l][None], tile_out)
        shift_p.append(_shift_rows(sh_p))
        shift_s.append(_shift_rows(sh_s))

    y_prompt = x[:mp].reshape(nchunk, bp, CHUNK, D_MODEL).transpose(1, 0, 2, 3).reshape(bp, tp, D_MODEL)
    y_prompt = y_prompt[:, front + N_META:]
    y_sample = x[mp:].reshape(bs, SAMPLE_ROWS, D_MODEL)[:, :dseq]
    return (y_prompt, y_sample, sr_p, jnp.stack(shift_p), sg_p, sr_s, jnp.stack(shift_s), sg_s)
```
